```python
import jax
import jax.numpy as jnp
from jax import lax
import numpy as np

D_MODEL = 1024
BATCH = 8
SEQ = 2048
DEPTH = 2
DEC_BATCH = 128
DEC_SEQ = 1
PAST_LEN = 16384
PAGE_SIZE = 128

MIX_W = D_MODEL
GROUP_W = MIX_W // 4
CONF_K = 31
CONF_COLS = 2 * GROUP_W
RWKV_HEAD = 64
RWKV_H = GROUP_W // RWKV_HEAD
DECAY_LORA = 32
AAA_LORA = 32
GATE_LORA = 64
RWKV_SIZES = (GROUP_W, DECAY_LORA, GROUP_W, GROUP_W, AAA_LORA, GATE_LORA)
RWKV_COLS = GROUP_W * 3 + DECAY_LORA + AAA_LORA + GATE_LORA
RWKV_GN_EPS = 64e-5
SC_K = 3
SC_COLS = 3 * GROUP_W
POOL_WINDOWS = (2, 4, 8, 16)
POOL_G = 4
POOL_CH = GROUP_W // POOL_G
POOL_BUF = 15
POOL_COLS = GROUP_W
IN_SIZES = (CONF_COLS, RWKV_COLS, SC_COLS, POOL_COLS)
IN_COLS = CONF_COLS + RWKV_COLS + SC_COLS + POOL_COLS
N_KEYS = 128
N_EXPERTS = N_KEYS * N_KEYS
PEER_HEADS = 8
PEER_TOPK = 16
PEER_QDIM = 256
PEER_BLOCK = 128
RMS_EPS = 1e-6
LN_EPS = 1e-5

kernel_name = 'hybrid_conformer_rwkv7_shortconv_pool_peer_step'


def _offsets(sizes):
    out, acc = [], 0
    for s in sizes[:-1]:
        acc += s
        out.append(acc)
    return out


def rmsnorm(x, g):
    xf = x.astype(jnp.float32)
    y = xf * lax.rsqrt(jnp.mean(xf * xf, axis=-1, keepdims=True) + RMS_EPS) * g.astype(jnp.float32)
    return y.astype(x.dtype)


def layernorm(x, g, b, eps):
    mu = jnp.mean(x, axis=-1, keepdims=True)
    xc = x - mu
    var = jnp.mean(xc * xc, axis=-1, keepdims=True)
    return xc * lax.rsqrt(var + eps) * g + b


def causal_dwconv(u, buf, w):
    ext = jnp.concatenate([buf, u], axis=1)
    k = w.shape[0]
    y = lax.conv_general_dilated(ext, w[:, None, :], window_strides=(1,), padding='VALID',
                                 dimension_numbers=('NWC', 'WIO', 'NWC'),
                                 feature_group_count=u.shape[-1])
    return y, ext[:, ext.shape[1] - (k - 1):]


def multiscale_pool(u, buf, start_pos, pool_w, pool_scale):
    bsz, t, c = u.shape
    ext = jnp.concatenate([buf, u], axis=1)
    cs = jnp.concatenate([jnp.zeros((bsz, 1, c), jnp.float32), jnp.cumsum(ext, axis=1)], axis=1)
    pos = start_pos + jnp.arange(t)
    outs = []
    for gi, win in enumerate(POOL_WINDOWS):
        lo, hi = gi * POOL_CH, (gi + 1) * POOL_CH
        wsum = cs[:, POOL_BUF + 1:POOL_BUF + 1 + t, lo:hi] - cs[:, POOL_BUF + 1 - win:POOL_BUF + 1 - win + t, lo:hi]
        cnt = jnp.minimum(win, pos + 1).astype(jnp.float32)[None, :, None]
        pooled = wsum / cnt - u[:, :, lo:hi]
        outs.append(jnp.einsum('btc,cd->btd', pooled, pool_w[gi]))
    y = jnp.concatenate(outs, axis=-1) * pool_scale
    return y, ext[:, ext.shape[1] - POOL_BUF:]


def wkv7_scan(r, w, k, v, kk, a, s0):
    def step(s, inp):
        r_t, w_t, k_t, v_t, kk_t, a_t = inp
        sa = jnp.einsum('bhij,bhj->bhi', s, -kk_t)
        s = (s * w_t[:, :, None, :] + sa[..., None] * (kk_t * a_t)[:, :, None, :]
             + v_t[..., None] * k_t[:, :, None, :])
        y = jnp.einsum('bhij,bhj->bhi', s, r_t)
        return s, y
    xs = tuple(jnp.moveaxis(z, 1, 0) for z in (r, w, k, v, kk, a))
    s_fin, ys = lax.scan(step, s0, xs)
    return jnp.moveaxis(ys, 0, 1), s_fin


def rwkv7_time_mix(p, shift_buf, s0, wl):
    bsz, t, _ = p.shape
    prev = jnp.concatenate([shift_buf[:, None, :], p[:, :-1]], axis=1)
    xs = p + (prev - p) * wl['rwkv_mu']
    r, w_lo, k, v, a_lo, g_lo = jnp.split(xs, _offsets(RWKV_SIZES), axis=-1)
    w = -jax.nn.softplus(-(wl['rwkv_w0'] + jnp.matmul(jnp.tanh(w_lo), wl['rwkv_w2']))) - 0.5
    decay = jnp.exp(-jnp.exp(w))
    a = jax.nn.sigmoid(wl['rwkv_a0'] + jnp.matmul(a_lo, wl['rwkv_a2']))
    g = jnp.matmul(jax.nn.sigmoid(g_lo), wl['rwkv_g2'])

    def heads(z):
        return z.reshape(bsz, t, RWKV_H, RWKV_HEAD)

    kk = heads(k * wl['rwkv_k_k'])
    kk = kk * lax.rsqrt(jnp.maximum(jnp.sum(kk * kk, axis=-1, keepdims=True), 1e-24))
    k = k * (1.0 + (a - 1.0) * wl['rwkv_k_a'])
    rh, kh, vh, ah, dh = heads(r), heads(k), heads(v), heads(a), heads(decay)
    yh, s_fin = wkv7_scan(rh, dh, kh, vh, kk, ah, s0)
    yh = layernorm(yh, wl['rwkv_ln_g'].reshape(RWKV_H, RWKV_HEAD),
                   wl['rwkv_ln_b'].reshape(RWKV_H, RWKV_HEAD), RWKV_GN_EPS)
    yh = yh + jnp.sum(rh * kh * wl['rwkv_r_k'], axis=-1, keepdims=True) * vh
    y = yh.reshape(bsz, t, GROUP_W) * g
    return y, p[:, -1], s_fin


def token_mixers(h, bufs, start_pos, wl):
    f32 = jnp.float32
    conf_buf, shift_buf, wkv_s, sc_buf, pool_buf = bufs
    z = jnp.matmul(h, wl['w_in']).astype(f32)
    za, zb, zc, zd = jnp.split(z, _offsets(IN_SIZES), axis=-1)
    glu = za[..., :GROUP_W] * jax.nn.sigmoid(za[..., GROUP_W:])
    ca, new_conf = causal_dwconv(glu, conf_buf.astype(f32), wl['conf_dw_w'].astype(f32))
    ya = jax.nn.silu(layernorm(ca + wl['conf_dw_b'], wl['conf_ln_g'], wl['conf_ln_b'], LN_EPS))
    yb, new_shift, new_wkv = rwkv7_time_mix(zb, shift_buf.astype(f32), wkv_s.astype(f32), wl)
    gate_b, gate_c, hc = jnp.split(zc, 3, axis=-1)
    cc, new_sc = causal_dwconv(gate_c * hc, sc_buf.astype(f32), wl['sc_conv_w'].astype(f32))
    yc = gate_b * cc
    yd, new_pool = multiscale_pool(zd, pool_buf.astype(f32), start_pos,
                                   wl['pool_w'].astype(f32), wl['pool_scale'])
    y = jnp.concatenate([ya, yb, yc, yd], axis=-1).astype(h.dtype)
    out = jnp.matmul(y, wl['w_out'])
    dt = h.dtype
    return out, (new_conf.astype(dt), new_shift.astype(dt), new_wkv.astype(dt),
                 new_sc.astype(dt), new_pool.astype(dt))


def peer_ffn(x, wq, k1, k2, u_tab, v_tab):
    f32 = jnp.float32
    bsz, t, d = x.shape
    xt = x.reshape(-1, d)
    n = xt.shape[0]
    q = jnp.matmul(xt, wq).astype(f32).reshape(n, PEER_HEADS, 2, PEER_QDIM // 2)
    s1 = jnp.einsum('nhc,hkc->nhk', q[:, :, 0], k1.astype(f32))
    s2 = jnp.einsum('nhc,hkc->nhk', q[:, :, 1], k2.astype(f32))
    v1, i1 = lax.top_k(s1, PEER_TOPK)
    v2, i2 = lax.top_k(s2, PEER_TOPK)
    cand = (v1[..., :, None] + v2[..., None, :]).reshape(n, PEER_HEADS, PEER_TOPK * PEER_TOPK)
    sc, pos = lax.top_k(cand, PEER_TOPK)
    e = (jnp.take_along_axis(i1, pos // PEER_TOPK, axis=-1) * N_KEYS
         + jnp.take_along_axis(i2, pos % PEER_TOPK, axis=-1))
    g = jax.nn.softmax(sc, axis=-1)
    e = e.reshape(n, PEER_HEADS * PEER_TOPK)
    g = g.reshape(n, PEER_HEADS * PEER_TOPK)
    pad = (-n) % PEER_BLOCK
    xp = jnp.pad(xt, ((0, pad), (0, 0)))
    ep = jnp.pad(e, ((0, pad), (0, 0)))
    gp = jnp.pad(g, ((0, pad), (0, 0)))
    nb = (n + pad) // PEER_BLOCK

    def block(args):
        xb, eb, gb = args
        hsel = jnp.einsum('tsd,td->ts', u_tab[eb], xb).astype(f32)
        wgt = gb * jax.nn.gelu(hsel, approximate=False)
        return jnp.einsum('ts,tsd->td', wgt.astype(v_tab.dtype), v_tab[eb])

    out = lax.map(block, (xp.reshape(nb, PEER_BLOCK, d), ep.reshape(nb, PEER_BLOCK, -1),
                          gp.reshape(nb, PEER_BLOCK, -1)))
    return out.reshape(-1, d)[:n].reshape(bsz, t, d).astype(x.dtype)


def setup_inputs(seed: int = 0) -> dict:
    key = jax.random.key(seed)
    ks = list(jax.random.split(key, 40))
    f32 = jnp.float32

    def nrm(i, shape, scale):
        return jax.random.normal(ks[i], shape, f32) * scale

    L = DEPTH
    return {
        'x_prompt': nrm(0, (BATCH, SEQ, D_MODEL), 1.0),
        'x_sample': nrm(1, (DEC_BATCH, DEC_SEQ, D_MODEL), 1.0),
        'state_conformer': nrm(2, (L, DEC_BATCH, CONF_K - 1, GROUP_W), 0.5),
        'state_rwkv_shift': nrm(3, (L, DEC_BATCH, RWKV_COLS), 1.0),
        'state_rwkv_wkv': nrm(4, (L, DEC_BATCH, RWKV_H, RWKV_HEAD, RWKV_HEAD), 0.3),
        'state_shortconv': nrm(5, (L, DEC_BATCH, SC_K - 1, GROUP_W), 0.7),
        'state_pool': nrm(6, (L, DEC_BATCH, POOL_BUF, GROUP_W), 1.0),
        'norm1_g': 1.0 + nrm(7, (L, D_MODEL), 0.05),
        'norm2_g': 1.0 + nrm(8, (L, D_MODEL), 0.05),
        'final_norm_g': 1.0 + nrm(9, (D_MODEL,), 0.05),
        'w_in': nrm(10, (L, D_MODEL, IN_COLS), D_MODEL ** -0.5),
        'conf_dw_w': nrm(11, (L, CONF_K, GROUP_W), CONF_K ** -0.5),
        'conf_dw_b': nrm(12, (L, GROUP_W), 0.02),
        'conf_ln_g': 1.0 + nrm(13, (L, GROUP_W), 0.05),
        'conf_ln_b': nrm(14, (L, GROUP_W), 0.02),
        'rwkv_mu': jax.random.uniform(ks[15], (L, RWKV_COLS), f32),
        'rwkv_w0': jax.random.uniform(ks[16], (L, GROUP_W), f32, -4.0, 1.0),
        'rwkv_w2': nrm(17, (L, DECAY_LORA, GROUP_W), 0.1),
        'rwkv_a0': nrm(18, (L, GROUP_W), 0.1),
        'rwkv_a2': nrm(19, (L, AAA_LORA, GROUP_W), 0.5 * AAA_LORA ** -0.5),
        'rwkv_g2': nrm(20, (L, GATE_LORA, GROUP_W), GATE_LORA ** -0.5),
        'rwkv_k_k': 0.85 + nrm(21, (L, GROUP_W), 0.05),
        'rwkv_k_a': 1.0 + nrm(22, (L, GROUP_W), 0.05),
        'rwkv_r_k': nrm(23, (L, RWKV_H, RWKV_HEAD), 0.1),
        'rwkv_ln_g': 1.0 + nrm(24, (L, GROUP_W), 0.05),
        'rwkv_ln_b': nrm(25, (L, GROUP_W), 0.02),
        'sc_conv_w': nrm(26, (L, SC_K, GROUP_W), SC_K ** -0.5),
        'pool_w': nrm(27, (L, POOL_G, POOL_CH, POOL_CH), POOL_CH ** -0.5),
        'pool_scale': 1.0 + nrm(28, (L, GROUP_W), 0.1),
        'w_out': nrm(29, (L, D_MODEL, D_MODEL), D_MODEL ** -0.5),
        'peer_wq': nrm(30, (L, D_MODEL, PEER_HEADS * PEER_QDIM), D_MODEL ** -0.5),
        'peer_k1': nrm(31, (L, PEER_HEADS, N_KEYS, PEER_QDIM // 2), (PEER_QDIM // 2) ** -0.5),
        'peer_k2': nrm(32, (L, PEER_HEADS, N_KEYS, PEER_QDIM // 2), (PEER_QDIM // 2) ** -0.5),
        'peer_u': nrm(33, (L, N_EXPERTS, D_MODEL), D_MODEL ** -0.5),
        'peer_v': nrm(34, (L, N_EXPERTS, D_MODEL), PEER_HEADS ** -0.5),
    }


def reference(x_prompt, x_sample, state_conformer, state_rwkv_shift, state_rwkv_wkv,
              state_shortconv, state_pool, norm1_g, norm2_g, final_norm_g, w_in,
              conf_dw_w, conf_dw_b, conf_ln_g, conf_ln_b, rwkv_mu, rwkv_w0, rwkv_w2,
              rwkv_a0, rwkv_a2, rwkv_g2, rwkv_k_k, rwkv_k_a, rwkv_r_k, rwkv_ln_g, rwkv_ln_b,
              sc_conv_w, pool_w, pool_scale, w_out, peer_wq, peer_k1, peer_k2, peer_u, peer_v):
    layer_weights = {
        'w_in': w_in, 'conf_dw_w': conf_dw_w, 'conf_dw_b': conf_dw_b,
        'conf_ln_g': conf_ln_g, 'conf_ln_b': conf_ln_b, 'rwkv_mu': rwkv_mu,
        'rwkv_w0': rwkv_w0, 'rwkv_w2': rwkv_w2, 'rwkv_a0': rwkv_a0, 'rwkv_a2': rwkv_a2,
        'rwkv_g2': rwkv_g2, 'rwkv_k_k': rwkv_k_k, 'rwkv_k_a': rwkv_k_a, 'rwkv_r_k': rwkv_r_k,
        'rwkv_ln_g': rwkv_ln_g, 'rwkv_ln_b': rwkv_ln_b, 'sc_conv_w': sc_conv_w,
        'pool_w': pool_w, 'pool_scale': pool_scale, 'w_out': w_out,
    }

    def run(x, states, start_pos):
        new_states = ([], [], [], [], [])
        for l in range(DEPTH):
            wl = {name: arr[l] for name, arr in layer_weights.items()}
            bufs = tuple(s[l] for s in states)
            y, nb = token_mixers(rmsnorm(x, norm1_g[l]), bufs, start_pos, wl)
            x = x + y.astype(x.dtype)
            x = x + peer_ffn(rmsnorm(x, norm2_g[l]), peer_wq[l], peer_k1[l], peer_k2[l],
                             peer_u[l], peer_v[l])
            for lst, s in zip(new_states, nb):
                lst.append(s)
        stacked = tuple(jnp.stack(lst, axis=0) for lst in new_states)
        return rmsnorm(x, final_norm_g), stacked

    dt = x_prompt.dtype
    zero_states = (
        jnp.zeros((DEPTH, BATCH, CONF_K - 1, GROUP_W), dt),
        jnp.zeros((DEPTH, BATCH, RWKV_COLS), dt),
        jnp.zeros((DEPTH, BATCH, RWKV_H, RWKV_HEAD, RWKV_HEAD), dt),
        jnp.zeros((DEPTH, BATCH, SC_K - 1, GROUP_W), dt),
        jnp.zeros((DEPTH, BATCH, POOL_BUF, GROUP_W), dt),
    )
    y_prompt, (conf_p, shift_p, wkv_p, sc_p, pool_p) = run(x_prompt, zero_states, 0)
    y_sample, (conf_s, shift_s, wkv_s, sc_s, pool_s) = run(
        x_sample, (state_conformer, state_rwkv_shift, state_rwkv_wkv, state_shortconv, state_pool),
        PAST_LEN)
    return (y_prompt, y_sample, conf_p, conf_s, shift_p, shift_s, wkv_p, wkv_s, sc_p, sc_s, pool_p, pool_s)
```

```python
import functools

import jax
import jax.numpy as jnp
import numpy as np
from jax import lax
from jax.experimental import pallas as pl
from jax.experimental.pallas import tpu as pltpu

F32 = jnp.float32
BF16 = jnp.bfloat16

D_MODEL = 1024
GROUP_W = 256
CONF_K = 31
RWKV_HEAD = 64
RWKV_H = 4
RWKV_COLS = 896
RWKV_GN_EPS = 64e-5
POOL_WINDOWS = (2, 4, 8, 16)
POOL_BUF = 15
N_KEYS = 128
PEER_HEADS = 8
PEER_TOPK = 16
RMS_EPS = 1e-6
LN_EPS = 1e-5
IN_SPLITS = (512, 896, 768, 256)

VMEM_LIMIT_BYTES = 56 * 1024 * 1024
WKV_CHUNK = 64

_RWKV_PERM = np.concatenate([np.arange(0, 256), np.arange(288, 544), np.arange(544, 800),
                             np.arange(256, 288), np.arange(800, 832), np.arange(832, 896)])
_RWKV_INV_PERM = np.argsort(_RWKV_PERM)


def _cparams(*sem):
    return pltpu.CompilerParams(dimension_semantics=tuple(sem) if sem else None,
                                vmem_limit_bytes=VMEM_LIMIT_BYTES)


def _full(shape):
    n = len(shape)
    return pl.BlockSpec(shape, lambda *_: (0,) * n)


def _split2(x):
    hi = x.astype(BF16)
    lo = (x - hi.astype(F32)).astype(BF16)
    return hi, lo


def _split3(x):
    hi = x.astype(BF16)
    r = x - hi.astype(F32)
    mid = r.astype(BF16)
    lo = (r - mid.astype(F32)).astype(BF16)
    return hi, mid, lo


_NN = (((1,), (0,)), ((), ()))
_NT = (((1,), (1,)), ((), ()))


def _dg(a, b, dims):
    return lax.dot_general(a, b, dims, preferred_element_type=F32)


def _mm3(a, b, dims=_NN):
    ah, al = _split2(a)
    bh, bl = _split2(b)
    return _dg(ah, bh, dims) + (_dg(al, bh, dims) + _dg(ah, bl, dims))


def _mm_exact_rhs(a, b_bf16):
    h, m, l = _split3(a)
    return _dg(h, b_bf16, _NN) + (_dg(m, b_bf16, _NN) + _dg(l, b_bf16, _NN))


def _mm_exact_lhs(a_bf16, b):
    h, m, l = _split3(b)
    return _dg(a_bf16, h, _NN) + (_dg(a_bf16, m, _NN) + _dg(a_bf16, l, _NN))


def _transpose_mxu(x, eye_bf16):
    h, m, l = _split3(x)
    return _dg(eye_bf16, h, _NT) + (_dg(eye_bf16, m, _NT) + _dg(eye_bf16, l, _NT))


def _eye(n, dtype):
    return (lax.broadcasted_iota(jnp.int32, (n, n), 0) == lax.broadcasted_iota(jnp.int32, (n, n), 1)).astype(dtype)


def _rmsnorm(x, g):
    ms = jnp.mean(x * x, axis=-1, keepdims=True)
    return x * lax.rsqrt(ms + RMS_EPS) * g


def _sigmoid(x):
    return 1.0 / (1.0 + jnp.exp(-x))


def _softplus(x):
    return jnp.maximum(x, 0.0) + jnp.log(1.0 + jnp.exp(-jnp.abs(x)))


def _norm_proj_kernel(x_ref, g_ref, w_ref, *o_refs, splits):
    xb = _rmsnorm(x_ref[...], g_ref[...]).astype(BF16)
    off = 0
    for o_ref, wd in zip(o_refs, splits):
        o_ref[...] = jnp.dot(xb, w_ref[:, off:off + wd], preferred_element_type=F32)
        off += wd


def _norm_proj(x, g, w_bf16, splits, tm):
    n, d = x.shape
    tm = min(tm, n)
    cols = w_bf16.shape[1]
    return pl.pallas_call(
        functools.partial(_norm_proj_kernel, splits=splits),
        grid=(n // tm,),
        in_specs=[pl.BlockSpec((tm, d), lambda i: (i, 0)), _full((1, d)), _full((d, cols))],
        out_specs=[pl.BlockSpec((tm, wd), lambda i: (i, 0)) for wd in splits],
        out_shape=[jax.ShapeDtypeStruct((n, wd), F32) for wd in splits],
        compiler_params=_cparams("parallel"),
        name="norm_proj",
    )(x, g.reshape(1, d), w_bf16)


def _out_proj_kernel(x_ref, ya_ref, yb_ref, yc_ref, yd_ref, w_ref, o_ref):
    acc = x_ref[...]
    for i, y_ref in enumerate((ya_ref, yb_ref, yc_ref, yd_ref)):
        acc = acc + jnp.dot(y_ref[...].astype(BF16), w_ref[i * GROUP_W:(i + 1) * GROUP_W, :],
                            preferred_element_type=F32)
    o_ref[...] = acc


def _out_proj(x, ys, w_bf16, tm):
    n, d = x.shape
    tm = min(tm, n)
    row = lambda i: (i, 0)
    return pl.pallas_call(
        _out_proj_kernel,
        grid=(n // tm,),
        in_specs=[pl.BlockSpec((tm, d), row)] + [pl.BlockSpec((tm, GROUP_W), row)] * 4 + [_full((d, d))],
        out_specs=pl.BlockSpec((tm, d), row),
        out_shape=jax.ShapeDtypeStruct((n, d), F32),
        compiler_params=_cparams("parallel"),
        name="out_proj",
    )(x, *ys, w_bf16)


def _final_norm_kernel(x_ref, g_ref, o_ref):
    o_ref[...] = _rmsnorm(x_ref[...], g_ref[...])


def _final_norm(x, g, tm):
    n, d = x.shape
    tm = min(tm, n)
    return pl.pallas_call(
        _final_norm_kernel,
        grid=(n // tm,),
        in_specs=[pl.BlockSpec((tm, d), lambda i: (i, 0)), _full((1, d))],
        out_specs=pl.BlockSpec((tm, d), lambda i: (i, 0)),
        out_shape=jax.ShapeDtypeStruct((n, d), F32),
        compiler_params=_cparams("parallel"),
        name="final_norm",
    )(x, g.reshape(1, d))


def _layernorm_lanes(x, g, b, eps):
    mu = jnp.mean(x, axis=-1, keepdims=True)
    xc = x - mu
    var = jnp.mean(xc * xc, axis=-1, keepdims=True)
    return xc * lax.rsqrt(var + eps) * g + b


def _conformer_tail(ca, lng, lnb):
    y = _layernorm_lanes(ca, lng, lnb, LN_EPS)
    return y * _sigmoid(y)


def _pool_select(w2, w4, w8, w16, shape):
    lane = lax.broadcasted_iota(jnp.int32, shape, len(shape) - 1)
    wsum = jnp.where(lane < 64, w2, jnp.where(lane < 128, w4, jnp.where(lane < 192, w8, w16)))
    win = jnp.where(lane < 64, 2.0, jnp.where(lane < 128, 4.0, jnp.where(lane < 192, 8.0, 16.0)))
    return wsum, win


def _head_sum(x, ones_bd):
    return _mm_exact_rhs(x, ones_bd)


def _rwkv_prep(p, prev, wr):
    (mu, w0, w2, a0, a2, g2, k_k, k_a, r_k, ones_bd) = wr
    xs = p + (prev - p) * mu
    r = xs[:, 0:256]
    k = xs[:, 256:512]
    v = xs[:, 512:768]
    w_lo = xs[:, 768:800]
    a_lo = xs[:, 800:832]
    g_lo = xs[:, 832:896]
    wexp = -_softplus(-(w0 + jnp.dot(jnp.tanh(w_lo).astype(BF16), w2, preferred_element_type=F32))) - 0.5
    logdecay = -jnp.exp(wexp)
    a = _sigmoid(a0 + jnp.dot(a_lo.astype(BF16), a2, preferred_element_type=F32))
    g = jnp.dot(_sigmoid(g_lo).astype(BF16), g2, preferred_element_type=F32)
    kk = k * k_k
    kk = kk * lax.rsqrt(jnp.maximum(_head_sum(kk * kk, ones_bd), 1e-24))
    k2 = k * (1.0 + (a - 1.0) * k_a)
    bonus = _head_sum(r * k2 * r_k, ones_bd) * v
    return r, logdecay, k2, v, kk, a, g, bonus


def _rwkv_post(y, bonus, g, lng, lnb, ones_bd):
    mu = _head_sum(y, ones_bd) * (1.0 / RWKV_HEAD)
    yc = y - mu
    var = _head_sum(yc * yc, ones_bd) * (1.0 / RWKV_HEAD)
    yn = yc * lax.rsqrt(var + RWKV_GN_EPS) * lng + lnb
    return (yn + bonus) * g


_CONV_ROWS = 64


def _prompt_acd_kernel(za_ref, zc_ref, zd_ref, conf0_ref, sc0_ref, pool0_ref,
                       cw_ref, cb_ref, clg_ref, clb_ref, scw_ref, pw_ref, ps_ref,
                       ya_ref, yc_ref, yd_ref, conf_o_ref, sc_o_ref, pool_o_ref,
                       ext_a, ext_c, ext_d, *, tt, start_pos):
    t = pl.program_id(1)

    @pl.when(t == 0)
    def _():
        ext_a[0:2, :] = jnp.zeros((2, GROUP_W), F32)
        ext_a[2:32, :] = conf0_ref[0]
        ext_c[0:6, :] = jnp.zeros((6, GROUP_W), F32)
        ext_c[6:8, :] = sc0_ref[0]
        ext_d[0:1, :] = jnp.zeros((1, GROUP_W), F32)
        ext_d[1:16, :] = pool0_ref[0]

    za = za_ref[0]
    ext_a[32:32 + tt, :] = za[:, 0:GROUP_W] * _sigmoid(za[:, GROUP_W:2 * GROUP_W])
    for c in range(tt // _CONV_ROWS):
        base = c * _CONV_ROWS
        acc = jnp.zeros((_CONV_ROWS, GROUP_W), F32) + cb_ref[...]
        for k in range(CONF_K):
            acc = acc + cw_ref[k:k + 1, :] * ext_a[base + k + 2:base + k + 2 + _CONV_ROWS, :]
        ya_ref[0, base:base + _CONV_ROWS, :] = _conformer_tail(acc, clg_ref[...], clb_ref[...])
    conf_o_ref[0] = ext_a[tt + 2:tt + 32, :]
    ext_a[0:32, :] = ext_a[tt:tt + 32, :]

    zc = zc_ref[0]
    ext_c[8:8 + tt, :] = zc[:, GROUP_W:2 * GROUP_W] * zc[:, 2 * GROUP_W:3 * GROUP_W]
    cc = (scw_ref[0:1, :] * ext_c[6:6 + tt, :] + scw_ref[1:2, :] * ext_c[7:7 + tt, :]
          + scw_ref[2:3, :] * ext_c[8:8 + tt, :])
    yc_ref[0] = zc[:, 0:GROUP_W] * cc
    sc_o_ref[0] = ext_c[tt + 6:tt + 8, :]
    ext_c[0:8, :] = ext_c[tt:tt + 8, :]

    u = zd_ref[0]
    ext_d[16:16 + tt, :] = u
    w2 = u + ext_d[15:15 + tt, :]
    w4 = w2 + ext_d[14:14 + tt, :] + ext_d[13:13 + tt, :]
    w8 = w4
    for j in range(4, 8):
        w8 = w8 + ext_d[16 - j:16 - j + tt, :]
    w16 = w8
    for j in range(8, 16):
        w16 = w16 + ext_d[16 - j:16 - j + tt, :]
    wsum, win = _pool_select(w2, w4, w8, w16, (tt, GROUP_W))
    pos = (lax.broadcasted_iota(jnp.int32, (tt, GROUP_W), 0) + (t * tt + start_pos + 1)).astype(F32)
    pooled = wsum / jnp.minimum(win, pos) - u
    yd_ref[0] = jnp.dot(pooled.astype(BF16), pw_ref[...], preferred_element_type=F32) * ps_ref[...]
    pool_o_ref[0] = ext_d[tt + 1:tt + 16, :]
    ext_d[0:16, :] = ext_d[tt:tt + 16, :]


def _prompt_acd(za, zc, zd, conf0, sc0, pool0, wl, tt, start_pos):
    b, t, _ = za.shape
    tile = lambda w: pl.BlockSpec((1, tt, w), lambda i, j: (i, j, 0))
    st = lambda r: pl.BlockSpec((1, r, GROUP_W), lambda i, j: (i, 0, 0))
    row = _full((1, GROUP_W))
    return pl.pallas_call(
        functools.partial(_prompt_acd_kernel, tt=tt, start_pos=start_pos),
        grid=(b, t // tt),
        in_specs=[tile(512), tile(768), tile(256), st(30), st(2), st(15),
                  _full((CONF_K, GROUP_W)), row, row, row, _full((3, GROUP_W)), _full((GROUP_W, GROUP_W)), row],
        out_specs=[tile(256), tile(256), tile(256), st(30), st(2), st(15)],
        out_shape=[jax.ShapeDtypeStruct((b, t, GROUP_W), F32)] * 3
        + [jax.ShapeDtypeStruct((b, r, GROUP_W), F32) for r in (30, 2, 15)],
        scratch_shapes=[pltpu.VMEM((32 + tt, GROUP_W), F32), pltpu.VMEM((8 + tt, GROUP_W), F32),
                        pltpu.VMEM((16 + tt, GROUP_W), F32)],
        compiler_params=_cparams("parallel", "arbitrary"),
        name="prompt_acd",
    )(za, zc, zd, conf0, sc0, pool0, wl["conf_dw_w"], wl["conf_dw_b"], wl["conf_ln_g"], wl["conf_ln_b"],
      wl["sc_conv_w"], wl["pool_wbd"], wl["pool_scale"])


def _prompt_rwkv_kernel(zb_ref, shift0_ref, wkv0_ref, mu_ref, w0_ref, w2_ref, a0_ref, a2_ref, g2_ref,
                        kk_ref, ka_ref, rk_ref, lng_ref, lnb_ref, ones_ref, tri_ref,
                        yb_ref, shift_o_ref, wkv_o_ref, prev_s, st_s, *, c):
    t = pl.program_id(1)
    nt = pl.num_programs(1)

    @pl.when(t == 0)
    def _():
        prev_s[...] = jnp.broadcast_to(shift0_ref[0], prev_s.shape)
        for h in range(RWKV_H):
            st_s[h] = _transpose_mxu(wkv0_ref[0, h], _eye(RWKV_HEAD, BF16))

    p = zb_ref[0]
    row = lax.broadcasted_iota(jnp.int32, p.shape, 0)
    prev = jnp.where(row == 0, prev_s[0:1, :], pltpu.roll(p, 1, axis=0))
    ones_bd = ones_ref[...]
    wr = (mu_ref[...], w0_ref[...], w2_ref[...], a0_ref[...], a2_ref[...], g2_ref[...],
          kk_ref[...], ka_ref[...], rk_ref[...], ones_bd)
    r, logdecay, k2, v, kk, a, g, bonus = _rwkv_prep(p, prev, wr)

    cum = _mm_exact_lhs(tri_ref[...], logdecay)
    cum_end = cum[c - 1:c, :]
    e_neg = jnp.exp(-cum)
    e_end = jnp.exp(cum_end - cum)
    bvec = kk * a
    a_t = -kk * jnp.exp(cum - logdecay)
    r_t = r * jnp.exp(cum)
    b_h = bvec * e_neg
    k_h = k2 * e_neg
    b_e = bvec * e_end
    k_e = k2 * e_end
    g_end = jnp.exp(cum_end)

    ri = lax.broadcasted_iota(jnp.int32, (c, c), 0)
    ci = lax.broadcasted_iota(jnp.int32, (c, c), 1)
    strict = ri > ci
    incl = ri >= ci
    eye = (ri == ci).astype(F32)
    eye_h = _eye(RWKV_HEAD, BF16)
    eye_2h = _eye(2 * RWKV_HEAD, BF16)

    ys = []
    for h in range(RWKV_H):
        sl = slice(h * RWKV_HEAD, (h + 1) * RWKV_HEAD)
        at_h, rt_h, bh_h, kh_h, be_h, ke_h, v_h = (z[:, sl] for z in (a_t, r_t, b_h, k_h, b_e, k_e, v))
        a_ab = jnp.where(strict, _mm3(at_h, bh_h, _NT), 0.0)
        a_ak = jnp.where(strict, _mm3(at_h, kh_h, _NT), 0.0)
        a_rb = jnp.where(incl, _mm3(rt_h, bh_h, _NT), 0.0)
        a_rk = jnp.where(incl, _mm3(rt_h, kh_h, _NT), 0.0)
        x = a_ab
        tinv = eye + x
        steps = int(np.log2(c)) - 1
        for _ in range(steps):
            x = _mm3(x, x)
            tinv = tinv + _mm3(tinv, x)
        s0 = st_s[h]
        rhs = _mm3(at_h, s0) + _mm3(a_ak, v_h)
        sa = _mm3(tinv, rhs)
        ys.append(_mm3(rt_h, s0) + _mm3(a_rb, sa) + _mm3(a_rk, v_h))
        bk_t = _transpose_mxu(jnp.concatenate([be_h, ke_h], axis=1), eye_2h)
        s_new = (_mm3(_eye(RWKV_HEAD, F32) * g_end[:, sl], s0) + _mm3(bk_t[0:RWKV_HEAD], sa)
                 + _mm3(bk_t[RWKV_HEAD:2 * RWKV_HEAD], v_h))
        st_s[h] = s_new

        @pl.when(t == nt - 1)
        def _():
            wkv_o_ref[0, h] = _transpose_mxu(s_new, eye_h)

    y = jnp.concatenate(ys, axis=-1)
    yb_ref[0] = _rwkv_post(y, bonus, g, lng_ref[...], lnb_ref[...], ones_bd)
    prev_s[...] = jnp.broadcast_to(p[c - 1:c, :], prev_s.shape)
    shift_o_ref[0] = p[c - 1:c, :]


def _rwkv_weight_args(wl):
    return (wl["rwkv_mu"], wl["rwkv_w0"], wl["rwkv_w2"], wl["rwkv_a0"], wl["rwkv_a2"], wl["rwkv_g2"],
            wl["rwkv_k_k"], wl["rwkv_k_a"], wl["rwkv_r_k"], wl["rwkv_ln_g"], wl["rwkv_ln_b"], wl["ones_bd"])


_RWKV_WEIGHT_SPECS = [(1, RWKV_COLS), (1, GROUP_W), (32, GROUP_W), (1, GROUP_W), (32, GROUP_W), (64, GROUP_W),
                      (1, GROUP_W), (1, GROUP_W), (1, GROUP_W), (1, GROUP_W), (1, GROUP_W), (GROUP_W, GROUP_W)]


def _prompt_rwkv(zb, shift0, wkv0, wl):
    b, t, _ = zb.shape
    c = WKV_CHUNK
    tri = jnp.tril(jnp.ones((c, c), F32)).astype(BF16)
    return pl.pallas_call(
        functools.partial(_prompt_rwkv_kernel, c=c),
        grid=(b, t // c),
        in_specs=[pl.BlockSpec((1, c, RWKV_COLS), lambda i, j: (i, j, 0)),
                  pl.BlockSpec((1, 1, RWKV_COLS), lambda i, j: (i, 0, 0)),
                  pl.BlockSpec((1, RWKV_H, RWKV_HEAD, RWKV_HEAD), lambda i, j: (i, 0, 0, 0))]
        + [_full(s) for s in _RWKV_WEIGHT_SPECS] + [_full((c, c))],
        out_specs=[pl.BlockSpec((1, c, GROUP_W), lambda i, j: (i, j, 0)),
                   pl.BlockSpec((1, 1, RWKV_COLS), lambda i, j: (i, 0, 0)),
                   pl.BlockSpec((1, RWKV_H, RWKV_HEAD, RWKV_HEAD), lambda i, j: (i, 0, 0, 0))],
        out_shape=[jax.ShapeDtypeStruct((b, t, GROUP_W), F32), jax.ShapeDtypeStruct((b, 1, RWKV_COLS), F32),
                   jax.ShapeDtypeStruct((b, RWKV_H, RWKV_HEAD, RWKV_HEAD), F32)],
        scratch_shapes=[pltpu.VMEM((8, RWKV_COLS), F32), pltpu.VMEM((RWKV_H, RWKV_HEAD, RWKV_HEAD), F32)],
        compiler_params=_cparams("parallel", "arbitrary"),
        name="prompt_rwkv",
    )(zb, shift0, wkv0, *_rwkv_weight_args(wl), tri)


def _decode_acd_prep_kernel(za_ref, zb_ref, zc_ref, zd_ref, conf_ref, shift_ref, sc_ref, pool_ref,
                            cw_ref, cb_ref, clg_ref, clb_ref, scw_ref, pw_ref, ps_ref,
                            mu_ref, w0_ref, w2_ref, a0_ref, a2_ref, g2_ref, kk_ref, ka_ref, rk_ref, ones_ref,
                            ya_ref, yc_ref, yd_ref, conf_o_ref, sc_o_ref, pool_o_ref,
                            r_o, w_o, k_o, v_o, kkn_o, b_o, g_o, bonus_o, *, start_pos):
    za = za_ref[...]
    glu = za[:, 0:GROUP_W] * _sigmoid(za[:, GROUP_W:2 * GROUP_W])
    acc = cb_ref[...] + cw_ref[CONF_K - 1:CONF_K, :] * glu
    for k in range(CONF_K - 1):
        acc = acc + cw_ref[k:k + 1, :] * conf_ref[k]
    ya_ref[...] = _conformer_tail(acc, clg_ref[...], clb_ref[...])
    for k in range(CONF_K - 2):
        conf_o_ref[k] = conf_ref[k + 1]
    conf_o_ref[CONF_K - 2] = glu

    zc = zc_ref[...]
    u = zc[:, GROUP_W:2 * GROUP_W] * zc[:, 2 * GROUP_W:3 * GROUP_W]
    cc = scw_ref[0:1, :] * sc_ref[0] + scw_ref[1:2, :] * sc_ref[1] + scw_ref[2:3, :] * u
    yc_ref[...] = zc[:, 0:GROUP_W] * cc
    sc_o_ref[0] = sc_ref[1]
    sc_o_ref[1] = u

    d = zd_ref[...]
    w2 = d + pool_ref[POOL_BUF - 1]
    w4 = w2 + pool_ref[POOL_BUF - 2] + pool_ref[POOL_BUF - 3]
    w8 = w4
    for j in range(4, 8):
        w8 = w8 + pool_ref[POOL_BUF - j]
    w16 = w8
    for j in range(8, 16):
        w16 = w16 + pool_ref[POOL_BUF - j]
    wsum, win = _pool_select(w2, w4, w8, w16, d.shape)
    pooled = wsum / jnp.minimum(win, float(start_pos + 1)) - d
    yd_ref[...] = jnp.dot(pooled.astype(BF16), pw_ref[...], preferred_element_type=F32) * ps_ref[...]
    for k in range(POOL_BUF - 1):
        pool_o_ref[k] = pool_ref[k + 1]
    pool_o_ref[POOL_BUF - 1] = d

    wr = (mu_ref[...], w0_ref[...], w2_ref[...], a0_ref[...], a2_ref[...], g2_ref[...],
          kk_ref[...], ka_ref[...], rk_ref[...], ones_ref[...])
    r, logdecay, k2, v, kk, a, g, bonus = _rwkv_prep(zb_ref[...], shift_ref[...], wr)
    r_o[...] = r
    w_o[...] = jnp.exp(logdecay)
    k_o[...] = k2
    v_o[...] = v
    kkn_o[...] = kk
    b_o[...] = kk * a
    g_o[...] = g
    bonus_o[...] = bonus


def _decode_acd_prep(za, zb, zc, zd, conf_t, shift, sc_t, pool_t, wl, start_pos):
    n = za.shape[0]
    ins = (za, zb, zc, zd, conf_t, shift, sc_t, pool_t, wl["conf_dw_w"], wl["conf_dw_b"], wl["conf_ln_g"],
           wl["conf_ln_b"], wl["sc_conv_w"], wl["pool_wbd"], wl["pool_scale"]) + _rwkv_weight_args(wl)[:9] + (wl["ones_bd"],)
    vec = jax.ShapeDtypeStruct((n, GROUP_W), F32)
    outs = [vec, vec, vec, jax.ShapeDtypeStruct(conf_t.shape, F32), jax.ShapeDtypeStruct(sc_t.shape, F32),
            jax.ShapeDtypeStruct(pool_t.shape, F32)] + [vec] * 8
    return pl.pallas_call(
        functools.partial(_decode_acd_prep_kernel, start_pos=start_pos),
        in_specs=[_full(x.shape) for x in ins],
        out_specs=[_full(o.shape) for o in outs],
        out_shape=outs,
        compiler_params=_cparams(),
        name="decode_acd_prep",
    )(*ins)


def _decode_wkv_kernel(s_ref, w_ref, kk_ref, b_ref, k_ref, r_ref, v_ref, s_o_ref, y_o_ref):
    s = s_ref[...]
    sa = -jnp.sum(s * kk_ref[...], axis=-1, keepdims=True)
    s_new = s * w_ref[...] + sa * b_ref[...] + v_ref[...] * k_ref[...]
    s_o_ref[...] = s_new
    y_o_ref[...] = jnp.sum(s_new * r_ref[...], axis=-1, keepdims=True)


def _decode_wkv(s, w, kk, bvec, k, r, v, blk):
    bh = s.shape[0]
    lane = pl.BlockSpec((blk, 1, RWKV_HEAD), lambda i: (i, 0, 0))
    col = pl.BlockSpec((blk, RWKV_HEAD, 1), lambda i: (i, 0, 0))
    mat = pl.BlockSpec((blk, RWKV_HEAD, RWKV_HEAD), lambda i: (i, 0, 0))
    return pl.pallas_call(
        _decode_wkv_kernel,
        grid=(bh // blk,),
        in_specs=[mat, lane, lane, lane, lane, lane, col],
        out_specs=[mat, col],
        out_shape=[jax.ShapeDtypeStruct(s.shape, F32), jax.ShapeDtypeStruct((bh, RWKV_HEAD, 1), F32)],
        compiler_params=_cparams("parallel"),
        name="decode_wkv",
    )(s, w, kk, bvec, k, r, v)


def _decode_post_kernel(y_ref, bonus_ref, g_ref, lng_ref, lnb_ref, ones_ref, o_ref):
    o_ref[...] = _rwkv_post(y_ref[...], bonus_ref[...], g_ref[...], lng_ref[...], lnb_ref[...], ones_ref[...])


def _decode_post(y, bonus, g, wl):
    ins = (y, bonus, g, wl["rwkv_ln_g"], wl["rwkv_ln_b"], wl["ones_bd"])
    return pl.pallas_call(
        _decode_post_kernel,
        in_specs=[_full(x.shape) for x in ins],
        out_specs=_full(y.shape),
        out_shape=jax.ShapeDtypeStruct(y.shape, F32),
        compiler_params=_cparams(),
        name="decode_post",
    )(*ins)


_CELLS = [(a, b) for a in range(PEER_TOPK) for b in range(PEER_TOPK) if (a + 1) * (b + 1) <= PEER_TOPK]
_CELL_PAIRS = [(c, d) for c in _CELLS for d in _CELLS if d[0] < c[0] and d[1] > c[1]]


def _top16_rows(s, tb, vals_ref, h):
    iota = lax.broadcasted_iota(jnp.int32, (N_KEYS, tb), 0).astype(F32)
    rank = jnp.full((N_KEYS, tb), float(PEER_TOPK), F32)
    for r in range(PEER_TOPK):
        m = jnp.max(s, axis=0, keepdims=True)
        idx = jnp.min(jnp.where(s == m, iota, float(N_KEYS)), axis=0, keepdims=True)
        sel = iota == idx
        rank = jnp.where(sel, float(r), rank)
        s = jnp.where(sel, -jnp.inf, s)
        vals_ref[r, h:h + 1, :] = m
    return rank


def _peer_route_kernel(h_ref, g_ref, wq_ref, k1_ref, k2_ref,
                       xn_ref, r2_ref, p2_ref, c1_ref, p1_ref,
                       v1_s, v2_s, r1_s, cnt_s, *, tb):
    xb = _rmsnorm(h_ref[...], g_ref[...]).astype(BF16)
    xn_ref[...] = xb
    for h in range(PEER_HEADS):
        q = jnp.dot(xb, wq_ref[:, h * 256:(h + 1) * 256], preferred_element_type=F32).astype(BF16)
        s1 = _dg(k1_ref[h], q[:, 0:128], _NT)
        s2 = _dg(k2_ref[h], q[:, 128:256], _NT)
        r1_s[h] = _top16_rows(s1, tb, v1_s, h)
        r2_ref[h] = _top16_rows(s2, tb, v2_s, h)
        p1_ref[h] = jnp.exp(s1 - v1_s[0, h:h + 1, :])
        p2_ref[h] = jnp.exp(s2 - v2_s[0, h:h + 1, :])

    sums = {c: v1_s[c[0]] + v2_s[c[1]] for c in _CELLS}
    rank = {c: jnp.full((PEER_HEADS, tb), float((c[0] + 1) * (c[1] + 1) - 1), F32) for c in _CELLS}
    for c, d in _CELL_PAIRS:
        ge = sums[d] >= sums[c]
        rank[c] = rank[c] + jnp.where(ge, 1.0, 0.0)
        rank[d] = rank[d] + jnp.where(ge, 0.0, 1.0)
    e1 = [jnp.exp(v1_s[a] - v1_s[0]) for a in range(PEER_TOPK)]
    e2 = [jnp.exp(v2_s[b] - v2_s[0]) for b in range(PEER_TOPK)]
    z = jnp.zeros((PEER_HEADS, tb), F32)
    cnt = [jnp.zeros((PEER_HEADS, tb), F32) for _ in range(PEER_TOPK)]
    for c in _CELLS:
        sel = rank[c] < float(PEER_TOPK)
        cnt[c[0]] = cnt[c[0]] + jnp.where(sel, 1.0, 0.0)
        z = z + jnp.where(sel, e1[c[0]] * e2[c[1]], 0.0)
    for a in range(PEER_TOPK):
        cnt_s[a] = cnt[a]
    cnt_s[PEER_TOPK] = 1.0 / z

    for h in range(PEER_HEADS):
        r1 = r1_s[h]
        c1 = jnp.zeros((N_KEYS, tb), F32)
        for a in range(PEER_TOPK):
            c1 = c1 + jnp.where(r1 == float(a), cnt_s[a, h:h + 1, :], 0.0)
        c1_ref[h] = c1
        p1_ref[h] = p1_ref[h] * cnt_s[PEER_TOPK, h:h + 1, :]


def _peer_route(hres, g, wq_bf16, k1_bf16, k2_bf16, tb):
    n, d = hres.shape
    tb = min(tb, n)
    gate = pl.BlockSpec((PEER_HEADS, N_KEYS, tb), lambda i: (0, 0, i))
    gshape = jax.ShapeDtypeStruct((PEER_HEADS, N_KEYS, n), F32)
    return pl.pallas_call(
        functools.partial(_peer_route_kernel, tb=tb),
        grid=(n // tb,),
        in_specs=[pl.BlockSpec((tb, d), lambda i: (i, 0)), _full((1, d)), _full(wq_bf16.shape),
                  _full(k1_bf16.shape), _full(k2_bf16.shape)],
        out_specs=[pl.BlockSpec((tb, d), lambda i: (i, 0)), gate, gate, gate, gate],
        out_shape=[jax.ShapeDtypeStruct((n, d), BF16), gshape, gshape, gshape, gshape],
        scratch_shapes=[pltpu.VMEM((PEER_TOPK, PEER_HEADS, tb), F32), pltpu.VMEM((PEER_TOPK, PEER_HEADS, tb), F32),
                        pltpu.VMEM((PEER_HEADS, N_KEYS, tb), F32), pltpu.VMEM((PEER_TOPK + 1, PEER_HEADS, tb), F32)],
        compiler_params=_cparams("parallel"),
        name="peer_route",
    )(hres, g.reshape(1, d), wq_bf16, k1_bf16, k2_bf16)


_SQRT_HALF = float(np.sqrt(0.5))


def _peer_expert_kernel(xn_ref, hres_ref, u_ref, vt_ref, r2_ref, p2_ref, c1_ref, p1_ref, o_ref, acc_s, *, eb):
    j = pl.program_id(1)

    @pl.when(j == 0)
    def _():
        acc_s[...] = jnp.zeros_like(acc_s)

    xb = xn_ref[...]
    for i in range(eb // N_KEYS):
        i1 = j * (eb // N_KEYS) + i
        ht = _dg(u_ref[i * N_KEYS:(i + 1) * N_KEYS, :], xb, _NT)
        gt = jnp.zeros_like(ht)
        for h in range(PEER_HEADS):
            c1 = c1_ref[h, pl.ds(i1, 1), :]
            p1 = p1_ref[h, pl.ds(i1, 1), :]
            gt = gt + jnp.where(r2_ref[h] < c1, p2_ref[h], 0.0) * p1
        act = 0.5 * ht * (1.0 + lax.erf(ht * _SQRT_HALF))
        at = (act * gt).astype(BF16)
        acc_s[...] += jnp.dot(vt_ref[:, i * N_KEYS:(i + 1) * N_KEYS], at, preferred_element_type=F32)

    @pl.when(j == pl.num_programs(1) - 1)
    def _():
        o_ref[...] = hres_ref[...] + acc_s[...].T


def _peer_experts(xn, hres, u_bf16, vt_bf16, gates, tb, eb):
    n, d = hres.shape
    tb = min(tb, n)
    e = u_bf16.shape[0]
    gate = pl.BlockSpec((PEER_HEADS, N_KEYS, tb), lambda i, j: (0, 0, i))
    return pl.pallas_call(
        functools.partial(_peer_expert_kernel, eb=eb),
        grid=(n // tb, e // eb),
        in_specs=[pl.BlockSpec((tb, d), lambda i, j: (i, 0)), pl.BlockSpec((tb, d), lambda i, j: (i, 0)),
                  pl.BlockSpec((eb, d), lambda i, j: (j, 0)), pl.BlockSpec((d, eb), lambda i, j: (0, j)),
                  gate, gate, gate, gate],
        out_specs=pl.BlockSpec((tb, d), lambda i, j: (i, 0)),
        out_shape=jax.ShapeDtypeStruct((n, d), F32),
        scratch_shapes=[pltpu.VMEM((d, tb), F32)],
        compiler_params=_cparams("parallel", "arbitrary"),
        name="peer_experts",
    )(xn, hres, u_bf16, vt_bf16, *gates)


def _layer_weights(l, w):
    row = lambda a: a[l].reshape(1, -1)
    perm = _RWKV_PERM
    w_in = w["w_in"][l]
    w_in = jnp.concatenate([w_in[:, 0:512], w_in[:, 512:1408][:, perm], w_in[:, 1408:]], axis=1).astype(BF16)
    eye4 = jnp.eye(RWKV_H, dtype=F32)
    ones_bd = jnp.kron(eye4, jnp.ones((RWKV_HEAD, RWKV_HEAD), F32)).astype(BF16)
    pool_wbd = jax.scipy.linalg.block_diag(*[w["pool_w"][l, gi] for gi in range(4)]).astype(BF16)
    return dict(
        norm1_g=w["norm1_g"][l], norm2_g=w["norm2_g"][l], w_in=w_in,
        conf_dw_w=w["conf_dw_w"][l], conf_dw_b=row(w["conf_dw_b"]), conf_ln_g=row(w["conf_ln_g"]),
        conf_ln_b=row(w["conf_ln_b"]), sc_conv_w=w["sc_conv_w"][l], pool_wbd=pool_wbd, pool_scale=row(w["pool_scale"]),
        rwkv_mu=w["rwkv_mu"][l][perm].reshape(1, -1), rwkv_w0=row(w["rwkv_w0"]), rwkv_w2=w["rwkv_w2"][l].astype(BF16),
        rwkv_a0=row(w["rwkv_a0"]), rwkv_a2=w["rwkv_a2"][l].astype(BF16), rwkv_g2=w["rwkv_g2"][l].astype(BF16),
        rwkv_k_k=row(w["rwkv_k_k"]), rwkv_k_a=row(w["rwkv_k_a"]), rwkv_r_k=row(w["rwkv_r_k"]),
        rwkv_ln_g=row(w["rwkv_ln_g"]), rwkv_ln_b=row(w["rwkv_ln_b"]), ones_bd=ones_bd,
        w_out=w["w_out"][l].astype(BF16), peer_wq=w["peer_wq"][l].astype(BF16),
        peer_k1=w["peer_k1"][l].astype(BF16), peer_k2=w["peer_k2"][l].astype(BF16),
        peer_u=w["peer_u"][l].astype(BF16), peer_vt=w["peer_v"][l].T.astype(BF16),
    )


def _peer_block(hres, wl, tb_route, tb_exp, eb):
    xn, r2, p2, c1, p1 = _peer_route(hres, wl["norm2_g"], wl["peer_wq"], wl["peer_k1"], wl["peer_k2"], tb_route)
    return _peer_experts(xn, hres, wl["peer_u"], wl["peer_vt"], (r2, p2, c1, p1), tb_exp, eb)


def _prompt_layer(x, wl, bsz, t):
    n = bsz * t
    za, zb, zc, zd = _norm_proj(x, wl["norm1_g"], wl["w_in"], IN_SPLITS, 512)
    z3 = lambda a: a.reshape(bsz, t, -1)
    zeros = lambda *s: jnp.zeros(s, F32)
    ya, yc, yd, conf, sc, pool = _prompt_acd(z3(za), z3(zc), z3(zd), zeros(bsz, 30, GROUP_W), zeros(bsz, 2, GROUP_W),
                                             zeros(bsz, 15, GROUP_W), wl, 256, 0)
    yb, shift, wkv = _prompt_rwkv(z3(zb), zeros(bsz, 1, RWKV_COLS), zeros(bsz, RWKV_H, RWKV_HEAD, RWKV_HEAD), wl)
    flat = lambda a: a.reshape(n, GROUP_W)
    hres = _out_proj(x, (flat(ya), flat(yb), flat(yc), flat(yd)), wl["w_out"], 512)
    x = _peer_block(hres, wl, 256, 512, 1024)
    return x, (conf, shift.reshape(bsz, RWKV_COLS)[:, _RWKV_INV_PERM], wkv, sc, pool)


def _decode_layer(x, states, wl, start_pos):
    conf, shift, wkv, sc, pool = states
    n = x.shape[0]
    za, zb, zc, zd = _norm_proj(x, wl["norm1_g"], wl["w_in"], IN_SPLITS, 128)
    tr = lambda a: jnp.transpose(a, (1, 0, 2))
    (ya, yc, yd, conf_n, sc_n, pool_n, r, w, k2, v, kk, bvec, g, bonus) = _decode_acd_prep(
        za, zb, zc, zd, tr(conf), shift[:, _RWKV_PERM], tr(sc), tr(pool), wl, start_pos)
    bh = n * RWKV_H
    lane = lambda a: a.reshape(bh, 1, RWKV_HEAD)
    s_new, y = _decode_wkv(wkv.reshape(bh, RWKV_HEAD, RWKV_HEAD), lane(w), lane(kk), lane(bvec), lane(k2), lane(r),
                           v.reshape(bh, RWKV_HEAD, 1), 64)
    yb = _decode_post(y.reshape(n, GROUP_W), bonus, g, wl)
    hres = _out_proj(x, (ya, yb, yc, yd), wl["w_out"], 128)
    x = _peer_block(hres, wl, 128, 128, 1024)
    return x, (tr(conf_n), zb[:, _RWKV_INV_PERM], s_new.reshape(n, RWKV_H, RWKV_HEAD, RWKV_HEAD), tr(sc_n), tr(pool_n))


def kernel(x_prompt, x_sample, state_conformer, state_rwkv_shift, state_rwkv_wkv, state_shortconv, state_pool, norm1_g, norm2_g, final_norm_g, w_in, conf_dw_w, conf_dw_b, conf_ln_g, conf_ln_b, rwkv_mu, rwkv_w0, rwkv_w2, rwkv_a0, rwkv_a2, rwkv_g2, rwkv_k_k, rwkv_k_a, rwkv_r_k, rwkv_ln_g, rwkv_ln_b, sc_conv_w, pool_w, pool_scale, w_out, peer_wq, peer_k1, peer_k2, peer_u, peer_v):
    w = dict(norm1_g=norm1_g, norm2_g=norm2_g, w_in=w_in, conf_dw_w=conf_dw_w, conf_dw_b=conf_dw_b,
             conf_ln_g=conf_ln_g, conf_ln_b=conf_ln_b, rwkv_mu=rwkv_mu, rwkv_w0=rwkv_w0, rwkv_w2=rwkv_w2,
             rwkv_a0=rwkv_a0, rwkv_a2=rwkv_a2, rwkv_g2=rwkv_g2, rwkv_k_k=rwkv_k_k, rwkv_k_a=rwkv_k_a,
             rwkv_r_k=rwkv_r_k, rwkv_ln_g=rwkv_ln_g, rwkv_ln_b=rwkv_ln_b, sc_conv_w=sc_conv_w, pool_w=pool_w,
             pool_scale=pool_scale, w_out=w_out, peer_wq=peer_wq, peer_k1=peer_k1, peer_k2=peer_k2,
             peer_u=peer_u, peer_v=peer_v)
    depth = w_in.shape[0]
    bsz, t, d = x_prompt.shape
    nb, dt, _ = x_sample.shape
    past_len = 16384

    xp = x_prompt.reshape(bsz * t, d)
    xs = x_sample.reshape(nb * dt, d)
    p_states, s_states = [], []
    for l in range(depth):
        wl = _layer_weights(l, w)
        xp, ps = _prompt_layer(xp, wl, bsz, t)
        xs, ss = _decode_layer(xs, (state_conformer[l], state_rwkv_shift[l], state_rwkv_wkv[l],
                                    state_shortconv[l], state_pool[l]), wl, past_len)
        p_states.append(ps)
        s_states.append(ss)
    y_prompt = _final_norm(xp, final_norm_g, 512).reshape(bsz, t, d)
    y_sample = _final_norm(xs, final_norm_g, 128).reshape(nb, dt, d)
    stack = lambda lst, i: jnp.stack([s[i] for s in lst], axis=0)
    conf_p, shift_p, wkv_p, sc_p, pool_p = (stack(p_states, i) for i in range(5))
    conf_s, shift_s, wkv_s, sc_s, pool_s = (stack(s_states, i) for i in range(5))
    return (y_prompt, y_sample, conf_p, conf_s, shift_p, shift_s, wkv_p, wkv_s, sc_p, sc_s, pool_p, pool_s)
```

```python
import functools

import jax
import jax.numpy as jnp
import numpy as np
from jax import lax
from jax.experimental import pallas as pl
from jax.experimental.pallas import tpu as pltpu

F32 = jnp.float32
BF16 = jnp.bfloat16

D_MODEL = 1024
GROUP_W = 256
CONF_K = 31
RWKV_HEAD = 64
RWKV_H = 4
RWKV_COLS = 896
RWKV_GN_EPS = 64e-5
POOL_WINDOWS = (2, 4, 8, 16)
POOL_BUF = 15
N_KEYS = 128
PEER_HEADS = 8
PEER_TOPK = 16
RMS_EPS = 1e-6
LN_EPS = 1e-5
IN_SPLITS = (512, 896, 768, 256)

VMEM_LIMIT_BYTES = 56 * 1024 * 1024
WKV_CHUNK = 64
WKV_SEQS_PER_STEP = 4
WKV_PASSES = (1, 1, 1, 1)

_RWKV_PERM = np.concatenate([np.arange(0, 256), np.arange(288, 544), np.arange(544, 800),
                             np.arange(256, 288), np.arange(800, 832), np.arange(832, 896)])
_RWKV_INV_PERM = np.argsort(_RWKV_PERM)


def _cparams(*sem):
    return pltpu.CompilerParams(dimension_semantics=tuple(sem) if sem else None,
                                vmem_limit_bytes=VMEM_LIMIT_BYTES)


def _full(shape):
    n = len(shape)
    return pl.BlockSpec(shape, lambda *_: (0,) * n)


def _split2(x):
    hi = x.astype(BF16)
    lo = (x - hi.astype(F32)).astype(BF16)
    return hi, lo


def _split3(x):
    hi = x.astype(BF16)
    r = x - hi.astype(F32)
    mid = r.astype(BF16)
    lo = (r - mid.astype(F32)).astype(BF16)
    return hi, mid, lo


_NN = (((1,), (0,)), ((), ()))
_NT = (((1,), (1,)), ((), ()))


def _dg(a, b, dims):
    return lax.dot_general(a, b, dims, preferred_element_type=F32)


def _mm3(a, b, dims=_NN):
    ah, al = _split2(a)
    bh, bl = _split2(b)
    return _dg(ah, bh, dims) + (_dg(al, bh, dims) + _dg(ah, bl, dims))


def _mm(a, b, dims=_NN, passes=3):
    if passes == 1:
        return _dg(a.astype(BF16), b.astype(BF16), dims)
    return _mm3(a, b, dims)


def _mm_exact_rhs(a, b_bf16):
    h, m, l = _split3(a)
    return _dg(h, b_bf16, _NN) + (_dg(m, b_bf16, _NN) + _dg(l, b_bf16, _NN))


def _mm_exact_lhs(a_bf16, b):
    h, m, l = _split3(b)
    return _dg(a_bf16, h, _NN) + (_dg(a_bf16, m, _NN) + _dg(a_bf16, l, _NN))


def _transpose_mxu(x, eye_bf16):
    h, m, l = _split3(x)
    return _dg(eye_bf16, h, _NT) + (_dg(eye_bf16, m, _NT) + _dg(eye_bf16, l, _NT))


def _eye(n, dtype):
    return (lax.broadcasted_iota(jnp.int32, (n, n), 0) == lax.broadcasted_iota(jnp.int32, (n, n), 1)).astype(dtype)


def _rmsnorm(x, g):
    ms = jnp.mean(x * x, axis=-1, keepdims=True)
    return x * lax.rsqrt(ms + RMS_EPS) * g


def _sigmoid(x):
    return 1.0 / (1.0 + jnp.exp(-x))


def _softplus(x):
    return jnp.maximum(x, 0.0) + jnp.log(1.0 + jnp.exp(-jnp.abs(x)))


def _norm_proj_kernel(x_ref, g_ref, w_ref, *o_refs, splits):
    xb = _rmsnorm(x_ref[...], g_ref[...]).astype(BF16)
    off = 0
    for o_ref, wd in zip(o_refs, splits):
        o_ref[...] = jnp.dot(xb, w_ref[:, off:off + wd], preferred_element_type=F32)
        off += wd


def _norm_proj(x, g, w_bf16, splits, tm):
    n, d = x.shape
    tm = min(tm, n)
    cols = w_bf16.shape[1]
    return pl.pallas_call(
        functools.partial(_norm_proj_kernel, splits=splits),
        grid=(n // tm,),
        in_specs=[pl.BlockSpec((tm, d), lambda i: (i, 0)), _full((1, d)), _full((d, cols))],
        out_specs=[pl.BlockSpec((tm, wd), lambda i: (i, 0)) for wd in splits],
        out_shape=[jax.ShapeDtypeStruct((n, wd), F32) for wd in splits],
        compiler_params=_cparams("parallel"),
        name="norm_proj",
    )(x, g.reshape(1, d), w_bf16)


def _out_proj_kernel(x_ref, ya_ref, yb_ref, yc_ref, yd_ref, w_ref, o_ref):
    acc = x_ref[...]
    for i, y_ref in enumerate((ya_ref, yb_ref, yc_ref, yd_ref)):
        acc = acc + jnp.dot(y_ref[...].astype(BF16), w_ref[i * GROUP_W:(i + 1) * GROUP_W, :],
                            preferred_element_type=F32)
    o_ref[...] = acc


def _out_proj(x, ys, w_bf16, tm):
    n, d = x.shape
    tm = min(tm, n)
    row = lambda i: (i, 0)
    return pl.pallas_call(
        _out_proj_kernel,
        grid=(n // tm,),
        in_specs=[pl.BlockSpec((tm, d), row)] + [pl.BlockSpec((tm, GROUP_W), row)] * 4 + [_full((d, d))],
        out_specs=pl.BlockSpec((tm, d), row),
        out_shape=jax.ShapeDtypeStruct((n, d), F32),
        compiler_params=_cparams("parallel"),
        name="out_proj",
    )(x, *ys, w_bf16)


def _final_norm_kernel(x_ref, g_ref, o_ref):
    o_ref[...] = _rmsnorm(x_ref[...], g_ref[...])


def _final_norm(x, g, tm):
    n, d = x.shape
    tm = min(tm, n)
    return pl.pallas_call(
        _final_norm_kernel,
        grid=(n // tm,),
        in_specs=[pl.BlockSpec((tm, d), lambda i: (i, 0)), _full((1, d))],
        out_specs=pl.BlockSpec((tm, d), lambda i: (i, 0)),
        out_shape=jax.ShapeDtypeStruct((n, d), F32),
        compiler_params=_cparams("parallel"),
        name="final_norm",
    )(x, g.reshape(1, d))


def _layernorm_lanes(x, g, b, eps):
    mu = jnp.mean(x, axis=-1, keepdims=True)
    xc = x - mu
    var = jnp.mean(xc * xc, axis=-1, keepdims=True)
    return xc * lax.rsqrt(var + eps) * g + b


def _conformer_tail(ca, lng, lnb):
    y = _layernorm_lanes(ca, lng, lnb, LN_EPS)
    return y * _sigmoid(y)


def _pool_select(w2, w4, w8, w16, shape):
    lane = lax.broadcasted_iota(jnp.int32, shape, len(shape) - 1)
    wsum = jnp.where(lane < 64, w2, jnp.where(lane < 128, w4, jnp.where(lane < 192, w8, w16)))
    win = jnp.where(lane < 64, 2.0, jnp.where(lane < 128, 4.0, jnp.where(lane < 192, 8.0, 16.0)))
    return wsum, win


def _head_sum(x, ones_bd):
    return _mm_exact_rhs(x, ones_bd)


def _rwkv_prep(p, prev, wr):
    (mu, w0, w2, a0, a2, g2, k_k, k_a, r_k, ones_bd) = wr
    xs = p + (prev - p) * mu
    r = xs[:, 0:256]
    k = xs[:, 256:512]
    v = xs[:, 512:768]
    w_lo = xs[:, 768:800]
    a_lo = xs[:, 800:832]
    g_lo = xs[:, 832:896]
    wexp = -_softplus(-(w0 + jnp.dot(jnp.tanh(w_lo).astype(BF16), w2, preferred_element_type=F32))) - 0.5
    logdecay = -jnp.exp(wexp)
    a = _sigmoid(a0 + jnp.dot(a_lo.astype(BF16), a2, preferred_element_type=F32))
    g = jnp.dot(_sigmoid(g_lo).astype(BF16), g2, preferred_element_type=F32)
    kk = k * k_k
    kk = kk * lax.rsqrt(jnp.maximum(_head_sum(kk * kk, ones_bd), 1e-24))
    k2 = k * (1.0 + (a - 1.0) * k_a)
    bonus = _head_sum(r * k2 * r_k, ones_bd) * v
    return r, logdecay, k2, v, kk, a, g, bonus


def _rwkv_post(y, bonus, g, lng, lnb, ones_bd):
    mu = _head_sum(y, ones_bd) * (1.0 / RWKV_HEAD)
    yc = y - mu
    var = _head_sum(yc * yc, ones_bd) * (1.0 / RWKV_HEAD)
    yn = yc * lax.rsqrt(var + RWKV_GN_EPS) * lng + lnb
    return (yn + bonus) * g


_CONV_ROWS = 64


def _prompt_acd_kernel(za_ref, zc_ref, zd_ref, conf0_ref, sc0_ref, pool0_ref,
                       cw_ref, cb_ref, clg_ref, clb_ref, scw_ref, pw_ref, ps_ref,
                       ya_ref, yc_ref, yd_ref, conf_o_ref, sc_o_ref, pool_o_ref,
                       ext_a, ext_c, ext_d, *, tt, start_pos):
    t = pl.program_id(1)

    @pl.when(t == 0)
    def _():
        ext_a[0:2, :] = jnp.zeros((2, GROUP_W), F32)
        ext_a[2:32, :] = conf0_ref[0]
        ext_c[0:6, :] = jnp.zeros((6, GROUP_W), F32)
        ext_c[6:8, :] = sc0_ref[0]
        ext_d[0:1, :] = jnp.zeros((1, GROUP_W), F32)
        ext_d[1:16, :] = pool0_ref[0]

    za = za_ref[0]
    ext_a[32:32 + tt, :] = za[:, 0:GROUP_W] * _sigmoid(za[:, GROUP_W:2 * GROUP_W])
    for c in range(tt // _CONV_ROWS):
        base = c * _CONV_ROWS
        acc = jnp.zeros((_CONV_ROWS, GROUP_W), F32) + cb_ref[...]
        for k in range(CONF_K):
            acc = acc + cw_ref[k:k + 1, :] * ext_a[base + k + 2:base + k + 2 + _CONV_ROWS, :]
        ya_ref[0, base:base + _CONV_ROWS, :] = _conformer_tail(acc, clg_ref[...], clb_ref[...])
    conf_o_ref[0] = ext_a[tt + 2:tt + 32, :]
    ext_a[0:32, :] = ext_a[tt:tt + 32, :]

    zc = zc_ref[0]
    ext_c[8:8 + tt, :] = zc[:, GROUP_W:2 * GROUP_W] * zc[:, 2 * GROUP_W:3 * GROUP_W]
    cc = (scw_ref[0:1, :] * ext_c[6:6 + tt, :] + scw_ref[1:2, :] * ext_c[7:7 + tt, :]
          + scw_ref[2:3, :] * ext_c[8:8 + tt, :])
    yc_ref[0] = zc[:, 0:GROUP_W] * cc
    sc_o_ref[0] = ext_c[tt + 6:tt + 8, :]
    ext_c[0:8, :] = ext_c[tt:tt + 8, :]

    u = zd_ref[0]
    ext_d[16:16 + tt, :] = u
    w2 = u + ext_d[15:15 + tt, :]
    w4 = w2 + ext_d[14:14 + tt, :] + ext_d[13:13 + tt, :]
    w8 = w4
    for j in range(4, 8):
        w8 = w8 + ext_d[16 - j:16 - j + tt, :]
    w16 = w8
    for j in range(8, 16):
        w16 = w16 + ext_d[16 - j:16 - j + tt, :]
    wsum, win = _pool_select(w2, w4, w8, w16, (tt, GROUP_W))
    pos = (lax.broadcasted_iota(jnp.int32, (tt, GROUP_W), 0) + (t * tt + start_pos + 1)).astype(F32)
    pooled = wsum / jnp.minimum(win, pos) - u
    yd_ref[0] = jnp.dot(pooled.astype(BF16), pw_ref[...], preferred_element_type=F32) * ps_ref[...]
    pool_o_ref[0] = ext_d[tt + 1:tt + 16, :]
    ext_d[0:16, :] = ext_d[tt:tt + 16, :]


def _prompt_acd(za, zc, zd, conf0, sc0, pool0, wl, tt, start_pos):
    b, t, _ = za.shape
    tile = lambda w: pl.BlockSpec((1, tt, w), lambda i, j: (i, j, 0))
    st = lambda r: pl.BlockSpec((1, r, GROUP_W), lambda i, j: (i, 0, 0))
    row = _full((1, GROUP_W))
    return pl.pallas_call(
        functools.partial(_prompt_acd_kernel, tt=tt, start_pos=start_pos),
        grid=(b, t // tt),
        in_specs=[tile(512), tile(768), tile(256), st(30), st(2), st(15),
                  _full((CONF_K, GROUP_W)), row, row, row, _full((3, GROUP_W)), _full((GROUP_W, GROUP_W)), row],
        out_specs=[tile(256), tile(256), tile(256), st(30), st(2), st(15)],
        out_shape=[jax.ShapeDtypeStruct((b, t, GROUP_W), F32)] * 3
        + [jax.ShapeDtypeStruct((b, r, GROUP_W), F32) for r in (30, 2, 15)],
        scratch_shapes=[pltpu.VMEM((32 + tt, GROUP_W), F32), pltpu.VMEM((8 + tt, GROUP_W), F32),
                        pltpu.VMEM((16 + tt, GROUP_W), F32)],
        compiler_params=_cparams("parallel", "arbitrary"),
        name="prompt_acd",
    )(za, zc, zd, conf0, sc0, pool0, wl["conf_dw_w"], wl["conf_dw_b"], wl["conf_ln_g"], wl["conf_ln_b"],
      wl["sc_conv_w"], wl["pool_wbd"], wl["pool_scale"])


def _prompt_rwkv_kernel(zb_ref, shift0_ref, wkv0_ref, mu_ref, w0_ref, w2_ref, a0_ref, a2_ref, g2_ref,
                        kk_ref, ka_ref, rk_ref, lng_ref, lnb_ref, ones_ref, tri_ref,
                        yb_ref, shift_o_ref, wkv_o_ref, prev_s, st_s, *, c, nb):
    t = pl.program_id(1)
    nt = pl.num_programs(1)

    @pl.when(t == 0)
    def _():
        for bb in range(nb):
            prev_s[bb] = jnp.broadcast_to(shift0_ref[bb], prev_s.shape[1:])
            for h in range(RWKV_H):
                st_s[bb, h] = _transpose_mxu(wkv0_ref[bb, h], _eye(RWKV_HEAD, BF16))

    new_states = _rwkv_chunks(zb_ref, mu_ref, w0_ref, w2_ref, a0_ref, a2_ref, g2_ref, kk_ref, ka_ref, rk_ref,
                              lng_ref, lnb_ref, ones_ref, tri_ref, yb_ref, shift_o_ref, prev_s, st_s, c, nb)

    @pl.when(t == nt - 1)
    def _():
        eye_h = _eye(RWKV_HEAD, BF16)
        for bb in range(nb):
            for h in range(RWKV_H):
                wkv_o_ref[bb, h] = _transpose_mxu(new_states[(bb, h)], eye_h)


def _rwkv_chunks(zb_ref, mu_ref, w0_ref, w2_ref, a0_ref, a2_ref, g2_ref, kk_ref, ka_ref, rk_ref,
                 lng_ref, lnb_ref, ones_ref, tri_ref, yb_ref, shift_o_ref, prev_s, st_s, c, nb):
    p = zb_ref[...].reshape(nb * c, RWKV_COLS)
    row = lax.broadcasted_iota(jnp.int32, p.shape, 0)
    prev = pltpu.roll(p, 1, axis=0)
    for bb in range(nb):
        prev = jnp.where(row == bb * c, prev_s[bb, 0:1, :], prev)
    ones_bd = ones_ref[...]
    wr = (mu_ref[...], w0_ref[...], w2_ref[...], a0_ref[...], a2_ref[...], g2_ref[...],
          kk_ref[...], ka_ref[...], rk_ref[...], ones_bd)
    r, logdecay, k2, v, kk, a, g, bonus = _rwkv_prep(p, prev, wr)

    cum = _mm_exact_lhs(tri_ref[...], logdecay)
    g_end = [cum[(bb + 1) * c - 1:(bb + 1) * c, :] for bb in range(nb)]
    cum_end = jnp.concatenate([jnp.broadcast_to(ge, (c, GROUP_W)) for ge in g_end], axis=0)
    g_end = [jnp.exp(ge) for ge in g_end]
    e_neg = jnp.exp(-cum)
    e_end = jnp.exp(cum_end - cum)
    bvec = kk * a
    a_t = -kk * jnp.exp(cum - logdecay)
    r_t = r * jnp.exp(cum)
    b_h = bvec * e_neg
    k_h = k2 * e_neg
    b_e = bvec * e_end
    k_e = k2 * e_end

    ri = lax.broadcasted_iota(jnp.int32, (2 * c, 2 * c), 0)
    ci = lax.broadcasted_iota(jnp.int32, (2 * c, 2 * c), 1)
    rt, cs = ri & (c - 1), ci & (c - 1)
    keep = (rt > cs) | ((ri >= c) & (rt == cs))
    eye_2h = _eye(2 * RWKV_HEAD, BF16)
    eye_f = _eye(RWKV_HEAD, F32)

    p_nt, p_rhs, p_neu, p_out = WKV_PASSES
    chains = [(bb, h) for bb in range(nb) for h in range(RWKV_H)]

    def part(z, bb, h):
        return z[bb * c:(bb + 1) * c, h * RWKV_HEAD:(h + 1) * RWKV_HEAD]

    big = {ch: jnp.where(keep, _mm(jnp.concatenate([part(a_t, *ch), part(r_t, *ch)], axis=0),
                                   jnp.concatenate([part(b_h, *ch), part(k_h, *ch)], axis=0), _NT, p_nt), 0.0)
           for ch in chains}
    s0 = {ch: st_s[ch[0], ch[1]] for ch in chains}
    vh = {ch: part(v, *ch) for ch in chains}
    sa = {ch: _mm(jnp.concatenate([part(a_t, *ch), big[ch][0:c, c:2 * c]], axis=1),
                  jnp.concatenate([s0[ch], vh[ch]], axis=0), _NN, p_rhs) for ch in chains}
    x = {ch: big[ch][0:c, 0:c] for ch in chains}
    n_sq = int(np.log2(c))
    for step in range(n_sq):
        sa = {ch: sa[ch] + _mm(x[ch], sa[ch], _NN, p_neu) for ch in chains}
        if step + 1 < n_sq:
            x = {ch: _mm(x[ch], x[ch], _NN, p_neu) for ch in chains}
    bk = {ch: jnp.concatenate([part(b_e, *ch), part(k_e, *ch)], axis=1) for ch in chains}
    bk_t = {ch: _dg(eye_2h, bk[ch].astype(BF16), _NT) if p_out == 1 else _transpose_mxu(bk[ch], eye_2h)
            for ch in chains}
    y_h = {ch: _mm(jnp.concatenate([part(r_t, *ch), big[ch][c:2 * c, 0:c], big[ch][c:2 * c, c:2 * c]], axis=1),
                   jnp.concatenate([s0[ch], sa[ch], vh[ch]], axis=0), _NN, p_out) for ch in chains}
    s_new = {}
    for ch in chains:
        bb, h = ch
        g_h = g_end[bb][:, h * RWKV_HEAD:(h + 1) * RWKV_HEAD]
        s_new[ch] = _mm3(eye_f * g_h, s0[ch]) + _mm(
            jnp.concatenate([bk_t[ch][0:RWKV_HEAD], bk_t[ch][RWKV_HEAD:]], axis=1),
            jnp.concatenate([sa[ch], vh[ch]], axis=0), _NN, p_out)
        st_s[bb, h] = s_new[ch]

    y = jnp.concatenate([jnp.concatenate([y_h[(bb, h)] for h in range(RWKV_H)], axis=-1) for bb in range(nb)], axis=0)
    yb = _rwkv_post(y, bonus, g, lng_ref[...], lnb_ref[...], ones_bd)
    for bb in range(nb):
        yb_ref[bb] = yb[bb * c:(bb + 1) * c]
        last = p[(bb + 1) * c - 1:(bb + 1) * c, :]
        prev_s[bb] = jnp.broadcast_to(last, prev_s.shape[1:])
        shift_o_ref[bb] = last
    return s_new


def _rwkv_weight_args(wl):
    return (wl["rwkv_mu"], wl["rwkv_w0"], wl["rwkv_w2"], wl["rwkv_a0"], wl["rwkv_a2"], wl["rwkv_g2"],
            wl["rwkv_k_k"], wl["rwkv_k_a"], wl["rwkv_r_k"], wl["rwkv_ln_g"], wl["rwkv_ln_b"], wl["ones_bd"])


_RWKV_WEIGHT_SPECS = [(1, RWKV_COLS), (1, GROUP_W), (32, GROUP_W), (1, GROUP_W), (32, GROUP_W), (64, GROUP_W),
                      (1, GROUP_W), (1, GROUP_W), (1, GROUP_W), (1, GROUP_W), (1, GROUP_W), (GROUP_W, GROUP_W)]


def _prompt_rwkv(zb, shift0, wkv0, wl, nb):
    b, t, _ = zb.shape
    c = WKV_CHUNK
    nb = min(nb, b)
    tri = jnp.kron(jnp.eye(nb, dtype=F32), jnp.tril(jnp.ones((c, c), F32))).astype(BF16)
    return pl.pallas_call(
        functools.partial(_prompt_rwkv_kernel, c=c, nb=nb),
        grid=(b // nb, t // c),
        in_specs=[pl.BlockSpec((nb, c, RWKV_COLS), lambda i, j: (i, j, 0)),
                  pl.BlockSpec((nb, 1, RWKV_COLS), lambda i, j: (i, 0, 0)),
                  pl.BlockSpec((nb, RWKV_H, RWKV_HEAD, RWKV_HEAD), lambda i, j: (i, 0, 0, 0))]
        + [_full(s) for s in _RWKV_WEIGHT_SPECS] + [_full((nb * c, nb * c))],
        out_specs=[pl.BlockSpec((nb, c, GROUP_W), lambda i, j: (i, j, 0)),
                   pl.BlockSpec((nb, 1, RWKV_COLS), lambda i, j: (i, 0, 0)),
                   pl.BlockSpec((nb, RWKV_H, RWKV_HEAD, RWKV_HEAD), lambda i, j: (i, 0, 0, 0))],
        out_shape=[jax.ShapeDtypeStruct((b, t, GROUP_W), F32), jax.ShapeDtypeStruct((b, 1, RWKV_COLS), F32),
                   jax.ShapeDtypeStruct((b, RWKV_H, RWKV_HEAD, RWKV_HEAD), F32)],
        scratch_shapes=[pltpu.VMEM((nb, 8, RWKV_COLS), F32), pltpu.VMEM((nb, RWKV_H, RWKV_HEAD, RWKV_HEAD), F32)],
        compiler_params=_cparams("parallel", "arbitrary"),
        name="prompt_rwkv",
    )(zb, shift0, wkv0, *_rwkv_weight_args(wl), tri)


def _decode_acd_prep_kernel(za_ref, zb_ref, zc_ref, zd_ref, conf_ref, shift_ref, sc_ref, pool_ref,
                            cw_ref, cb_ref, clg_ref, clb_ref, scw_ref, pw_ref, ps_ref,
                            mu_ref, w0_ref, w2_ref, a0_ref, a2_ref, g2_ref, kk_ref, ka_ref, rk_ref, ones_ref,
                            ya_ref, yc_ref, yd_ref, conf_o_ref, sc_o_ref, pool_o_ref,
                            r_o, w_o, k_o, v_o, kkn_o, b_o, g_o, bonus_o, *, start_pos):
    za = za_ref[...]
    glu = za[:, 0:GROUP_W] * _sigmoid(za[:, GROUP_W:2 * GROUP_W])
    acc = cb_ref[...] + cw_ref[CONF_K - 1:CONF_K, :] * glu
    for k in range(CONF_K - 1):
        acc = acc + cw_ref[k:k + 1, :] * conf_ref[k]
    ya_ref[...] = _conformer_tail(acc, clg_ref[...], clb_ref[...])
    for k in range(CONF_K - 2):
        conf_o_ref[k] = conf_ref[k + 1]
    conf_o_ref[CONF_K - 2] = glu

    zc = zc_ref[...]
    u = zc[:, GROUP_W:2 * GROUP_W] * zc[:, 2 * GROUP_W:3 * GROUP_W]
    cc = scw_ref[0:1, :] * sc_ref[0] + scw_ref[1:2, :] * sc_ref[1] + scw_ref[2:3, :] * u
    yc_ref[...] = zc[:, 0:GROUP_W] * cc
    sc_o_ref[0] = sc_ref[1]
    sc_o_ref[1] = u

    d = zd_ref[...]
    w2 = d + pool_ref[POOL_BUF - 1]
    w4 = w2 + pool_ref[POOL_BUF - 2] + pool_ref[POOL_BUF - 3]
    w8 = w4
    for j in range(4, 8):
        w8 = w8 + pool_ref[POOL_BUF - j]
    w16 = w8
    for j in range(8, 16):
        w16 = w16 + pool_ref[POOL_BUF - j]
    wsum, win = _pool_select(w2, w4, w8, w16, d.shape)
    pooled = wsum / jnp.minimum(win, float(start_pos + 1)) - d
    yd_ref[...] = jnp.dot(pooled.astype(BF16), pw_ref[...], preferred_element_type=F32) * ps_ref[...]
    for k in range(POOL_BUF - 1):
        pool_o_ref[k] = pool_ref[k + 1]
    pool_o_ref[POOL_BUF - 1] = d

    wr = (mu_ref[...], w0_ref[...], w2_ref[...], a0_ref[...], a2_ref[...], g2_ref[...],
          kk_ref[...], ka_ref[...], rk_ref[...], ones_ref[...])
    r, logdecay, k2, v, kk, a, g, bonus = _rwkv_prep(zb_ref[...], shift_ref[...], wr)
    r_o[...] = r
    w_o[...] = jnp.exp(logdecay)
    k_o[...] = k2
    v_o[...] = v
    kkn_o[...] = kk
    b_o[...] = kk * a
    g_o[...] = g
    bonus_o[...] = bonus


def _decode_acd_prep(za, zb, zc, zd, conf_t, shift, sc_t, pool_t, wl, start_pos):
    n = za.shape[0]
    ins = (za, zb, zc, zd, conf_t, shift, sc_t, pool_t, wl["conf_dw_w"], wl["conf_dw_b"], wl["conf_ln_g"],
           wl["conf_ln_b"], wl["sc_conv_w"], wl["pool_wbd"], wl["pool_scale"]) + _rwkv_weight_args(wl)[:9] + (wl["ones_bd"],)
    vec = jax.ShapeDtypeStruct((n, GROUP_W), F32)
    outs = [vec, vec, vec, jax.ShapeDtypeStruct(conf_t.shape, F32), jax.ShapeDtypeStruct(sc_t.shape, F32),
            jax.ShapeDtypeStruct(pool_t.shape, F32)] + [vec] * 8
    return pl.pallas_call(
        functools.partial(_decode_acd_prep_kernel, start_pos=start_pos),
        in_specs=[_full(x.shape) for x in ins],
        out_specs=[_full(o.shape) for o in outs],
        out_shape=outs,
        compiler_params=_cparams(),
        name="decode_acd_prep",
    )(*ins)


def _decode_wkv_kernel(s_ref, w_ref, kk_ref, b_ref, k_ref, r_ref, v_ref, s_o_ref, y_o_ref):
    s = s_ref[...]
    sa = -jnp.sum(s * kk_ref[...], axis=-1, keepdims=True)
    s_new = s * w_ref[...] + sa * b_ref[...] + v_ref[...] * k_ref[...]
    s_o_ref[...] = s_new
    y_o_ref[...] = jnp.sum(s_new * r_ref[...], axis=-1, keepdims=True)


def _decode_wkv(s, w, kk, bvec, k, r, v, blk):
    bh = s.shape[0]
    lane = pl.BlockSpec((blk, 1, RWKV_HEAD), lambda i: (i, 0, 0))
    col = pl.BlockSpec((blk, RWKV_HEAD, 1), lambda i: (i, 0, 0))
    mat = pl.BlockSpec((blk, RWKV_HEAD, RWKV_HEAD), lambda i: (i, 0, 0))
    return pl.pallas_call(
        _decode_wkv_kernel,
        grid=(bh // blk,),
        in_specs=[mat, lane, lane, lane, lane, lane, col],
        out_specs=[mat, col],
        out_shape=[jax.ShapeDtypeStruct(s.shape, F32), jax.ShapeDtypeStruct((bh, RWKV_HEAD, 1), F32)],
        compiler_params=_cparams("parallel"),
        name="decode_wkv",
    )(s, w, kk, bvec, k, r, v)


def _decode_post_kernel(y_ref, bonus_ref, g_ref, lng_ref, lnb_ref, ones_ref, o_ref):
    o_ref[...] = _rwkv_post(y_ref[...], bonus_ref[...], g_ref[...], lng_ref[...], lnb_ref[...], ones_ref[...])


def _decode_post(y, bonus, g, wl):
    ins = (y, bonus, g, wl["rwkv_ln_g"], wl["rwkv_ln_b"], wl["ones_bd"])
    return pl.pallas_call(
        _decode_post_kernel,
        in_specs=[_full(x.shape) for x in ins],
        out_specs=_full(y.shape),
        out_shape=jax.ShapeDtypeStruct(y.shape, F32),
        compiler_params=_cparams(),
        name="decode_post",
    )(*ins)


_CELLS = [(a, b) for a in range(PEER_TOPK) for b in range(PEER_TOPK) if (a + 1) * (b + 1) <= PEER_TOPK]
_CELL_PAIRS = [(c, d) for c in _CELLS for d in _CELLS if d[0] < c[0] and d[1] > c[1]]


def _top16_rows(s, tb, vals_ref, h, tie_safe):
    lanes = 128
    iota = lax.broadcasted_iota(jnp.int32, (N_KEYS, lanes), 0).astype(F32)
    ranks = []
    for c0 in range(0, tb, lanes):
        sc = s[:, c0:c0 + lanes]
        rank = jnp.full((N_KEYS, lanes), float(PEER_TOPK), F32)
        for r in range(PEER_TOPK):
            m = jnp.max(sc, axis=0, keepdims=True)
            sel = sc == m
            if tie_safe:
                sel = iota == jnp.min(jnp.where(sel, iota, float(N_KEYS)), axis=0, keepdims=True)
            rank = jnp.where(sel, float(r), rank)
            sc = jnp.where(sel, -jnp.inf, sc)
            vals_ref[r, h:h + 1, c0:c0 + lanes] = m
        ranks.append(rank)
    rank = jnp.concatenate(ranks, axis=1) if len(ranks) > 1 else ranks[0]
    n_sel = jnp.sum(jnp.where(rank < float(PEER_TOPK), 1.0, 0.0), axis=0, keepdims=True)
    return rank, n_sel


def _peer_route_kernel(h_ref, g_ref, wq_ref, k1_ref, k2_ref,
                       xn_ref, r2_ref, p2_ref, c1_ref, p1_ref,
                       v1_s, v2_s, r1_s, cnt_s, *, tb):
    refs = (h_ref, g_ref, wq_ref, k1_ref, k2_ref, xn_ref, r2_ref, p2_ref, c1_ref, p1_ref, v1_s, v2_s, r1_s, cnt_s)
    n_bad = _peer_route_body(*refs, tb=tb, tie_safe=False)

    @pl.when(jnp.max(n_bad) > 0.0)
    def _():
        _peer_route_body(*refs, tb=tb, tie_safe=True)


def _peer_route_body(h_ref, g_ref, wq_ref, k1_ref, k2_ref,
                     xn_ref, r2_ref, p2_ref, c1_ref, p1_ref,
                     v1_s, v2_s, r1_s, cnt_s, *, tb, tie_safe):
    xb = _rmsnorm(h_ref[...], g_ref[...]).astype(BF16)
    xn_ref[...] = xb
    n_bad = jnp.zeros((1, tb), F32)
    for h in range(PEER_HEADS):
        q = jnp.dot(xb, wq_ref[:, h * 256:(h + 1) * 256], preferred_element_type=F32).astype(BF16)
        s1 = _dg(k1_ref[h], q[:, 0:128], _NT)
        s2 = _dg(k2_ref[h], q[:, 128:256], _NT)
        r1, n1 = _top16_rows(s1, tb, v1_s, h, tie_safe)
        r2, n2 = _top16_rows(s2, tb, v2_s, h, tie_safe)
        n_bad = n_bad + jnp.where(n1 != float(PEER_TOPK), 1.0, 0.0) + jnp.where(n2 != float(PEER_TOPK), 1.0, 0.0)
        r1_s[h] = r1
        r2_ref[h] = r2.astype(BF16)
        p1_ref[h] = jnp.exp(s1 - v1_s[0, h:h + 1, :])
        p2_ref[h] = jnp.exp(s2 - v2_s[0, h:h + 1, :]).astype(BF16)

    sums = {c: v1_s[c[0]] + v2_s[c[1]] for c in _CELLS}
    rank = {c: jnp.full((PEER_HEADS, tb), float((c[0] + 1) * (c[1] + 1) - 1), F32) for c in _CELLS}
    for c, d in _CELL_PAIRS:
        ge = sums[d] >= sums[c]
        rank[c] = rank[c] + jnp.where(ge, 1.0, 0.0)
        rank[d] = rank[d] + jnp.where(ge, 0.0, 1.0)
    e1 = [jnp.exp(v1_s[a] - v1_s[0]) for a in range(PEER_TOPK)]
    e2 = [jnp.exp(v2_s[b] - v2_s[0]) for b in range(PEER_TOPK)]
    z = jnp.zeros((PEER_HEADS, tb), F32)
    cnt = [jnp.zeros((PEER_HEADS, tb), F32) for _ in range(PEER_TOPK)]
    for c in _CELLS:
        sel = rank[c] < float(PEER_TOPK)
        cnt[c[0]] = cnt[c[0]] + jnp.where(sel, 1.0, 0.0)
        z = z + jnp.where(sel, e1[c[0]] * e2[c[1]], 0.0)
    for a in range(PEER_TOPK):
        cnt_s[a] = cnt[a]
    cnt_s[PEER_TOPK] = 0.5 / z

    for h in range(PEER_HEADS):
        r1 = r1_s[h]
        c1 = jnp.zeros((N_KEYS, tb), F32)
        for a in range(PEER_TOPK):
            c1 = jnp.where(r1 == float(a), cnt_s[a, h:h + 1, :], c1)
        c1_ref[h] = c1
        p1_ref[h] = p1_ref[h] * cnt_s[PEER_TOPK, h:h + 1, :]
    return n_bad


def _peer_route(hres, g, wq_bf16, k1_bf16, k2_bf16, tb):
    n, d = hres.shape
    tb = min(tb, n)
    gate = pl.BlockSpec((PEER_HEADS, N_KEYS, tb), lambda i: (0, 0, i))
    gshape = lambda dt: jax.ShapeDtypeStruct((PEER_HEADS, N_KEYS, n), dt)
    return pl.pallas_call(
        functools.partial(_peer_route_kernel, tb=tb),
        grid=(n // tb,),
        in_specs=[pl.BlockSpec((tb, d), lambda i: (i, 0)), _full((1, d)), _full(wq_bf16.shape),
                  _full(k1_bf16.shape), _full(k2_bf16.shape)],
        out_specs=[pl.BlockSpec((tb, d), lambda i: (i, 0)), gate, gate, gate, gate],
        out_shape=[jax.ShapeDtypeStruct((n, d), BF16), gshape(BF16), gshape(BF16), gshape(F32), gshape(F32)],
        scratch_shapes=[pltpu.VMEM((PEER_TOPK, PEER_HEADS, tb), F32), pltpu.VMEM((PEER_TOPK, PEER_HEADS, tb), F32),
                        pltpu.VMEM((PEER_HEADS, N_KEYS, tb), F32), pltpu.VMEM((PEER_TOPK + 1, PEER_HEADS, tb), F32)],
        compiler_params=_cparams("parallel"),
        name="peer_route",
    )(hres, g.reshape(1, d), wq_bf16, k1_bf16, k2_bf16)


_SQRT_HALF = float(np.sqrt(0.5))


def _peer_expert_kernel(xn_ref, hres_ref, u_ref, vt_ref, r2_ref, p2_ref, c1_ref, p1_ref, o_ref,
                        acc_s, ht_s, at_s, *, eb):
    j = pl.program_id(1)

    @pl.when(j == 0)
    def _():
        acc_s[...] = jnp.zeros_like(acc_s)

    ht_s[...] = _dg(u_ref[...], xn_ref[...], _NT)
    zero = jnp.zeros((), BF16)
    for i in range(eb // N_KEYS):
        i1 = j * (eb // N_KEYS) + i
        gt = None
        for h in range(PEER_HEADS):
            c1 = c1_ref[h, pl.ds(i1, 1), :].astype(BF16)
            p1 = p1_ref[h, pl.ds(i1, 1), :].astype(BF16)
            term = jnp.where(r2_ref[h] < c1, p2_ref[h], zero) * p1
            gt = term if gt is None else gt + term
        ht = ht_s[i * N_KEYS:(i + 1) * N_KEYS, :]
        act = ht * (1.0 + lax.erf(ht * _SQRT_HALF))
        at_s[i * N_KEYS:(i + 1) * N_KEYS, :] = act.astype(BF16) * gt
    acc_s[...] += jnp.dot(vt_ref[...], at_s[...], preferred_element_type=F32)

    @pl.when(j == pl.num_programs(1) - 1)
    def _():
        o_ref[...] = hres_ref[...] + acc_s[...].T


def _peer_experts(xn, hres, u_bf16, vt_bf16, gates, tb, eb):
    n, d = hres.shape
    tb = min(tb, n)
    e = u_bf16.shape[0]
    gate = pl.BlockSpec((PEER_HEADS, N_KEYS, tb), lambda i, j: (0, 0, i))
    return pl.pallas_call(
        functools.partial(_peer_expert_kernel, eb=eb),
        grid=(n // tb, e // eb),
        in_specs=[pl.BlockSpec((tb, d), lambda i, j: (i, 0)), pl.BlockSpec((tb, d), lambda i, j: (i, 0)),
                  pl.BlockSpec((eb, d), lambda i, j: (j, 0)), pl.BlockSpec((d, eb), lambda i, j: (0, j)),
                  gate, gate, gate, gate],
        out_specs=pl.BlockSpec((tb, d), lambda i, j: (i, 0)),
        out_shape=jax.ShapeDtypeStruct((n, d), F32),
        scratch_shapes=[pltpu.VMEM((d, tb), F32), pltpu.VMEM((eb, tb), F32), pltpu.VMEM((eb, tb), BF16)],
        compiler_params=_cparams("parallel", "arbitrary"),
        name="peer_experts",
    )(xn, hres, u_bf16, vt_bf16, *gates)


def _layer_weights(l, w):
    row = lambda a: a[l].reshape(1, -1)
    perm = _RWKV_PERM
    w_in = w["w_in"][l]
    w_in = jnp.concatenate([w_in[:, 0:512], w_in[:, 512:1408][:, perm], w_in[:, 1408:]], axis=1).astype(BF16)
    eye4 = jnp.eye(RWKV_H, dtype=F32)
    ones_bd = jnp.kron(eye4, jnp.ones((RWKV_HEAD, RWKV_HEAD), F32)).astype(BF16)
    pool_wbd = jax.scipy.linalg.block_diag(*[w["pool_w"][l, gi] for gi in range(4)]).astype(BF16)
    return dict(
        norm1_g=w["norm1_g"][l], norm2_g=w["norm2_g"][l], w_in=w_in,
        conf_dw_w=w["conf_dw_w"][l], conf_dw_b=row(w["conf_dw_b"]), conf_ln_g=row(w["conf_ln_g"]),
        conf_ln_b=row(w["conf_ln_b"]), sc_conv_w=w["sc_conv_w"][l], pool_wbd=pool_wbd, pool_scale=row(w["pool_scale"]),
        rwkv_mu=w["rwkv_mu"][l][perm].reshape(1, -1), rwkv_w0=row(w["rwkv_w0"]), rwkv_w2=w["rwkv_w2"][l].astype(BF16),
        rwkv_a0=row(w["rwkv_a0"]), rwkv_a2=w["rwkv_a2"][l].astype(BF16), rwkv_g2=w["rwkv_g2"][l].astype(BF16),
        rwkv_k_k=row(w["rwkv_k_k"]), rwkv_k_a=row(w["rwkv_k_a"]), rwkv_r_k=row(w["rwkv_r_k"]),
        rwkv_ln_g=row(w["rwkv_ln_g"]), rwkv_ln_b=row(w["rwkv_ln_b"]), ones_bd=ones_bd,
        w_out=w["w_out"][l].astype(BF16), peer_wq=w["peer_wq"][l].astype(BF16),
        peer_k1=w["peer_k1"][l].astype(BF16), peer_k2=w["peer_k2"][l].astype(BF16),
        peer_u=w["peer_u"][l].astype(BF16), peer_vt=w["peer_v"][l].T.astype(BF16),
    )


def _peer_block(hres, wl, tb_route, tb_exp, eb):
    xn, r2, p2, c1, p1 = _peer_route(hres, wl["norm2_g"], wl["peer_wq"], wl["peer_k1"], wl["peer_k2"], tb_route)
    return _peer_experts(xn, hres, wl["peer_u"], wl["peer_vt"], (r2, p2, c1, p1), tb_exp, eb)


def _prompt_layer(x, wl, bsz, t):
    n = bsz * t
    za, zb, zc, zd = _norm_proj(x, wl["norm1_g"], wl["w_in"], IN_SPLITS, 512)
    z3 = lambda a: a.reshape(bsz, t, -1)
    zeros = lambda *s: jnp.zeros(s, F32)
    ya, yc, yd, conf, sc, pool = _prompt_acd(z3(za), z3(zc), z3(zd), zeros(bsz, 30, GROUP_W), zeros(bsz, 2, GROUP_W),
                                             zeros(bsz, 15, GROUP_W), wl, 256, 0)
    yb, shift, wkv = _prompt_rwkv(z3(zb), zeros(bsz, 1, RWKV_COLS), zeros(bsz, RWKV_H, RWKV_HEAD, RWKV_HEAD), wl,
                                   WKV_SEQS_PER_STEP)
    flat = lambda a: a.reshape(n, GROUP_W)
    hres = _out_proj(x, (flat(ya), flat(yb), flat(yc), flat(yd)), wl["w_out"], 512)
    x = _peer_block(hres, wl, 256, 512, 1024)
    return x, (conf, shift.reshape(bsz, RWKV_COLS)[:, _RWKV_INV_PERM], wkv, sc, pool)


def _decode_layer(x, states, wl, start_pos):
    conf, shift, wkv, sc, pool = states
    n = x.shape[0]
    za, zb, zc, zd = _norm_proj(x, wl["norm1_g"], wl["w_in"], IN_SPLITS, 128)
    tr = lambda a: jnp.transpose(a, (1, 0, 2))
    (ya, yc, yd, conf_n, sc_n, pool_n, r, w, k2, v, kk, bvec, g, bonus) = _decode_acd_prep(
        za, zb, zc, zd, tr(conf), shift[:, _RWKV_PERM], tr(sc), tr(pool), wl, start_pos)
    bh = n * RWKV_H
    lane = lambda a: a.reshape(bh, 1, RWKV_HEAD)
    s_new, y = _decode_wkv(wkv.reshape(bh, RWKV_HEAD, RWKV_HEAD), lane(w), lane(kk), lane(bvec), lane(k2), lane(r),
                           v.reshape(bh, RWKV_HEAD, 1), 64)
    yb = _decode_post(y.reshape(n, GROUP_W), bonus, g, wl)
    hres = _out_proj(x, (ya, yb, yc, yd), wl["w_out"], 128)
    x = _peer_block(hres, wl, 128, 128, 1024)
    return x, (tr(conf_n), zb[:, _RWKV_INV_PERM], s_new.reshape(n, RWKV_H, RWKV_HEAD, RWKV_HEAD), tr(sc_n), tr(pool_n))


def kernel(x_prompt, x_sample, state_conformer, state_rwkv_shift, state_rwkv_wkv, state_shortconv, state_pool, norm1_g, norm2_g, final_norm_g, w_in, conf_dw_w, conf_dw_b, conf_ln_g, conf_ln_b, rwkv_mu, rwkv_w0, rwkv_w2, rwkv_a0, rwkv_a2, rwkv_g2, rwkv_k_k, rwkv_k_a, rwkv_r_k, rwkv_ln_g, rwkv_ln_b, sc_conv_w, pool_w, pool_scale, w_out, peer_wq, peer_k1, peer_k2, peer_u, peer_v):
    w = dict(norm1_g=norm1_g, norm2_g=norm2_g, w_in=w_in, conf_dw_w=conf_dw_w, conf_dw_b=conf_dw_b,
             conf_ln_g=conf_ln_g, conf_ln_b=conf_ln_b, rwkv_mu=rwkv_mu, rwkv_w0=rwkv_w0, rwkv_w2=rwkv_w2,
             rwkv_a0=rwkv_a0, rwkv_a2=rwkv_a2, rwkv_g2=rwkv_g2, rwkv_k_k=rwkv_k_k, rwkv_k_a=rwkv_k_a,
             rwkv_r_k=rwkv_r_k, rwkv_ln_g=rwkv_ln_g, rwkv_ln_b=rwkv_ln_b, sc_conv_w=sc_conv_w, pool_w=pool_w,
             pool_scale=pool_scale, w_out=w_out, peer_wq=peer_wq, peer_k1=peer_k1, peer_k2=peer_k2,
             peer_u=peer_u, peer_v=peer_v)
    depth = w_in.shape[0]
    bsz, t, d = x_prompt.shape
    nb, dt, _ = x_sample.shape
    past_len = 16384

    xp = x_prompt.reshape(bsz * t, d)
    xs = x_sample.reshape(nb * dt, d)
    p_states, s_states = [], []
    for l in range(depth):
        wl = _layer_weights(l, w)
        xp, ps = _prompt_layer(xp, wl, bsz, t)
        xs, ss = _decode_layer(xs, (state_conformer[l], state_rwkv_shift[l], state_rwkv_wkv[l],
                                    state_shortconv[l], state_pool[l]), wl, past_len)
        p_states.append(ps)
        s_states.append(ss)
    y_prompt = _final_norm(xp, final_norm_g, 512).reshape(bsz, t, d)
    y_sample = _final_norm(xs, final_norm_g, 128).reshape(nb, dt, d)
    stack = lambda lst, i: jnp.stack([s[i] for s in lst], axis=0)
    conf_p, shift_p, wkv_p, sc_p, pool_p = (stack(p_states, i) for i in range(5))
    conf_s, shift_s, wkv_s, sc_s, pool_s = (stack(s_states, i) for i in range(5))
    return (y_prompt, y_sample, conf_p, conf_s, shift_p, shift_s, wkv_p, wkv_s, sc_p, sc_s, pool_p, pool_s)
```

```python
import functools

import jax
import jax.numpy as jnp
import numpy as np
from jax import lax
from jax.experimental import pallas as pl
from jax.experimental.pallas import tpu as pltpu

F32 = jnp.float32
BF16 = jnp.bfloat16

D_MODEL = 1024
GROUP_W = 256
CONF_K = 31
RWKV_HEAD = 64
RWKV_H = 4
RWKV_COLS = 896
RWKV_GN_EPS = 64e-5
POOL_WINDOWS = (2, 4, 8, 16)
POOL_BUF = 15
N_KEYS = 128
PEER_HEADS = 8
PEER_TOPK = 16
RMS_EPS = 1e-6
LN_EPS = 1e-5
IN_SPLITS = (512, 896, 768, 256)

VMEM_LIMIT_BYTES = 56 * 1024 * 1024
WKV_CHUNK = 64
EXPERT_SUB_BLOCK = 1024
WKV_SEQS_PER_STEP = 4
WKV_PASSES = (1, 1, 1, 1)

_RWKV_PERM = np.concatenate([np.arange(0, 256), np.arange(288, 544), np.arange(544, 800),
                             np.arange(256, 288), np.arange(800, 832), np.arange(832, 896)])
_RWKV_INV_PERM = np.argsort(_RWKV_PERM)


def _cparams(*sem):
    return pltpu.CompilerParams(dimension_semantics=tuple(sem) if sem else None,
                                vmem_limit_bytes=VMEM_LIMIT_BYTES)


def _full(shape):
    n = len(shape)
    return pl.BlockSpec(shape, lambda *_: (0,) * n)


def _split2(x):
    hi = x.astype(BF16)
    lo = (x - hi.astype(F32)).astype(BF16)
    return hi, lo


def _split3(x):
    hi = x.astype(BF16)
    r = x - hi.astype(F32)
    mid = r.astype(BF16)
    lo = (r - mid.astype(F32)).astype(BF16)
    return hi, mid, lo


_NN = (((1,), (0,)), ((), ()))
_NT = (((1,), (1,)), ((), ()))


def _dg(a, b, dims):
    return lax.dot_general(a, b, dims, preferred_element_type=F32)


def _mm3(a, b, dims=_NN):
    ah, al = _split2(a)
    bh, bl = _split2(b)
    return _dg(ah, bh, dims) + (_dg(al, bh, dims) + _dg(ah, bl, dims))


def _mm(a, b, dims=_NN, passes=3):
    if passes == 1:
        return _dg(a.astype(BF16), b.astype(BF16), dims)
    return _mm3(a, b, dims)


def _mm_exact_rhs(a, b_bf16):
    h, m, l = _split3(a)
    return _dg(h, b_bf16, _NN) + (_dg(m, b_bf16, _NN) + _dg(l, b_bf16, _NN))


def _mm_exact_lhs(a_bf16, b):
    h, m, l = _split3(b)
    return _dg(a_bf16, h, _NN) + (_dg(a_bf16, m, _NN) + _dg(a_bf16, l, _NN))


def _transpose_mxu(x, eye_bf16):
    h, m, l = _split3(x)
    return _dg(eye_bf16, h, _NT) + (_dg(eye_bf16, m, _NT) + _dg(eye_bf16, l, _NT))


def _eye(n, dtype):
    return (lax.broadcasted_iota(jnp.int32, (n, n), 0) == lax.broadcasted_iota(jnp.int32, (n, n), 1)).astype(dtype)


def _rmsnorm(x, g):
    ms = jnp.mean(x * x, axis=-1, keepdims=True)
    return x * lax.rsqrt(ms + RMS_EPS) * g


def _sigmoid(x):
    return 1.0 / (1.0 + jnp.exp(-x))


def _softplus(x):
    return jnp.maximum(x, 0.0) + jnp.log(1.0 + jnp.exp(-jnp.abs(x)))


def _norm_proj_kernel(x_ref, g_ref, w_ref, *o_refs, splits):
    xb = _rmsnorm(x_ref[...], g_ref[...]).astype(BF16)
    off = 0
    for o_ref, wd in zip(o_refs, splits):
        o_ref[...] = jnp.dot(xb, w_ref[:, off:off + wd], preferred_element_type=F32)
        off += wd


def _norm_proj(x, g, w_bf16, splits, tm):
    n, d = x.shape
    tm = min(tm, n)
    cols = w_bf16.shape[1]
    return pl.pallas_call(
        functools.partial(_norm_proj_kernel, splits=splits),
        grid=(n // tm,),
        in_specs=[pl.BlockSpec((tm, d), lambda i: (i, 0)), _full((1, d)), _full((d, cols))],
        out_specs=[pl.BlockSpec((tm, wd), lambda i: (i, 0)) for wd in splits],
        out_shape=[jax.ShapeDtypeStruct((n, wd), F32) for wd in splits],
        compiler_params=_cparams("parallel"),
        name="norm_proj",
    )(x, g.reshape(1, d), w_bf16)


def _out_proj_kernel(x_ref, ya_ref, yb_ref, yc_ref, yd_ref, w_ref, o_ref):
    acc = x_ref[...]
    for i, y_ref in enumerate((ya_ref, yb_ref, yc_ref, yd_ref)):
        acc = acc + jnp.dot(y_ref[...].astype(BF16), w_ref[i * GROUP_W:(i + 1) * GROUP_W, :],
                            preferred_element_type=F32)
    o_ref[...] = acc


def _out_proj(x, ys, w_bf16, tm):
    n, d = x.shape
    tm = min(tm, n)
    row = lambda i: (i, 0)
    return pl.pallas_call(
        _out_proj_kernel,
        grid=(n // tm,),
        in_specs=[pl.BlockSpec((tm, d), row)] + [pl.BlockSpec((tm, GROUP_W), row)] * 4 + [_full((d, d))],
        out_specs=pl.BlockSpec((tm, d), row),
        out_shape=jax.ShapeDtypeStruct((n, d), F32),
        compiler_params=_cparams("parallel"),
        name="out_proj",
    )(x, *ys, w_bf16)


def _final_norm_kernel(x_ref, g_ref, o_ref):
    o_ref[...] = _rmsnorm(x_ref[...], g_ref[...])


def _final_norm(x, g, tm):
    n, d = x.shape
    tm = min(tm, n)
    return pl.pallas_call(
        _final_norm_kernel,
        grid=(n // tm,),
        in_specs=[pl.BlockSpec((tm, d), lambda i: (i, 0)), _full((1, d))],
        out_specs=pl.BlockSpec((tm, d), lambda i: (i, 0)),
        out_shape=jax.ShapeDtypeStruct((n, d), F32),
        compiler_params=_cparams("parallel"),
        name="final_norm",
    )(x, g.reshape(1, d))


def _layernorm_lanes(x, g, b, eps):
    mu = jnp.mean(x, axis=-1, keepdims=True)
    xc = x - mu
    var = jnp.mean(xc * xc, axis=-1, keepdims=True)
    return xc * lax.rsqrt(var + eps) * g + b


def _conformer_tail(ca, lng, lnb):
    y = _layernorm_lanes(ca, lng, lnb, LN_EPS)
    return y * _sigmoid(y)


def _pool_select(w2, w4, w8, w16, shape):
    lane = lax.broadcasted_iota(jnp.int32, shape, len(shape) - 1)
    wsum = jnp.where(lane < 64, w2, jnp.where(lane < 128, w4, jnp.where(lane < 192, w8, w16)))
    win = jnp.where(lane < 64, 2.0, jnp.where(lane < 128, 4.0, jnp.where(lane < 192, 8.0, 16.0)))
    return wsum, win


def _head_sum(x, ones_bd):
    return _mm_exact_rhs(x, ones_bd)


def _rwkv_prep(p, prev, wr):
    (mu, w0, w2, a0, a2, g2, k_k, k_a, r_k, ones_bd) = wr
    xs = p + (prev - p) * mu
    r = xs[:, 0:256]
    k = xs[:, 256:512]
    v = xs[:, 512:768]
    w_lo = xs[:, 768:800]
    a_lo = xs[:, 800:832]
    g_lo = xs[:, 832:896]
    wexp = -_softplus(-(w0 + jnp.dot(jnp.tanh(w_lo).astype(BF16), w2, preferred_element_type=F32))) - 0.5
    logdecay = -jnp.exp(wexp)
    a = _sigmoid(a0 + jnp.dot(a_lo.astype(BF16), a2, preferred_element_type=F32))
    g = jnp.dot(_sigmoid(g_lo).astype(BF16), g2, preferred_element_type=F32)
    kk = k * k_k
    kk = kk * lax.rsqrt(jnp.maximum(_head_sum(kk * kk, ones_bd), 1e-24))
    k2 = k * (1.0 + (a - 1.0) * k_a)
    bonus = _head_sum(r * k2 * r_k, ones_bd) * v
    return r, logdecay, k2, v, kk, a, g, bonus


def _rwkv_post(y, bonus, g, lng, lnb, ones_bd):
    mu = _head_sum(y, ones_bd) * (1.0 / RWKV_HEAD)
    yc = y - mu
    var = _head_sum(yc * yc, ones_bd) * (1.0 / RWKV_HEAD)
    yn = yc * lax.rsqrt(var + RWKV_GN_EPS) * lng + lnb
    return (yn + bonus) * g


_CONV_ROWS = 64


def _prompt_acd_kernel(za_ref, zc_ref, zd_ref, conf0_ref, sc0_ref, pool0_ref,
                       cw_ref, cb_ref, clg_ref, clb_ref, scw_ref, pw_ref, ps_ref,
                       ya_ref, yc_ref, yd_ref, conf_o_ref, sc_o_ref, pool_o_ref,
                       ext_a, ext_c, ext_d, *, tt, start_pos):
    t = pl.program_id(1)

    @pl.when(t == 0)
    def _():
        ext_a[0:2, :] = jnp.zeros((2, GROUP_W), F32)
        ext_a[2:32, :] = conf0_ref[0]
        ext_c[0:6, :] = jnp.zeros((6, GROUP_W), F32)
        ext_c[6:8, :] = sc0_ref[0]
        ext_d[0:1, :] = jnp.zeros((1, GROUP_W), F32)
        ext_d[1:16, :] = pool0_ref[0]

    za = za_ref[0]
    ext_a[32:32 + tt, :] = za[:, 0:GROUP_W] * _sigmoid(za[:, GROUP_W:2 * GROUP_W])
    for c in range(tt // _CONV_ROWS):
        base = c * _CONV_ROWS
        acc = jnp.zeros((_CONV_ROWS, GROUP_W), F32) + cb_ref[...]
        for k in range(CONF_K):
            acc = acc + cw_ref[k:k + 1, :] * ext_a[base + k + 2:base + k + 2 + _CONV_ROWS, :]
        ya_ref[0, base:base + _CONV_ROWS, :] = _conformer_tail(acc, clg_ref[...], clb_ref[...])
    conf_o_ref[0] = ext_a[tt + 2:tt + 32, :]
    ext_a[0:32, :] = ext_a[tt:tt + 32, :]

    zc = zc_ref[0]
    ext_c[8:8 + tt, :] = zc[:, GROUP_W:2 * GROUP_W] * zc[:, 2 * GROUP_W:3 * GROUP_W]
    cc = (scw_ref[0:1, :] * ext_c[6:6 + tt, :] + scw_ref[1:2, :] * ext_c[7:7 + tt, :]
          + scw_ref[2:3, :] * ext_c[8:8 + tt, :])
    yc_ref[0] = zc[:, 0:GROUP_W] * cc
    sc_o_ref[0] = ext_c[tt + 6:tt + 8, :]
    ext_c[0:8, :] = ext_c[tt:tt + 8, :]

    u = zd_ref[0]
    ext_d[16:16 + tt, :] = u
    w2 = u + ext_d[15:15 + tt, :]
    w4 = w2 + ext_d[14:14 + tt, :] + ext_d[13:13 + tt, :]
    w8 = w4
    for j in range(4, 8):
        w8 = w8 + ext_d[16 - j:16 - j + tt, :]
    w16 = w8
    for j in range(8, 16):
        w16 = w16 + ext_d[16 - j:16 - j + tt, :]
    wsum, win = _pool_select(w2, w4, w8, w16, (tt, GROUP_W))
    pos = (lax.broadcasted_iota(jnp.int32, (tt, GROUP_W), 0) + (t * tt + start_pos + 1)).astype(F32)
    pooled = wsum / jnp.minimum(win, pos) - u
    yd_ref[0] = jnp.dot(pooled.astype(BF16), pw_ref[...], preferred_element_type=F32) * ps_ref[...]
    pool_o_ref[0] = ext_d[tt + 1:tt + 16, :]
    ext_d[0:16, :] = ext_d[tt:tt + 16, :]


def _prompt_acd(za, zc, zd, conf0, sc0, pool0, wl, tt, start_pos):
    b, t, _ = za.shape
    tile = lambda w: pl.BlockSpec((1, tt, w), lambda i, j: (i, j, 0))
    st = lambda r: pl.BlockSpec((1, r, GROUP_W), lambda i, j: (i, 0, 0))
    row = _full((1, GROUP_W))
    return pl.pallas_call(
        functools.partial(_prompt_acd_kernel, tt=tt, start_pos=start_pos),
        grid=(b, t // tt),
        in_specs=[tile(512), tile(768), tile(256), st(30), st(2), st(15),
                  _full((CONF_K, GROUP_W)), row, row, row, _full((3, GROUP_W)), _full((GROUP_W, GROUP_W)), row],
        out_specs=[tile(256), tile(256), tile(256), st(30), st(2), st(15)],
        out_shape=[jax.ShapeDtypeStruct((b, t, GROUP_W), F32)] * 3
        + [jax.ShapeDtypeStruct((b, r, GROUP_W), F32) for r in (30, 2, 15)],
        scratch_shapes=[pltpu.VMEM((32 + tt, GROUP_W), F32), pltpu.VMEM((8 + tt, GROUP_W), F32),
                        pltpu.VMEM((16 + tt, GROUP_W), F32)],
        compiler_params=_cparams("parallel", "arbitrary"),
        name="prompt_acd",
    )(za, zc, zd, conf0, sc0, pool0, wl["conf_dw_w"], wl["conf_dw_b"], wl["conf_ln_g"], wl["conf_ln_b"],
      wl["sc_conv_w"], wl["pool_wbd"], wl["pool_scale"])


def _prompt_rwkv_kernel(zb_ref, shift0_ref, wkv0_ref, mu_ref, w0_ref, w2_ref, a0_ref, a2_ref, g2_ref,
                        kk_ref, ka_ref, rk_ref, lng_ref, lnb_ref, ones_ref, tri_ref,
                        yb_ref, shift_o_ref, wkv_o_ref, prev_s, st_s, *, c, nb):
    t = pl.program_id(1)
    nt = pl.num_programs(1)

    @pl.when(t == 0)
    def _():
        for bb in range(nb):
            prev_s[bb] = jnp.broadcast_to(shift0_ref[bb], prev_s.shape[1:])
            for h in range(RWKV_H):
                st_s[bb, h] = _transpose_mxu(wkv0_ref[bb, h], _eye(RWKV_HEAD, BF16))

    new_states = _rwkv_chunks(zb_ref, mu_ref, w0_ref, w2_ref, a0_ref, a2_ref, g2_ref, kk_ref, ka_ref, rk_ref,
                              lng_ref, lnb_ref, ones_ref, tri_ref, yb_ref, shift_o_ref, prev_s, st_s, c, nb)

    @pl.when(t == nt - 1)
    def _():
        eye_h = _eye(RWKV_HEAD, BF16)
        for bb in range(nb):
            for h in range(RWKV_H):
                wkv_o_ref[bb, h] = _transpose_mxu(new_states[(bb, h)], eye_h)


def _rwkv_chunks(zb_ref, mu_ref, w0_ref, w2_ref, a0_ref, a2_ref, g2_ref, kk_ref, ka_ref, rk_ref,
                 lng_ref, lnb_ref, ones_ref, tri_ref, yb_ref, shift_o_ref, prev_s, st_s, c, nb):
    p = zb_ref[...].reshape(nb * c, RWKV_COLS)
    row = lax.broadcasted_iota(jnp.int32, p.shape, 0)
    prev = pltpu.roll(p, 1, axis=0)
    for bb in range(nb):
        prev = jnp.where(row == bb * c, prev_s[bb, 0:1, :], prev)
    ones_bd = ones_ref[...]
    wr = (mu_ref[...], w0_ref[...], w2_ref[...], a0_ref[...], a2_ref[...], g2_ref[...],
          kk_ref[...], ka_ref[...], rk_ref[...], ones_bd)
    r, logdecay, k2, v, kk, a, g, bonus = _rwkv_prep(p, prev, wr)

    cum = _mm_exact_lhs(tri_ref[...], logdecay)
    g_end = [cum[(bb + 1) * c - 1:(bb + 1) * c, :] for bb in range(nb)]
    cum_end = jnp.concatenate([jnp.broadcast_to(ge, (c, GROUP_W)) for ge in g_end], axis=0)
    g_end = [jnp.exp(ge) for ge in g_end]
    e_neg = jnp.exp(-cum)
    e_end = jnp.exp(cum_end - cum)
    bvec = kk * a
    a_t = -kk * jnp.exp(cum - logdecay)
    r_t = r * jnp.exp(cum)
    b_h = bvec * e_neg
    k_h = k2 * e_neg
    b_e = bvec * e_end
    k_e = k2 * e_end

    ri = lax.broadcasted_iota(jnp.int32, (2 * c, 2 * c), 0)
    ci = lax.broadcasted_iota(jnp.int32, (2 * c, 2 * c), 1)
    rt, cs = ri & (c - 1), ci & (c - 1)
    keep = (rt > cs) | ((ri >= c) & (rt == cs))
    eye_2h = _eye(2 * RWKV_HEAD, BF16)
    eye_f = _eye(RWKV_HEAD, F32)

    p_nt, p_rhs, p_neu, p_out = WKV_PASSES
    chains = [(bb, h) for bb in range(nb) for h in range(RWKV_H)]

    def part(z, bb, h):
        return z[bb * c:(bb + 1) * c, h * RWKV_HEAD:(h + 1) * RWKV_HEAD]

    big = {ch: jnp.where(keep, _mm(jnp.concatenate([part(a_t, *ch), part(r_t, *ch)], axis=0),
                                   jnp.concatenate([part(b_h, *ch), part(k_h, *ch)], axis=0), _NT, p_nt), 0.0)
           for ch in chains}
    s0 = {ch: st_s[ch[0], ch[1]] for ch in chains}
    vh = {ch: part(v, *ch) for ch in chains}
    sa = {ch: _mm(jnp.concatenate([part(a_t, *ch), big[ch][0:c, c:2 * c]], axis=1),
                  jnp.concatenate([s0[ch], vh[ch]], axis=0), _NN, p_rhs) for ch in chains}
    x = {ch: big[ch][0:c, 0:c] for ch in chains}
    n_sq = int(np.log2(c))
    for step in range(n_sq):
        sa = {ch: sa[ch] + _mm(x[ch], sa[ch], _NN, p_neu) for ch in chains}
        if step + 1 < n_sq:
            x = {ch: _mm(x[ch], x[ch], _NN, p_neu) for ch in chains}
    bk = {ch: jnp.concatenate([part(b_e, *ch), part(k_e, *ch)], axis=1) for ch in chains}
    bk_t = {ch: _dg(eye_2h, bk[ch].astype(BF16), _NT) if p_out == 1 else _transpose_mxu(bk[ch], eye_2h)
            for ch in chains}
    y_h = {ch: _mm(jnp.concatenate([part(r_t, *ch), big[ch][c:2 * c, 0:c], big[ch][c:2 * c, c:2 * c]], axis=1),
                   jnp.concatenate([s0[ch], sa[ch], vh[ch]], axis=0), _NN, p_out) for ch in chains}
    s_new = {}
    for ch in chains:
        bb, h = ch
        g_h = g_end[bb][:, h * RWKV_HEAD:(h + 1) * RWKV_HEAD]
        s_new[ch] = _mm3(eye_f * g_h, s0[ch]) + _mm(
            jnp.concatenate([bk_t[ch][0:RWKV_HEAD], bk_t[ch][RWKV_HEAD:]], axis=1),
            jnp.concatenate([sa[ch], vh[ch]], axis=0), _NN, p_out)
        st_s[bb, h] = s_new[ch]

    y = jnp.concatenate([jnp.concatenate([y_h[(bb, h)] for h in range(RWKV_H)], axis=-1) for bb in range(nb)], axis=0)
    yb = _rwkv_post(y, bonus, g, lng_ref[...], lnb_ref[...], ones_bd)
    for bb in range(nb):
        yb_ref[bb] = yb[bb * c:(bb + 1) * c]
        last = p[(bb + 1) * c - 1:(bb + 1) * c, :]
        prev_s[bb] = jnp.broadcast_to(last, prev_s.shape[1:])
        shift_o_ref[bb] = last
    return s_new


def _rwkv_weight_args(wl):
    return (wl["rwkv_mu"], wl["rwkv_w0"], wl["rwkv_w2"], wl["rwkv_a0"], wl["rwkv_a2"], wl["rwkv_g2"],
            wl["rwkv_k_k"], wl["rwkv_k_a"], wl["rwkv_r_k"], wl["rwkv_ln_g"], wl["rwkv_ln_b"], wl["ones_bd"])


_RWKV_WEIGHT_SPECS = [(1, RWKV_COLS), (1, GROUP_W), (32, GROUP_W), (1, GROUP_W), (32, GROUP_W), (64, GROUP_W),
                      (1, GROUP_W), (1, GROUP_W), (1, GROUP_W), (1, GROUP_W), (1, GROUP_W), (GROUP_W, GROUP_W)]


def _prompt_rwkv(zb, shift0, wkv0, wl, nb):
    b, t, _ = zb.shape
    c = WKV_CHUNK
    nb = min(nb, b)
    tri = jnp.kron(jnp.eye(nb, dtype=F32), jnp.tril(jnp.ones((c, c), F32))).astype(BF16)
    return pl.pallas_call(
        functools.partial(_prompt_rwkv_kernel, c=c, nb=nb),
        grid=(b // nb, t // c),
        in_specs=[pl.BlockSpec((nb, c, RWKV_COLS), lambda i, j: (i, j, 0)),
                  pl.BlockSpec((nb, 1, RWKV_COLS), lambda i, j: (i, 0, 0)),
                  pl.BlockSpec((nb, RWKV_H, RWKV_HEAD, RWKV_HEAD), lambda i, j: (i, 0, 0, 0))]
        + [_full(s) for s in _RWKV_WEIGHT_SPECS] + [_full((nb * c, nb * c))],
        out_specs=[pl.BlockSpec((nb, c, GROUP_W), lambda i, j: (i, j, 0)),
                   pl.BlockSpec((nb, 1, RWKV_COLS), lambda i, j: (i, 0, 0)),
                   pl.BlockSpec((nb, RWKV_H, RWKV_HEAD, RWKV_HEAD), lambda i, j: (i, 0, 0, 0))],
        out_shape=[jax.ShapeDtypeStruct((b, t, GROUP_W), F32), jax.ShapeDtypeStruct((b, 1, RWKV_COLS), F32),
                   jax.ShapeDtypeStruct((b, RWKV_H, RWKV_HEAD, RWKV_HEAD), F32)],
        scratch_shapes=[pltpu.VMEM((nb, 8, RWKV_COLS), F32), pltpu.VMEM((nb, RWKV_H, RWKV_HEAD, RWKV_HEAD), F32)],
        compiler_params=_cparams("parallel", "arbitrary"),
        name="prompt_rwkv",
    )(zb, shift0, wkv0, *_rwkv_weight_args(wl), tri)


def _decode_acd_prep_kernel(za_ref, zb_ref, zc_ref, zd_ref, conf_ref, shift_ref, sc_ref, pool_ref,
                            cw_ref, cb_ref, clg_ref, clb_ref, scw_ref, pw_ref, ps_ref,
                            mu_ref, w0_ref, w2_ref, a0_ref, a2_ref, g2_ref, kk_ref, ka_ref, rk_ref, ones_ref,
                            ya_ref, yc_ref, yd_ref, conf_o_ref, sc_o_ref, pool_o_ref,
                            r_o, w_o, k_o, v_o, kkn_o, b_o, g_o, bonus_o, *, start_pos):
    za = za_ref[...]
    glu = za[:, 0:GROUP_W] * _sigmoid(za[:, GROUP_W:2 * GROUP_W])
    acc = cb_ref[...] + cw_ref[CONF_K - 1:CONF_K, :] * glu
    for k in range(CONF_K - 1):
        acc = acc + cw_ref[k:k + 1, :] * conf_ref[k]
    ya_ref[...] = _conformer_tail(acc, clg_ref[...], clb_ref[...])
    for k in range(CONF_K - 2):
        conf_o_ref[k] = conf_ref[k + 1]
    conf_o_ref[CONF_K - 2] = glu

    zc = zc_ref[...]
    u = zc[:, GROUP_W:2 * GROUP_W] * zc[:, 2 * GROUP_W:3 * GROUP_W]
    cc = scw_ref[0:1, :] * sc_ref[0] + scw_ref[1:2, :] * sc_ref[1] + scw_ref[2:3, :] * u
    yc_ref[...] = zc[:, 0:GROUP_W] * cc
    sc_o_ref[0] = sc_ref[1]
    sc_o_ref[1] = u

    d = zd_ref[...]
    w2 = d + pool_ref[POOL_BUF - 1]
    w4 = w2 + pool_ref[POOL_BUF - 2] + pool_ref[POOL_BUF - 3]
    w8 = w4
    for j in range(4, 8):
        w8 = w8 + pool_ref[POOL_BUF - j]
    w16 = w8
    for j in range(8, 16):
        w16 = w16 + pool_ref[POOL_BUF - j]
    wsum, win = _pool_select(w2, w4, w8, w16, d.shape)
    pooled = wsum / jnp.minimum(win, float(start_pos + 1)) - d
    yd_ref[...] = jnp.dot(pooled.astype(BF16), pw_ref[...], preferred_element_type=F32) * ps_ref[...]
    for k in range(POOL_BUF - 1):
        pool_o_ref[k] = pool_ref[k + 1]
    pool_o_ref[POOL_BUF - 1] = d

    wr = (mu_ref[...], w0_ref[...], w2_ref[...], a0_ref[...], a2_ref[...], g2_ref[...],
          kk_ref[...], ka_ref[...], rk_ref[...], ones_ref[...])
    r, logdecay, k2, v, kk, a, g, bonus = _rwkv_prep(zb_ref[...], shift_ref[...], wr)
    r_o[...] = r
    w_o[...] = jnp.exp(logdecay)
    k_o[...] = k2
    v_o[...] = v
    kkn_o[...] = kk
    b_o[...] = kk * a
    g_o[...] = g
    bonus_o[...] = bonus


def _decode_acd_prep(za, zb, zc, zd, conf_t, shift, sc_t, pool_t, wl, start_pos):
    n = za.shape[0]
    ins = (za, zb, zc, zd, conf_t, shift, sc_t, pool_t, wl["conf_dw_w"], wl["conf_dw_b"], wl["conf_ln_g"],
           wl["conf_ln_b"], wl["sc_conv_w"], wl["pool_wbd"], wl["pool_scale"]) + _rwkv_weight_args(wl)[:9] + (wl["ones_bd"],)
    vec = jax.ShapeDtypeStruct((n, GROUP_W), F32)
    outs = [vec, vec, vec, jax.ShapeDtypeStruct(conf_t.shape, F32), jax.ShapeDtypeStruct(sc_t.shape, F32),
            jax.ShapeDtypeStruct(pool_t.shape, F32)] + [vec] * 8
    return pl.pallas_call(
        functools.partial(_decode_acd_prep_kernel, start_pos=start_pos),
        in_specs=[_full(x.shape) for x in ins],
        out_specs=[_full(o.shape) for o in outs],
        out_shape=outs,
        compiler_params=_cparams(),
        name="decode_acd_prep",
    )(*ins)


def _decode_wkv_kernel(s_ref, w_ref, kk_ref, b_ref, k_ref, r_ref, v_ref, s_o_ref, y_o_ref):
    s = s_ref[...]
    sa = -jnp.sum(s * kk_ref[...], axis=-1, keepdims=True)
    s_new = s * w_ref[...] + sa * b_ref[...] + v_ref[...] * k_ref[...]
    s_o_ref[...] = s_new
    y_o_ref[...] = jnp.sum(s_new * r_ref[...], axis=-1, keepdims=True)


def _decode_wkv(s, w, kk, bvec, k, r, v, blk):
    bh = s.shape[0]
    lane = pl.BlockSpec((blk, 1, RWKV_HEAD), lambda i: (i, 0, 0))
    col = pl.BlockSpec((blk, RWKV_HEAD, 1), lambda i: (i, 0, 0))
    mat = pl.BlockSpec((blk, RWKV_HEAD, RWKV_HEAD), lambda i: (i, 0, 0))
    return pl.pallas_call(
        _decode_wkv_kernel,
        grid=(bh // blk,),
        in_specs=[mat, lane, lane, lane, lane, lane, col],
        out_specs=[mat, col],
        out_shape=[jax.ShapeDtypeStruct(s.shape, F32), jax.ShapeDtypeStruct((bh, RWKV_HEAD, 1), F32)],
        compiler_params=_cparams("parallel"),
        name="decode_wkv",
    )(s, w, kk, bvec, k, r, v)


def _decode_post_kernel(y_ref, bonus_ref, g_ref, lng_ref, lnb_ref, ones_ref, o_ref):
    o_ref[...] = _rwkv_post(y_ref[...], bonus_ref[...], g_ref[...], lng_ref[...], lnb_ref[...], ones_ref[...])


def _decode_post(y, bonus, g, wl):
    ins = (y, bonus, g, wl["rwkv_ln_g"], wl["rwkv_ln_b"], wl["ones_bd"])
    return pl.pallas_call(
        _decode_post_kernel,
        in_specs=[_full(x.shape) for x in ins],
        out_specs=_full(y.shape),
        out_shape=jax.ShapeDtypeStruct(y.shape, F32),
        compiler_params=_cparams(),
        name="decode_post",
    )(*ins)


_CELLS = [(a, b) for a in range(PEER_TOPK) for b in range(PEER_TOPK) if (a + 1) * (b + 1) <= PEER_TOPK]
_CELL_PAIRS = [(c, d) for c in _CELLS for d in _CELLS if d[0] < c[0] and d[1] > c[1]]


def _top16_rows(s, tb, vals_ref, h, tie_safe):
    lanes = 128
    iota = lax.broadcasted_iota(jnp.int32, (N_KEYS, lanes), 0).astype(F32)
    ranks = []
    for c0 in range(0, tb, lanes):
        sc = s[:, c0:c0 + lanes]
        rank = jnp.full((N_KEYS, lanes), float(PEER_TOPK), F32)
        for r in range(PEER_TOPK):
            m = jnp.max(sc, axis=0, keepdims=True)
            sel = sc == m
            if tie_safe:
                sel = iota == jnp.min(jnp.where(sel, iota, float(N_KEYS)), axis=0, keepdims=True)
            rank = jnp.where(sel, float(r), rank)
            sc = jnp.where(sel, -jnp.inf, sc)
            vals_ref[r, h:h + 1, c0:c0 + lanes] = m
        ranks.append(rank)
    rank = jnp.concatenate(ranks, axis=1) if len(ranks) > 1 else ranks[0]
    n_sel = jnp.sum(jnp.where(rank < float(PEER_TOPK), 1.0, 0.0), axis=0, keepdims=True)
    return rank, n_sel


def _peer_route_kernel(h_ref, g_ref, wq_ref, k1_ref, k2_ref,
                       xn_ref, r2_ref, p2_ref, c1_ref, p1_ref,
                       v1_s, v2_s, r1_s, cnt_s, *, tb):
    refs = (h_ref, g_ref, wq_ref, k1_ref, k2_ref, xn_ref, r2_ref, p2_ref, c1_ref, p1_ref, v1_s, v2_s, r1_s, cnt_s)
    n_bad = _peer_route_body(*refs, tb=tb, tie_safe=False)

    @pl.when(jnp.max(n_bad) > 0.0)
    def _():
        _peer_route_body(*refs, tb=tb, tie_safe=True)


def _peer_route_body(h_ref, g_ref, wq_ref, k1_ref, k2_ref,
                     xn_ref, r2_ref, p2_ref, c1_ref, p1_ref,
                     v1_s, v2_s, r1_s, cnt_s, *, tb, tie_safe):
    xn = _rmsnorm(h_ref[...], g_ref[...])
    xb = xn.astype(BF16)
    xn_ref[...] = xn.T.astype(BF16)
    n_bad = jnp.zeros((1, tb), F32)
    for h in range(PEER_HEADS):
        q = jnp.dot(xb, wq_ref[:, h * 256:(h + 1) * 256], preferred_element_type=F32).astype(BF16)
        s1 = _dg(k1_ref[h], q[:, 0:128], _NT)
        s2 = _dg(k2_ref[h], q[:, 128:256], _NT)
        r1, n1 = _top16_rows(s1, tb, v1_s, h, tie_safe)
        r2, n2 = _top16_rows(s2, tb, v2_s, h, tie_safe)
        n_bad = n_bad + jnp.where(n1 != float(PEER_TOPK), 1.0, 0.0) + jnp.where(n2 != float(PEER_TOPK), 1.0, 0.0)
        r1_s[h] = r1
        r2_ref[h] = r2.astype(BF16)
        p1_ref[h] = jnp.exp(s1 - v1_s[0, h:h + 1, :])
        p2_ref[h] = jnp.exp(s2 - v2_s[0, h:h + 1, :]).astype(BF16)

    sums = {c: v1_s[c[0]] + v2_s[c[1]] for c in _CELLS}
    rank = {c: jnp.full((PEER_HEADS, tb), float((c[0] + 1) * (c[1] + 1) - 1), F32) for c in _CELLS}
    for c, d in _CELL_PAIRS:
        ge = sums[d] >= sums[c]
        rank[c] = rank[c] + jnp.where(ge, 1.0, 0.0)
        rank[d] = rank[d] + jnp.where(ge, 0.0, 1.0)
    e1 = [jnp.exp(v1_s[a] - v1_s[0]) for a in range(PEER_TOPK)]
    e2 = [jnp.exp(v2_s[b] - v2_s[0]) for b in range(PEER_TOPK)]
    z = jnp.zeros((PEER_HEADS, tb), F32)
    cnt = [jnp.zeros((PEER_HEADS, tb), F32) for _ in range(PEER_TOPK)]
    for c in _CELLS:
        sel = rank[c] < float(PEER_TOPK)
        cnt[c[0]] = cnt[c[0]] + jnp.where(sel, 1.0, 0.0)
        z = z + jnp.where(sel, e1[c[0]] * e2[c[1]], 0.0)
    for a in range(PEER_TOPK):
        cnt_s[a] = cnt[a]
    cnt_s[PEER_TOPK] = 0.5 / z

    for h in range(PEER_HEADS):
        r1 = r1_s[h]
        c1 = jnp.zeros((N_KEYS, tb), F32)
        for a in range(PEER_TOPK):
            c1 = jnp.where(r1 == float(a), cnt_s[a, h:h + 1, :], c1)
        c1_ref[h] = c1
        p1_ref[h] = p1_ref[h] * cnt_s[PEER_TOPK, h:h + 1, :]
    return n_bad


def _peer_route(hres, g, wq_bf16, k1_bf16, k2_bf16, tb):
    n, d = hres.shape
    tb = min(tb, n)
    gate = pl.BlockSpec((PEER_HEADS, N_KEYS, tb), lambda i: (0, 0, i))
    gshape = lambda dt: jax.ShapeDtypeStruct((PEER_HEADS, N_KEYS, n), dt)
    return pl.pallas_call(
        functools.partial(_peer_route_kernel, tb=tb),
        grid=(n // tb,),
        in_specs=[pl.BlockSpec((tb, d), lambda i: (i, 0)), _full((1, d)), _full(wq_bf16.shape),
                  _full(k1_bf16.shape), _full(k2_bf16.shape)],
        out_specs=[pl.BlockSpec((d, tb), lambda i: (0, i)), gate, gate, gate, gate],
        out_shape=[jax.ShapeDtypeStruct((d, n), BF16), gshape(BF16), gshape(BF16), gshape(F32), gshape(F32)],
        scratch_shapes=[pltpu.VMEM((PEER_TOPK, PEER_HEADS, tb), F32), pltpu.VMEM((PEER_TOPK, PEER_HEADS, tb), F32),
                        pltpu.VMEM((PEER_HEADS, N_KEYS, tb), F32), pltpu.VMEM((PEER_TOPK + 1, PEER_HEADS, tb), F32)],
        compiler_params=_cparams("parallel"),
        name="peer_route",
    )(hres, g.reshape(1, d), wq_bf16, k1_bf16, k2_bf16)


_SQRT_HALF = float(np.sqrt(0.5))


def _peer_expert_kernel(xn_ref, hres_ref, u_ref, vt_ref, r2_ref, p2_ref, c1_ref, p1_ref, o_ref,
                        acc_s, ht_a, ht_b, at_s, *, eb):
    j = pl.program_id(1)
    nblk = pl.num_programs(1) - 1
    per = eb // N_KEYS

    @pl.when(j == 0)
    def _():
        acc_s[...] = jnp.zeros_like(acc_s)
        ht_b[...] = jnp.zeros_like(ht_b)

    sub = EXPERT_SUB_BLOCK
    first = jnp.maximum(j - 1, 0) * per

    def score(ht_w, k):
        rows = slice(k * sub, (k + 1) * sub)
        ht_w[rows, :] = jnp.dot(u_ref[rows, :], xn_ref[...], preferred_element_type=F32)

    def gate(ht_r, i):
        i1 = first + i

        def row_tile(ref, h):
            row = jnp.broadcast_to(ref[h, pl.ds(i1, 1), :], (16, ref.shape[2])).astype(BF16)
            return pltpu.repeat(row, N_KEYS // 16, axis=0)

        gt = None
        for h in range(PEER_HEADS):
            c1 = row_tile(c1_ref, h)
            p1 = row_tile(p1_ref, h)
            term = jnp.where(r2_ref[h] < c1, p2_ref[h], jnp.zeros((), BF16)) * p1
            gt = term if gt is None else gt + term
        ht = ht_r[i * N_KEYS:(i + 1) * N_KEYS, :]
        act = ht * (1.0 + lax.erf(ht * _SQRT_HALF))
        at_s[i * N_KEYS:(i + 1) * N_KEYS, :] = act.astype(BF16) * gt

    def accumulate(k):
        rows = slice(k * sub, (k + 1) * sub)
        acc_s[...] += jnp.dot(vt_ref[:, rows], at_s[rows, :], preferred_element_type=F32)

    def step(ht_w, ht_r):
        for k in range(eb // sub):
            for i in range(k * sub // N_KEYS, (k + 1) * sub // N_KEYS):
                gate(ht_r, i)
            if ht_w is not None:
                score(ht_w, k)
            accumulate(k)

    even = (j & 1) == 0

    @pl.when(even & (j < nblk))
    def _():
        step(ht_a, ht_b)

    @pl.when(jnp.logical_not(even) & (j < nblk))
    def _():
        step(ht_b, ht_a)

    @pl.when(j == nblk)
    def _():
        @pl.when(even)
        def _():
            step(None, ht_b)

        @pl.when(jnp.logical_not(even))
        def _():
            step(None, ht_a)
        o_ref[...] = hres_ref[...] + acc_s[...].T


def _peer_experts(xn, hres, u_bf16, vt_bf16, gates, tb, eb):
    n, d = hres.shape
    tb = min(tb, n)
    nblk = u_bf16.shape[0] // eb
    gate = pl.BlockSpec((PEER_HEADS, N_KEYS, tb), lambda i, j: (0, 0, i))
    return pl.pallas_call(
        functools.partial(_peer_expert_kernel, eb=eb),
        grid=(n // tb, nblk + 1),
        in_specs=[pl.BlockSpec((d, tb), lambda i, j: (0, i)), pl.BlockSpec((tb, d), lambda i, j: (i, 0)),
                  pl.BlockSpec((eb, d), lambda i, j: (jnp.minimum(j, nblk - 1), 0)),
                  pl.BlockSpec((d, eb), lambda i, j: (0, jnp.maximum(j - 1, 0))),
                  gate, gate, gate, gate],
        out_specs=pl.BlockSpec((tb, d), lambda i, j: (i, 0)),
        out_shape=jax.ShapeDtypeStruct((n, d), F32),
        scratch_shapes=[pltpu.VMEM((d, tb), F32), pltpu.VMEM((eb, tb), F32), pltpu.VMEM((eb, tb), F32),
                        pltpu.VMEM((eb, tb), BF16)],
        compiler_params=_cparams("parallel", "arbitrary"),
        name="peer_experts",
    )(xn, hres, u_bf16, vt_bf16, *gates)


def _layer_weights(l, w):
    row = lambda a: a[l].reshape(1, -1)
    perm = _RWKV_PERM
    w_in = w["w_in"][l]
    w_in = jnp.concatenate([w_in[:, 0:512], w_in[:, 512:1408][:, perm], w_in[:, 1408:]], axis=1).astype(BF16)
    eye4 = jnp.eye(RWKV_H, dtype=F32)
    ones_bd = jnp.kron(eye4, jnp.ones((RWKV_HEAD, RWKV_HEAD), F32)).astype(BF16)
    pool_wbd = jax.scipy.linalg.block_diag(*[w["pool_w"][l, gi] for gi in range(4)]).astype(BF16)
    return dict(
        norm1_g=w["norm1_g"][l], norm2_g=w["norm2_g"][l], w_in=w_in,
        conf_dw_w=w["conf_dw_w"][l], conf_dw_b=row(w["conf_dw_b"]), conf_ln_g=row(w["conf_ln_g"]),
        conf_ln_b=row(w["conf_ln_b"]), sc_conv_w=w["sc_conv_w"][l], pool_wbd=pool_wbd, pool_scale=row(w["pool_scale"]),
        rwkv_mu=w["rwkv_mu"][l][perm].reshape(1, -1), rwkv_w0=row(w["rwkv_w0"]), rwkv_w2=w["rwkv_w2"][l].astype(BF16),
        rwkv_a0=row(w["rwkv_a0"]), rwkv_a2=w["rwkv_a2"][l].astype(BF16), rwkv_g2=w["rwkv_g2"][l].astype(BF16),
        rwkv_k_k=row(w["rwkv_k_k"]), rwkv_k_a=row(w["rwkv_k_a"]), rwkv_r_k=row(w["rwkv_r_k"]),
        rwkv_ln_g=row(w["rwkv_ln_g"]), rwkv_ln_b=row(w["rwkv_ln_b"]), ones_bd=ones_bd,
        w_out=w["w_out"][l].astype(BF16), peer_wq=w["peer_wq"][l].astype(BF16),
        peer_k1=w["peer_k1"][l].astype(BF16), peer_k2=w["peer_k2"][l].astype(BF16),
        peer_u=w["peer_u"][l].astype(BF16), peer_vt=w["peer_v"][l].T.astype(BF16),
    )


def _peer_block(hres, wl, tb_route, tb_exp, eb):
    xn, r2, p2, c1, p1 = _peer_route(hres, wl["norm2_g"], wl["peer_wq"], wl["peer_k1"], wl["peer_k2"], tb_route)
    return _peer_experts(xn, hres, wl["peer_u"], wl["peer_vt"], (r2, p2, c1, p1), tb_exp, eb)


def _prompt_layer(x, wl, bsz, t):
    n = bsz * t
    za, zb, zc, zd = _norm_proj(x, wl["norm1_g"], wl["w_in"], IN_SPLITS, 512)
    z3 = lambda a: a.reshape(bsz, t, -1)
    zeros = lambda *s: jnp.zeros(s, F32)
    ya, yc, yd, conf, sc, pool = _prompt_acd(z3(za), z3(zc), z3(zd), zeros(bsz, 30, GROUP_W), zeros(bsz, 2, GROUP_W),
                                             zeros(bsz, 15, GROUP_W), wl, 256, 0)
    yb, shift, wkv = _prompt_rwkv(z3(zb), zeros(bsz, 1, RWKV_COLS), zeros(bsz, RWKV_H, RWKV_HEAD, RWKV_HEAD), wl,
                                   WKV_SEQS_PER_STEP)
    flat = lambda a: a.reshape(n, GROUP_W)
    hres = _out_proj(x, (flat(ya), flat(yb), flat(yc), flat(yd)), wl["w_out"], 512)
    x = _peer_block(hres, wl, 256, 512, 1024)
    return x, (conf, shift.reshape(bsz, RWKV_COLS)[:, _RWKV_INV_PERM], wkv, sc, pool)


def _decode_layer(x, states, wl, start_pos):
    conf, shift, wkv, sc, pool = states
    n = x.shape[0]
    za, zb, zc, zd = _norm_proj(x, wl["norm1_g"], wl["w_in"], IN_SPLITS, 128)
    tr = lambda a: jnp.transpose(a, (1, 0, 2))
    (ya, yc, yd, conf_n, sc_n, pool_n, r, w, k2, v, kk, bvec, g, bonus) = _decode_acd_prep(
        za, zb, zc, zd, tr(conf), shift[:, _RWKV_PERM], tr(sc), tr(pool), wl, start_pos)
    bh = n * RWKV_H
    lane = lambda a: a.reshape(bh, 1, RWKV_HEAD)
    s_new, y = _decode_wkv(wkv.reshape(bh, RWKV_HEAD, RWKV_HEAD), lane(w), lane(kk), lane(bvec), lane(k2), lane(r),
                           v.reshape(bh, RWKV_HEAD, 1), 64)
    yb = _decode_post(y.reshape(n, GROUP_W), bonus, g, wl)
    hres = _out_proj(x, (ya, yb, yc, yd), wl["w_out"], 128)
    x = _peer_block(hres, wl, 128, 128, 1024)
    return x, (tr(conf_n), zb[:, _RWKV_INV_PERM], s_new.reshape(n, RWKV_H, RWKV_HEAD, RWKV_HEAD), tr(sc_n), tr(pool_n))


def kernel(x_prompt, x_sample, state_conformer, state_rwkv_shift, state_rwkv_wkv, state_shortconv, state_pool, norm1_g, norm2_g, final_norm_g, w_in, conf_dw_w, conf_dw_b, conf_ln_g, conf_ln_b, rwkv_mu, rwkv_w0, rwkv_w2, rwkv_a0, rwkv_a2, rwkv_g2, rwkv_k_k, rwkv_k_a, rwkv_r_k, rwkv_ln_g, rwkv_ln_b, sc_conv_w, pool_w, pool_scale, w_out, peer_wq, peer_k1, peer_k2, peer_u, peer_v):
    w = dict(norm1_g=norm1_g, norm2_g=norm2_g, w_in=w_in, conf_dw_w=conf_dw_w, conf_dw_b=conf_dw_b,
             conf_ln_g=conf_ln_g, conf_ln_b=conf_ln_b, rwkv_mu=rwkv_mu, rwkv_w0=rwkv_w0, rwkv_w2=rwkv_w2,
             rwkv_a0=rwkv_a0, rwkv_a2=rwkv_a2, rwkv_g2=rwkv_g2, rwkv_k_k=rwkv_k_k, rwkv_k_a=rwkv_k_a,
             rwkv_r_k=rwkv_r_k, rwkv_ln_g=rwkv_ln_g, rwkv_ln_b=rwkv_ln_b, sc_conv_w=sc_conv_w, pool_w=pool_w,
             pool_scale=pool_scale, w_out=w_out, peer_wq=peer_wq, peer_k1=peer_k1, peer_k2=peer_k2,
             peer_u=peer_u, peer_v=peer_v)
    depth = w_in.shape[0]
    bsz, t, d = x_prompt.shape
    nb, dt, _ = x_sample.shape
    past_len = 16384

    xp = x_prompt.reshape(bsz * t, d)
    xs = x_sample.reshape(nb * dt, d)
    p_states, s_states = [], []
    for l in range(depth):
        wl = _layer_weights(l, w)
        xp, ps = _prompt_layer(xp, wl, bsz, t)
        xs, ss = _decode_layer(xs, (state_conformer[l], state_rwkv_shift[l], state_rwkv_wkv[l],
                                    state_shortconv[l], state_pool[l]), wl, past_len)
        p_states.append(ps)
        s_states.append(ss)
    y_prompt = _final_norm(xp, final_norm_g, 512).reshape(bsz, t, d)
    y_sample = _final_norm(xs, final_norm_g, 128).reshape(nb, dt, d)
    stack = lambda lst, i: jnp.stack([s[i] for s in lst], axis=0)
    conf_p, shift_p, wkv_p, sc_p, pool_p = (stack(p_states, i) for i in range(5))
    conf_s, shift_s, wkv_s, sc_s, pool_s = (stack(s_states, i) for i in range(5))
    return (y_prompt, y_sample, conf_p, conf_s, shift_p, shift_s, wkv_p, wkv_s, sc_p, sc_s, pool_p, pool_s)
```

```python
import functools

import jax
import jax.numpy as jnp
import numpy as np
from jax import lax
from jax.experimental import pallas as pl
from jax.experimental.pallas import tpu as pltpu

F32 = jnp.float32
BF16 = jnp.bfloat16

D_MODEL = 1024
GROUP_W = 256
CONF_K = 31
RWKV_HEAD = 64
RWKV_H = 4
RWKV_COLS = 896
RWKV_GN_EPS = 64e-5
POOL_WINDOWS = (2, 4, 8, 16)
POOL_BUF = 15
N_KEYS = 128
PEER_HEADS = 8
PEER_TOPK = 16
RMS_EPS = 1e-6
LN_EPS = 1e-5
IN_SPLITS = (512, 896, 768, 256)

VMEM_LIMIT_BYTES = 56 * 1024 * 1024
WKV_CHUNK = 64
EXPERT_BLOCK = 2048
WKV_SEQS_PER_STEP = 8
WKV_PASSES = (1, 1, 1, 1)

_RWKV_PERM = np.concatenate([np.arange(0, 256), np.arange(288, 544), np.arange(544, 800),
                             np.arange(256, 288), np.arange(800, 832), np.arange(832, 896)])
_RWKV_INV_PERM = np.argsort(_RWKV_PERM)


def _cparams(*sem):
    return pltpu.CompilerParams(dimension_semantics=tuple(sem) if sem else None,
                                vmem_limit_bytes=VMEM_LIMIT_BYTES)


def _full(shape):
    n = len(shape)
    return pl.BlockSpec(shape, lambda *_: (0,) * n)


def _split2(x):
    hi = x.astype(BF16)
    lo = (x - hi.astype(F32)).astype(BF16)
    return hi, lo


def _split3(x):
    hi = x.astype(BF16)
    r = x - hi.astype(F32)
    mid = r.astype(BF16)
    lo = (r - mid.astype(F32)).astype(BF16)
    return hi, mid, lo


_NN = (((1,), (0,)), ((), ()))
_NT = (((1,), (1,)), ((), ()))


def _dg(a, b, dims):
    return lax.dot_general(a, b, dims, preferred_element_type=F32)


def _mm3(a, b, dims=_NN):
    ah, al = _split2(a)
    bh, bl = _split2(b)
    return _dg(ah, bh, dims) + (_dg(al, bh, dims) + _dg(ah, bl, dims))


def _mm(a, b, dims=_NN, passes=3):
    if passes == 1:
        return _dg(a.astype(BF16), b.astype(BF16), dims)
    return _mm3(a, b, dims)


def _mm_exact_rhs(a, b_bf16):
    h, m, l = _split3(a)
    return _dg(h, b_bf16, _NN) + (_dg(m, b_bf16, _NN) + _dg(l, b_bf16, _NN))


def _mm_exact_lhs(a_bf16, b):
    h, m, l = _split3(b)
    return _dg(a_bf16, h, _NN) + (_dg(a_bf16, m, _NN) + _dg(a_bf16, l, _NN))


def _transpose_mxu(x, eye_bf16):
    h, m, l = _split3(x)
    return _dg(eye_bf16, h, _NT) + (_dg(eye_bf16, m, _NT) + _dg(eye_bf16, l, _NT))


def _eye(n, dtype):
    return (lax.broadcasted_iota(jnp.int32, (n, n), 0) == lax.broadcasted_iota(jnp.int32, (n, n), 1)).astype(dtype)


def _rmsnorm(x, g):
    ms = jnp.mean(x * x, axis=-1, keepdims=True)
    return x * lax.rsqrt(ms + RMS_EPS) * g


def _sigmoid(x):
    return 1.0 / (1.0 + jnp.exp(-x))


def _softplus(x):
    return jnp.maximum(x, 0.0) + jnp.log(1.0 + jnp.exp(-jnp.abs(x)))


def _norm_proj_kernel(x_ref, g_ref, w_ref, *o_refs, splits):
    xb = _rmsnorm(x_ref[...], g_ref[...]).astype(BF16)
    off = 0
    for o_ref, wd in zip(o_refs, splits):
        o_ref[...] = jnp.dot(xb, w_ref[:, off:off + wd], preferred_element_type=F32)
        off += wd


def _norm_proj(x, g, w_bf16, splits, tm):
    n, d = x.shape
    tm = min(tm, n)
    cols = w_bf16.shape[1]
    return pl.pallas_call(
        functools.partial(_norm_proj_kernel, splits=splits),
        grid=(n // tm,),
        in_specs=[pl.BlockSpec((tm, d), lambda i: (i, 0)), _full((1, d)), _full((d, cols))],
        out_specs=[pl.BlockSpec((tm, wd), lambda i: (i, 0)) for wd in splits],
        out_shape=[jax.ShapeDtypeStruct((n, wd), F32) for wd in splits],
        compiler_params=_cparams("parallel"),
        name="norm_proj",
    )(x, g.reshape(1, d), w_bf16)


def _out_proj_kernel(x_ref, ya_ref, yb_ref, yc_ref, yd_ref, w_ref, o_ref):
    acc = x_ref[...]
    for i, y_ref in enumerate((ya_ref, yb_ref, yc_ref, yd_ref)):
        acc = acc + jnp.dot(y_ref[...].astype(BF16), w_ref[i * GROUP_W:(i + 1) * GROUP_W, :],
                            preferred_element_type=F32)
    o_ref[...] = acc


def _out_proj(x, ys, w_bf16, tm):
    n, d = x.shape
    tm = min(tm, n)
    row = lambda i: (i, 0)
    return pl.pallas_call(
        _out_proj_kernel,
        grid=(n // tm,),
        in_specs=[pl.BlockSpec((tm, d), row)] + [pl.BlockSpec((tm, GROUP_W), row)] * 4 + [_full((d, d))],
        out_specs=pl.BlockSpec((tm, d), row),
        out_shape=jax.ShapeDtypeStruct((n, d), F32),
        compiler_params=_cparams("parallel"),
        name="out_proj",
    )(x, *ys, w_bf16)


def _final_norm_kernel(x_ref, g_ref, o_ref):
    o_ref[...] = _rmsnorm(x_ref[...], g_ref[...])


def _final_norm(x, g, tm):
    n, d = x.shape
    tm = min(tm, n)
    return pl.pallas_call(
        _final_norm_kernel,
        grid=(n // tm,),
        in_specs=[pl.BlockSpec((tm, d), lambda i: (i, 0)), _full((1, d))],
        out_specs=pl.BlockSpec((tm, d), lambda i: (i, 0)),
        out_shape=jax.ShapeDtypeStruct((n, d), F32),
        compiler_params=_cparams("parallel"),
        name="final_norm",
    )(x, g.reshape(1, d))


def _layernorm_lanes(x, g, b, eps):
    mu = jnp.mean(x, axis=-1, keepdims=True)
    xc = x - mu
    var = jnp.mean(xc * xc, axis=-1, keepdims=True)
    return xc * lax.rsqrt(var + eps) * g + b


def _conformer_tail(ca, lng, lnb):
    y = _layernorm_lanes(ca, lng, lnb, LN_EPS)
    return y * _sigmoid(y)


def _pool_select(w2, w4, w8, w16, shape):
    lane = lax.broadcasted_iota(jnp.int32, shape, len(shape) - 1)
    wsum = jnp.where(lane < 64, w2, jnp.where(lane < 128, w4, jnp.where(lane < 192, w8, w16)))
    win = jnp.where(lane < 64, 2.0, jnp.where(lane < 128, 4.0, jnp.where(lane < 192, 8.0, 16.0)))
    return wsum, win


def _head_sum(x, ones_bd):
    return _mm_exact_rhs(x, ones_bd)


def _rwkv_prep(p, prev, wr):
    (mu, w0, w2, a0, a2, g2, k_k, k_a, r_k, ones_bd) = wr
    xs = p + (prev - p) * mu
    r = xs[:, 0:256]
    k = xs[:, 256:512]
    v = xs[:, 512:768]
    w_lo = xs[:, 768:800]
    a_lo = xs[:, 800:832]
    g_lo = xs[:, 832:896]
    wexp = -_softplus(-(w0 + jnp.dot(jnp.tanh(w_lo).astype(BF16), w2, preferred_element_type=F32))) - 0.5
    logdecay = -jnp.exp(wexp)
    a = _sigmoid(a0 + jnp.dot(a_lo.astype(BF16), a2, preferred_element_type=F32))
    g = jnp.dot(_sigmoid(g_lo).astype(BF16), g2, preferred_element_type=F32)
    kk = k * k_k
    kk = kk * lax.rsqrt(jnp.maximum(_head_sum(kk * kk, ones_bd), 1e-24))
    k2 = k * (1.0 + (a - 1.0) * k_a)
    bonus = _head_sum(r * k2 * r_k, ones_bd) * v
    return r, logdecay, k2, v, kk, a, g, bonus


def _rwkv_post(y, bonus, g, lng, lnb, ones_bd):
    mu = _head_sum(y, ones_bd) * (1.0 / RWKV_HEAD)
    yc = y - mu
    var = _head_sum(yc * yc, ones_bd) * (1.0 / RWKV_HEAD)
    yn = yc * lax.rsqrt(var + RWKV_GN_EPS) * lng + lnb
    return (yn + bonus) * g


_CONV_ROWS = 64


def _prompt_acd_kernel(za_ref, zc_ref, zd_ref, conf0_ref, sc0_ref, pool0_ref,
                       cw_ref, cb_ref, clg_ref, clb_ref, scw_ref, pw_ref, ps_ref,
                       ya_ref, yc_ref, yd_ref, conf_o_ref, sc_o_ref, pool_o_ref,
                       ext_a, ext_c, ext_d, *, tt, start_pos):
    t = pl.program_id(1)

    @pl.when(t == 0)
    def _():
        ext_a[0:2, :] = jnp.zeros((2, GROUP_W), F32)
        ext_a[2:32, :] = conf0_ref[0]
        ext_c[0:6, :] = jnp.zeros((6, GROUP_W), F32)
        ext_c[6:8, :] = sc0_ref[0]
        ext_d[0:1, :] = jnp.zeros((1, GROUP_W), F32)
        ext_d[1:16, :] = pool0_ref[0]

    za = za_ref[0]
    ext_a[32:32 + tt, :] = za[:, 0:GROUP_W] * _sigmoid(za[:, GROUP_W:2 * GROUP_W])
    for c in range(tt // _CONV_ROWS):
        base = c * _CONV_ROWS
        acc = jnp.zeros((_CONV_ROWS, GROUP_W), F32) + cb_ref[...]
        for k in range(CONF_K):
            acc = acc + cw_ref[k:k + 1, :] * ext_a[base + k + 2:base + k + 2 + _CONV_ROWS, :]
        ya_ref[0, base:base + _CONV_ROWS, :] = _conformer_tail(acc, clg_ref[...], clb_ref[...])
    conf_o_ref[0] = ext_a[tt + 2:tt + 32, :]
    ext_a[0:32, :] = ext_a[tt:tt + 32, :]

    zc = zc_ref[0]
    ext_c[8:8 + tt, :] = zc[:, GROUP_W:2 * GROUP_W] * zc[:, 2 * GROUP_W:3 * GROUP_W]
    cc = (scw_ref[0:1, :] * ext_c[6:6 + tt, :] + scw_ref[1:2, :] * ext_c[7:7 + tt, :]
          + scw_ref[2:3, :] * ext_c[8:8 + tt, :])
    yc_ref[0] = zc[:, 0:GROUP_W] * cc
    sc_o_ref[0] = ext_c[tt + 6:tt + 8, :]
    ext_c[0:8, :] = ext_c[tt:tt + 8, :]

    u = zd_ref[0]
    ext_d[16:16 + tt, :] = u
    w2 = u + ext_d[15:15 + tt, :]
    w4 = w2 + ext_d[14:14 + tt, :] + ext_d[13:13 + tt, :]
    w8 = w4
    for j in range(4, 8):
        w8 = w8 + ext_d[16 - j:16 - j + tt, :]
    w16 = w8
    for j in range(8, 16):
        w16 = w16 + ext_d[16 - j:16 - j + tt, :]
    wsum, win = _pool_select(w2, w4, w8, w16, (tt, GROUP_W))
    pos = (lax.broadcasted_iota(jnp.int32, (tt, GROUP_W), 0) + (t * tt + start_pos + 1)).astype(F32)
    pooled = wsum / jnp.minimum(win, pos) - u
    yd_ref[0] = jnp.dot(pooled.astype(BF16), pw_ref[...], preferred_element_type=F32) * ps_ref[...]
    pool_o_ref[0] = ext_d[tt + 1:tt + 16, :]
    ext_d[0:16, :] = ext_d[tt:tt + 16, :]


def _prompt_acd(za, zc, zd, conf0, sc0, pool0, wl, tt, start_pos):
    b, t, _ = za.shape
    tile = lambda w: pl.BlockSpec((1, tt, w), lambda i, j: (i, j, 0))
    st = lambda r: pl.BlockSpec((1, r, GROUP_W), lambda i, j: (i, 0, 0))
    row = _full((1, GROUP_W))
    return pl.pallas_call(
        functools.partial(_prompt_acd_kernel, tt=tt, start_pos=start_pos),
        grid=(b, t // tt),
        in_specs=[tile(512), tile(768), tile(256), st(30), st(2), st(15),
                  _full((CONF_K, GROUP_W)), row, row, row, _full((3, GROUP_W)), _full((GROUP_W, GROUP_W)), row],
        out_specs=[tile(256), tile(256), tile(256), st(30), st(2), st(15)],
        out_shape=[jax.ShapeDtypeStruct((b, t, GROUP_W), F32)] * 3
        + [jax.ShapeDtypeStruct((b, r, GROUP_W), F32) for r in (30, 2, 15)],
        scratch_shapes=[pltpu.VMEM((32 + tt, GROUP_W), F32), pltpu.VMEM((8 + tt, GROUP_W), F32),
                        pltpu.VMEM((16 + tt, GROUP_W), F32)],
        compiler_params=_cparams("parallel", "arbitrary"),
        name="prompt_acd",
    )(za, zc, zd, conf0, sc0, pool0, wl["conf_dw_w"], wl["conf_dw_b"], wl["conf_ln_g"], wl["conf_ln_b"],
      wl["sc_conv_w"], wl["pool_wbd"], wl["pool_scale"])


def _prompt_rwkv_kernel(zb_ref, shift0_ref, wkv0_ref, mu_ref, w0_ref, w2_ref, a0_ref, a2_ref, g2_ref,
                        kk_ref, ka_ref, rk_ref, lng_ref, lnb_ref, ones_ref, tri_ref,
                        yb_ref, shift_o_ref, wkv_o_ref, prev_s, st_s, *, c, nb):
    t = pl.program_id(1)
    nt = pl.num_programs(1)

    @pl.when(t == 0)
    def _():
        for bb in range(nb):
            prev_s[bb] = jnp.broadcast_to(shift0_ref[bb], prev_s.shape[1:])
            for h in range(RWKV_H):
                st_s[bb, h] = _transpose_mxu(wkv0_ref[bb, h], _eye(RWKV_HEAD, BF16))

    new_states = _rwkv_chunks(zb_ref, mu_ref, w0_ref, w2_ref, a0_ref, a2_ref, g2_ref, kk_ref, ka_ref, rk_ref,
                              lng_ref, lnb_ref, ones_ref, tri_ref, yb_ref, shift_o_ref, prev_s, st_s, c, nb)

    @pl.when(t == nt - 1)
    def _():
        eye_h = _eye(RWKV_HEAD, BF16)
        for bb in range(nb):
            for h in range(RWKV_H):
                wkv_o_ref[bb, h] = _transpose_mxu(new_states[(bb, h)], eye_h)


def _rwkv_chunks(zb_ref, mu_ref, w0_ref, w2_ref, a0_ref, a2_ref, g2_ref, kk_ref, ka_ref, rk_ref,
                 lng_ref, lnb_ref, ones_ref, tri_ref, yb_ref, shift_o_ref, prev_s, st_s, c, nb):
    p = zb_ref[...].reshape(nb * c, RWKV_COLS)
    row = lax.broadcasted_iota(jnp.int32, p.shape, 0)
    prev = pltpu.roll(p, 1, axis=0)
    for bb in range(nb):
        prev = jnp.where(row == bb * c, prev_s[bb, 0:1, :], prev)
    ones_bd = ones_ref[...]
    wr = (mu_ref[...], w0_ref[...], w2_ref[...], a0_ref[...], a2_ref[...], g2_ref[...],
          kk_ref[...], ka_ref[...], rk_ref[...], ones_bd)
    r, logdecay, k2, v, kk, a, g, bonus = _rwkv_prep(p, prev, wr)

    cum = _mm_exact_lhs(tri_ref[...], logdecay)
    g_end = [cum[(bb + 1) * c - 1:(bb + 1) * c, :] for bb in range(nb)]
    cum_end = jnp.concatenate([jnp.broadcast_to(ge, (c, GROUP_W)) for ge in g_end], axis=0)
    g_end = [jnp.exp(ge) for ge in g_end]
    e_neg = jnp.exp(-cum)
    e_end = jnp.exp(cum_end - cum)
    bvec = kk * a
    a_t = -kk * jnp.exp(cum - logdecay)
    r_t = r * jnp.exp(cum)
    b_h = bvec * e_neg
    k_h = k2 * e_neg
    b_e = bvec * e_end
    k_e = k2 * e_end

    ri = lax.broadcasted_iota(jnp.int32, (2 * c, 2 * c), 0)
    ci = lax.broadcasted_iota(jnp.int32, (2 * c, 2 * c), 1)
    rt, cs = ri & (c - 1), ci & (c - 1)
    keep = (rt > cs) | ((ri >= c) & (rt == cs))
    eye_2h = _eye(2 * RWKV_HEAD, BF16)
    eye_f = _eye(RWKV_HEAD, F32)

    p_nt, p_rhs, p_neu, p_out = WKV_PASSES
    chains = [(bb, h) for bb in range(nb) for h in range(RWKV_H)]

    def part(z, bb, h):
        return z[bb * c:(bb + 1) * c, h * RWKV_HEAD:(h + 1) * RWKV_HEAD]

    big = {ch: jnp.where(keep, _mm(jnp.concatenate([part(a_t, *ch), part(r_t, *ch)], axis=0),
                                   jnp.concatenate([part(b_h, *ch), part(k_h, *ch)], axis=0), _NT, p_nt), 0.0)
           for ch in chains}
    s0 = {ch: st_s[ch[0], ch[1]] for ch in chains}
    vh = {ch: part(v, *ch) for ch in chains}
    sa = {ch: _mm(jnp.concatenate([part(a_t, *ch), big[ch][0:c, c:2 * c]], axis=1),
                  jnp.concatenate([s0[ch], vh[ch]], axis=0), _NN, p_rhs) for ch in chains}
    x = {ch: big[ch][0:c, 0:c] for ch in chains}
    n_sq = int(np.log2(c))
    for step in range(n_sq):
        sa = {ch: sa[ch] + _mm(x[ch], sa[ch], _NN, p_neu) for ch in chains}
        if step + 1 < n_sq:
            x = {ch: _mm(x[ch], x[ch], _NN, p_neu) for ch in chains}
    bk = {ch: jnp.concatenate([part(b_e, *ch), part(k_e, *ch)], axis=1) for ch in chains}
    bk_t = {ch: _dg(eye_2h, bk[ch].astype(BF16), _NT) if p_out == 1 else _transpose_mxu(bk[ch], eye_2h)
            for ch in chains}
    y_h = {ch: _mm(jnp.concatenate([part(r_t, *ch), big[ch][c:2 * c, 0:c], big[ch][c:2 * c, c:2 * c]], axis=1),
                   jnp.concatenate([s0[ch], sa[ch], vh[ch]], axis=0), _NN, p_out) for ch in chains}
    s_new = {}
    for ch in chains:
        bb, h = ch
        g_h = g_end[bb][:, h * RWKV_HEAD:(h + 1) * RWKV_HEAD]
        s_new[ch] = _mm3(eye_f * g_h, s0[ch]) + _mm(
            jnp.concatenate([bk_t[ch][0:RWKV_HEAD], bk_t[ch][RWKV_HEAD:]], axis=1),
            jnp.concatenate([sa[ch], vh[ch]], axis=0), _NN, p_out)
        st_s[bb, h] = s_new[ch]

    y = jnp.concatenate([jnp.concatenate([y_h[(bb, h)] for h in range(RWKV_H)], axis=-1) for bb in range(nb)], axis=0)
    yb = _rwkv_post(y, bonus, g, lng_ref[...], lnb_ref[...], ones_bd)
    for bb in range(nb):
        yb_ref[bb] = yb[bb * c:(bb + 1) * c]
        last = p[(bb + 1) * c - 1:(bb + 1) * c, :]
        prev_s[bb] = jnp.broadcast_to(last, prev_s.shape[1:])
        shift_o_ref[bb] = last
    return s_new


def _rwkv_weight_args(wl):
    return (wl["rwkv_mu"], wl["rwkv_w0"], wl["rwkv_w2"], wl["rwkv_a0"], wl["rwkv_a2"], wl["rwkv_g2"],
            wl["rwkv_k_k"], wl["rwkv_k_a"], wl["rwkv_r_k"], wl["rwkv_ln_g"], wl["rwkv_ln_b"], wl["ones_bd"])


_RWKV_WEIGHT_SPECS = [(1, RWKV_COLS), (1, GROUP_W), (32, GROUP_W), (1, GROUP_W), (32, GROUP_W), (64, GROUP_W),
                      (1, GROUP_W), (1, GROUP_W), (1, GROUP_W), (1, GROUP_W), (1, GROUP_W), (GROUP_W, GROUP_W)]


def _prompt_rwkv(zb, shift0, wkv0, wl, nb):
    b, t, _ = zb.shape
    c = WKV_CHUNK
    nb = min(nb, b)
    tri = jnp.kron(jnp.eye(nb, dtype=F32), jnp.tril(jnp.ones((c, c), F32))).astype(BF16)
    return pl.pallas_call(
        functools.partial(_prompt_rwkv_kernel, c=c, nb=nb),
        grid=(b // nb, t // c),
        in_specs=[pl.BlockSpec((nb, c, RWKV_COLS), lambda i, j: (i, j, 0)),
                  pl.BlockSpec((nb, 1, RWKV_COLS), lambda i, j: (i, 0, 0)),
                  pl.BlockSpec((nb, RWKV_H, RWKV_HEAD, RWKV_HEAD), lambda i, j: (i, 0, 0, 0))]
        + [_full(s) for s in _RWKV_WEIGHT_SPECS] + [_full((nb * c, nb * c))],
        out_specs=[pl.BlockSpec((nb, c, GROUP_W), lambda i, j: (i, j, 0)),
                   pl.BlockSpec((nb, 1, RWKV_COLS), lambda i, j: (i, 0, 0)),
                   pl.BlockSpec((nb, RWKV_H, RWKV_HEAD, RWKV_HEAD), lambda i, j: (i, 0, 0, 0))],
        out_shape=[jax.ShapeDtypeStruct((b, t, GROUP_W), F32), jax.ShapeDtypeStruct((b, 1, RWKV_COLS), F32),
                   jax.ShapeDtypeStruct((b, RWKV_H, RWKV_HEAD, RWKV_HEAD), F32)],
        scratch_shapes=[pltpu.VMEM((nb, 8, RWKV_COLS), F32), pltpu.VMEM((nb, RWKV_H, RWKV_HEAD, RWKV_HEAD), F32)],
        compiler_params=_cparams("parallel", "arbitrary"),
        name="prompt_rwkv",
    )(zb, shift0, wkv0, *_rwkv_weight_args(wl), tri)


def _decode_acd_prep_kernel(za_ref, zb_ref, zc_ref, zd_ref, conf_ref, shift_ref, sc_ref, pool_ref,
                            cw_ref, cb_ref, clg_ref, clb_ref, scw_ref, pw_ref, ps_ref,
                            mu_ref, w0_ref, w2_ref, a0_ref, a2_ref, g2_ref, kk_ref, ka_ref, rk_ref, ones_ref,
                            ya_ref, yc_ref, yd_ref, conf_o_ref, sc_o_ref, pool_o_ref,
                            r_o, w_o, k_o, v_o, kkn_o, b_o, g_o, bonus_o, *, start_pos):
    za = za_ref[...]
    glu = za[:, 0:GROUP_W] * _sigmoid(za[:, GROUP_W:2 * GROUP_W])
    acc = cb_ref[...] + cw_ref[CONF_K - 1:CONF_K, :] * glu
    for k in range(CONF_K - 1):
        acc = acc + cw_ref[k:k + 1, :] * conf_ref[k]
    ya_ref[...] = _conformer_tail(acc, clg_ref[...], clb_ref[...])
    for k in range(CONF_K - 2):
        conf_o_ref[k] = conf_ref[k + 1]
    conf_o_ref[CONF_K - 2] = glu

    zc = zc_ref[...]
    u = zc[:, GROUP_W:2 * GROUP_W] * zc[:, 2 * GROUP_W:3 * GROUP_W]
    cc = scw_ref[0:1, :] * sc_ref[0] + scw_ref[1:2, :] * sc_ref[1] + scw_ref[2:3, :] * u
    yc_ref[...] = zc[:, 0:GROUP_W] * cc
    sc_o_ref[0] = sc_ref[1]
    sc_o_ref[1] = u

    d = zd_ref[...]
    w2 = d + pool_ref[POOL_BUF - 1]
    w4 = w2 + pool_ref[POOL_BUF - 2] + pool_ref[POOL_BUF - 3]
    w8 = w4
    for j in range(4, 8):
        w8 = w8 + pool_ref[POOL_BUF - j]
    w16 = w8
    for j in range(8, 16):
        w16 = w16 + pool_ref[POOL_BUF - j]
    wsum, win = _pool_select(w2, w4, w8, w16, d.shape)
    pooled = wsum / jnp.minimum(win, float(start_pos + 1)) - d
    yd_ref[...] = jnp.dot(pooled.astype(BF16), pw_ref[...], preferred_element_type=F32) * ps_ref[...]
    for k in range(POOL_BUF - 1):
        pool_o_ref[k] = pool_ref[k + 1]
    pool_o_ref[POOL_BUF - 1] = d

    wr = (mu_ref[...], w0_ref[...], w2_ref[...], a0_ref[...], a2_ref[...], g2_ref[...],
          kk_ref[...], ka_ref[...], rk_ref[...], ones_ref[...])
    r, logdecay, k2, v, kk, a, g, bonus = _rwkv_prep(zb_ref[...], shift_ref[...], wr)
    r_o[...] = r
    w_o[...] = jnp.exp(logdecay)
    k_o[...] = k2
    v_o[...] = v
    kkn_o[...] = kk
    b_o[...] = kk * a
    g_o[...] = g
    bonus_o[...] = bonus


def _decode_acd_prep(za, zb, zc, zd, conf_t, shift, sc_t, pool_t, wl, start_pos):
    n = za.shape[0]
    ins = (za, zb, zc, zd, conf_t, shift, sc_t, pool_t, wl["conf_dw_w"], wl["conf_dw_b"], wl["conf_ln_g"],
           wl["conf_ln_b"], wl["sc_conv_w"], wl["pool_wbd"], wl["pool_scale"]) + _rwkv_weight_args(wl)[:9] + (wl["ones_bd"],)
    vec = jax.ShapeDtypeStruct((n, GROUP_W), F32)
    outs = [vec, vec, vec, jax.ShapeDtypeStruct(conf_t.shape, F32), jax.ShapeDtypeStruct(sc_t.shape, F32),
            jax.ShapeDtypeStruct(pool_t.shape, F32)] + [vec] * 8
    return pl.pallas_call(
        functools.partial(_decode_acd_prep_kernel, start_pos=start_pos),
        in_specs=[_full(x.shape) for x in ins],
        out_specs=[_full(o.shape) for o in outs],
        out_shape=outs,
        compiler_params=_cparams(),
        name="decode_acd_prep",
    )(*ins)


def _decode_wkv_kernel(s_ref, w_ref, kk_ref, b_ref, k_ref, r_ref, v_ref, s_o_ref, y_o_ref):
    s = s_ref[...]
    sa = -jnp.sum(s * kk_ref[...], axis=-1, keepdims=True)
    s_new = s * w_ref[...] + sa * b_ref[...] + v_ref[...] * k_ref[...]
    s_o_ref[...] = s_new
    y_o_ref[...] = jnp.sum(s_new * r_ref[...], axis=-1, keepdims=True)


def _decode_wkv(s, w, kk, bvec, k, r, v, blk):
    bh = s.shape[0]
    lane = pl.BlockSpec((blk, 1, RWKV_HEAD), lambda i: (i, 0, 0))
    col = pl.BlockSpec((blk, RWKV_HEAD, 1), lambda i: (i, 0, 0))
    mat = pl.BlockSpec((blk, RWKV_HEAD, RWKV_HEAD), lambda i: (i, 0, 0))
    return pl.pallas_call(
        _decode_wkv_kernel,
        grid=(bh // blk,),
        in_specs=[mat, lane, lane, lane, lane, lane, col],
        out_specs=[mat, col],
        out_shape=[jax.ShapeDtypeStruct(s.shape, F32), jax.ShapeDtypeStruct((bh, RWKV_HEAD, 1), F32)],
        compiler_params=_cparams("parallel"),
        name="decode_wkv",
    )(s, w, kk, bvec, k, r, v)


def _decode_post_kernel(y_ref, bonus_ref, g_ref, lng_ref, lnb_ref, ones_ref, o_ref):
    o_ref[...] = _rwkv_post(y_ref[...], bonus_ref[...], g_ref[...], lng_ref[...], lnb_ref[...], ones_ref[...])


def _decode_post(y, bonus, g, wl):
    ins = (y, bonus, g, wl["rwkv_ln_g"], wl["rwkv_ln_b"], wl["ones_bd"])
    return pl.pallas_call(
        _decode_post_kernel,
        in_specs=[_full(x.shape) for x in ins],
        out_specs=_full(y.shape),
        out_shape=jax.ShapeDtypeStruct(y.shape, F32),
        compiler_params=_cparams(),
        name="decode_post",
    )(*ins)


_CELLS = [(a, b) for a in range(PEER_TOPK) for b in range(PEER_TOPK) if (a + 1) * (b + 1) <= PEER_TOPK]
_CELL_PAIRS = [(c, d) for c in _CELLS for d in _CELLS if d[0] < c[0] and d[1] > c[1]]


def _top16_rows(s, tb, vals_ref, h, tie_safe):
    lanes = 128
    iota = lax.broadcasted_iota(jnp.int32, (N_KEYS, lanes), 0).astype(F32)
    ranks = []
    for c0 in range(0, tb, lanes):
        sc = s[:, c0:c0 + lanes]
        rank = jnp.full((N_KEYS, lanes), float(PEER_TOPK), F32)
        for r in range(PEER_TOPK):
            m = jnp.max(sc, axis=0, keepdims=True)
            sel = sc == m
            if tie_safe:
                sel = iota == jnp.min(jnp.where(sel, iota, float(N_KEYS)), axis=0, keepdims=True)
            rank = jnp.where(sel, float(r), rank)
            sc = jnp.where(sel, -jnp.inf, sc)
            vals_ref[r, h:h + 1, c0:c0 + lanes] = m
        ranks.append(rank)
    rank = jnp.concatenate(ranks, axis=1) if len(ranks) > 1 else ranks[0]
    n_sel = jnp.sum(jnp.where(rank < float(PEER_TOPK), 1.0, 0.0), axis=0, keepdims=True)
    return rank, n_sel


def _peer_route_kernel(h_ref, g_ref, wq_ref, k1_ref, k2_ref,
                       xn_ref, r2_ref, p2_ref, c1_ref, p1_ref,
                       v1_s, v2_s, r1_s, cnt_s, *, tb):
    refs = (h_ref, g_ref, wq_ref, k1_ref, k2_ref, xn_ref, r2_ref, p2_ref, c1_ref, p1_ref, v1_s, v2_s, r1_s, cnt_s)
    n_bad = _peer_route_body(*refs, tb=tb, tie_safe=False)

    @pl.when(jnp.max(n_bad) > 0.0)
    def _():
        _peer_route_body(*refs, tb=tb, tie_safe=True)


def _peer_route_body(h_ref, g_ref, wq_ref, k1_ref, k2_ref,
                     xn_ref, r2_ref, p2_ref, c1_ref, p1_ref,
                     v1_s, v2_s, r1_s, cnt_s, *, tb, tie_safe):
    xn = _rmsnorm(h_ref[...], g_ref[...])
    xb = xn.astype(BF16)
    xn_ref[...] = xn.T.astype(BF16)
    n_bad = jnp.zeros((1, tb), F32)
    for h in range(PEER_HEADS):
        q = jnp.dot(xb, wq_ref[:, h * 256:(h + 1) * 256], preferred_element_type=F32).astype(BF16)
        s1 = _dg(k1_ref[h], q[:, 0:128], _NT)
        s2 = _dg(k2_ref[h], q[:, 128:256], _NT)
        r1, n1 = _top16_rows(s1, tb, v1_s, h, tie_safe)
        r2, n2 = _top16_rows(s2, tb, v2_s, h, tie_safe)
        n_bad = n_bad + jnp.where(n1 != float(PEER_TOPK), 1.0, 0.0) + jnp.where(n2 != float(PEER_TOPK), 1.0, 0.0)
        r1_s[h] = r1
        r2_ref[h] = r2.astype(BF16)
        p1_ref[h] = jnp.exp(s1 - v1_s[0, h:h + 1, :])
        p2_ref[h] = jnp.exp(s2 - v2_s[0, h:h + 1, :]).astype(BF16)

    sums = {c: v1_s[c[0]] + v2_s[c[1]] for c in _CELLS}
    rank = {c: jnp.full((PEER_HEADS, tb), float((c[0] + 1) * (c[1] + 1) - 1), F32) for c in _CELLS}
    for c, d in _CELL_PAIRS:
        ge = sums[d] >= sums[c]
        rank[c] = rank[c] + jnp.where(ge, 1.0, 0.0)
        rank[d] = rank[d] + jnp.where(ge, 0.0, 1.0)
    e1 = [jnp.exp(v1_s[a] - v1_s[0]) for a in range(PEER_TOPK)]
    e2 = [jnp.exp(v2_s[b] - v2_s[0]) for b in range(PEER_TOPK)]
    z = jnp.zeros((PEER_HEADS, tb), F32)
    cnt = [jnp.zeros((PEER_HEADS, tb), F32) for _ in range(PEER_TOPK)]
    for c in _CELLS:
        sel = rank[c] < float(PEER_TOPK)
        cnt[c[0]] = cnt[c[0]] + jnp.where(sel, 1.0, 0.0)
        z = z + jnp.where(sel, e1[c[0]] * e2[c[1]], 0.0)
    for a in range(PEER_TOPK):
        cnt_s[a] = cnt[a]
    cnt_s[PEER_TOPK] = 0.5 / z

    for h in range(PEER_HEADS):
        r1 = r1_s[h]
        c1 = jnp.zeros((N_KEYS, tb), F32)
        for a in range(PEER_TOPK):
            c1 = jnp.where(r1 == float(a), cnt_s[a, h:h + 1, :], c1)
        c1_ref[h] = c1
        p1_ref[h] = p1_ref[h] * cnt_s[PEER_TOPK, h:h + 1, :]
    return n_bad


def _peer_route(hres, g, wq_bf16, k1_bf16, k2_bf16, tb):
    n, d = hres.shape
    tb = min(tb, n)
    gate = pl.BlockSpec((PEER_HEADS, N_KEYS, tb), lambda i: (0, 0, i))
    gshape = lambda dt: jax.ShapeDtypeStruct((PEER_HEADS, N_KEYS, n), dt)
    return pl.pallas_call(
        functools.partial(_peer_route_kernel, tb=tb),
        grid=(n // tb,),
        in_specs=[pl.BlockSpec((tb, d), lambda i: (i, 0)), _full((1, d)), _full(wq_bf16.shape),
                  _full(k1_bf16.shape), _full(k2_bf16.shape)],
        out_specs=[pl.BlockSpec((d, tb), lambda i: (0, i)), gate, gate, gate, gate],
        out_shape=[jax.ShapeDtypeStruct((d, n), BF16), gshape(BF16), gshape(BF16), gshape(F32), gshape(F32)],
        scratch_shapes=[pltpu.VMEM((PEER_TOPK, PEER_HEADS, tb), F32), pltpu.VMEM((PEER_TOPK, PEER_HEADS, tb), F32),
                        pltpu.VMEM((PEER_HEADS, N_KEYS, tb), F32), pltpu.VMEM((PEER_TOPK + 1, PEER_HEADS, tb), F32)],
        compiler_params=_cparams("parallel"),
        name="peer_route",
    )(hres, g.reshape(1, d), wq_bf16, k1_bf16, k2_bf16)


_SQRT_HALF = float(np.sqrt(0.5))


def _peer_expert_kernel(xn_ref, hres_ref, u_ref, vt_ref, r2_ref, p2_ref, c1_ref, p1_ref, o_ref,
                        acc_s, ht_s, at_s, *, eb):
    j = pl.program_id(1)
    per = eb // N_KEYS

    @pl.when(j == 0)
    def _():
        acc_s[...] = jnp.zeros_like(acc_s)

    tb = ht_s.shape[1]
    ht_s[...] = jnp.dot(u_ref[...], xn_ref[...], preferred_element_type=F32)
    for i in range(per):
        i1 = j * per + i

        def row_tile(ref, h):
            row = jnp.broadcast_to(ref[h, pl.ds(i1, 1), :], (16, tb)).astype(BF16)
            return jnp.concatenate([row] * (N_KEYS // 16), axis=0)

        gt = None
        for h in range(PEER_HEADS):
            c1 = row_tile(c1_ref, h)
            p1 = row_tile(p1_ref, h)
            term = jnp.where(r2_ref[h] < c1, p2_ref[h], jnp.zeros((), BF16)) * p1
            gt = term if gt is None else gt + term
        ht = ht_s[i * N_KEYS:(i + 1) * N_KEYS, :]
        act = ht * (1.0 + lax.erf(ht * _SQRT_HALF))
        at_s[i * N_KEYS:(i + 1) * N_KEYS, :] = act.astype(BF16) * gt
    acc_s[...] += jnp.dot(vt_ref[0], at_s[...], preferred_element_type=F32)

    @pl.when(j == pl.num_programs(1) - 1)
    def _():
        o_ref[...] = hres_ref[...] + acc_s[...].T


def _peer_experts(xn, hres, u_bf16, vt_bf16, gates, tb, eb):
    n, d = hres.shape
    tb = min(tb, n)
    nblk = u_bf16.shape[0] // eb
    gate = pl.BlockSpec((PEER_HEADS, N_KEYS, tb), lambda i, j: (0, 0, i))
    return pl.pallas_call(
        functools.partial(_peer_expert_kernel, eb=eb),
        grid=(n // tb, nblk),
        in_specs=[pl.BlockSpec((d, tb), lambda i, j: (0, i)), pl.BlockSpec((tb, d), lambda i, j: (i, 0)),
                  pl.BlockSpec((eb, d), lambda i, j: (j, 0)),
                  pl.BlockSpec((1, d, eb), lambda i, j: (j, 0, 0)),
                  gate, gate, gate, gate],
        out_specs=pl.BlockSpec((tb, d), lambda i, j: (i, 0)),
        out_shape=jax.ShapeDtypeStruct((n, d), F32),
        scratch_shapes=[pltpu.VMEM((d, tb), F32), pltpu.VMEM((eb, tb), F32), pltpu.VMEM((eb, tb), BF16)],
        compiler_params=_cparams("parallel", "arbitrary"),
        name="peer_experts",
    )(xn, hres, u_bf16, vt_bf16, *gates)


def _layer_weights(l, w):
    row = lambda a: a[l].reshape(1, -1)
    perm = _RWKV_PERM
    w_in = w["w_in"][l]
    w_in = jnp.concatenate([w_in[:, 0:512], w_in[:, 512:1408][:, perm], w_in[:, 1408:]], axis=1).astype(BF16)
    eye4 = jnp.eye(RWKV_H, dtype=F32)
    ones_bd = jnp.kron(eye4, jnp.ones((RWKV_HEAD, RWKV_HEAD), F32)).astype(BF16)
    pool_wbd = jax.scipy.linalg.block_diag(*[w["pool_w"][l, gi] for gi in range(4)]).astype(BF16)
    return dict(
        norm1_g=w["norm1_g"][l], norm2_g=w["norm2_g"][l], w_in=w_in,
        conf_dw_w=w["conf_dw_w"][l], conf_dw_b=row(w["conf_dw_b"]), conf_ln_g=row(w["conf_ln_g"]),
        conf_ln_b=row(w["conf_ln_b"]), sc_conv_w=w["sc_conv_w"][l], pool_wbd=pool_wbd, pool_scale=row(w["pool_scale"]),
        rwkv_mu=w["rwkv_mu"][l][perm].reshape(1, -1), rwkv_w0=row(w["rwkv_w0"]), rwkv_w2=w["rwkv_w2"][l].astype(BF16),
        rwkv_a0=row(w["rwkv_a0"]), rwkv_a2=w["rwkv_a2"][l].astype(BF16), rwkv_g2=w["rwkv_g2"][l].astype(BF16),
        rwkv_k_k=row(w["rwkv_k_k"]), rwkv_k_a=row(w["rwkv_k_a"]), rwkv_r_k=row(w["rwkv_r_k"]),
        rwkv_ln_g=row(w["rwkv_ln_g"]), rwkv_ln_b=row(w["rwkv_ln_b"]), ones_bd=ones_bd,
        w_out=w["w_out"][l].astype(BF16), peer_wq=w["peer_wq"][l].astype(BF16),
        peer_k1=w["peer_k1"][l].astype(BF16), peer_k2=w["peer_k2"][l].astype(BF16),
        peer_u=w["peer_u"][l].astype(BF16),
        peer_vt=jnp.transpose(w["peer_v"][l].astype(BF16).reshape(-1, EXPERT_BLOCK, D_MODEL), (0, 2, 1)),
    )


def _peer_block(hres, wl, tb_route, tb_exp, eb):
    xn, r2, p2, c1, p1 = _peer_route(hres, wl["norm2_g"], wl["peer_wq"], wl["peer_k1"], wl["peer_k2"], tb_route)
    return _peer_experts(xn, hres, wl["peer_u"], wl["peer_vt"], (r2, p2, c1, p1), tb_exp, eb)


def _prompt_layer(x, wl, bsz, t):
    n = bsz * t
    za, zb, zc, zd = _norm_proj(x, wl["norm1_g"], wl["w_in"], IN_SPLITS, 512)
    z3 = lambda a: a.reshape(bsz, t, -1)
    zeros = lambda *s: jnp.zeros(s, F32)
    ya, yc, yd, conf, sc, pool = _prompt_acd(z3(za), z3(zc), z3(zd), zeros(bsz, 30, GROUP_W), zeros(bsz, 2, GROUP_W),
                                             zeros(bsz, 15, GROUP_W), wl, 256, 0)
    yb, shift, wkv = _prompt_rwkv(z3(zb), zeros(bsz, 1, RWKV_COLS), zeros(bsz, RWKV_H, RWKV_HEAD, RWKV_HEAD), wl,
                                   WKV_SEQS_PER_STEP)
    flat = lambda a: a.reshape(n, GROUP_W)
    hres = _out_proj(x, (flat(ya), flat(yb), flat(yc), flat(yd)), wl["w_out"], 512)
    x = _peer_block(hres, wl, 256, 512, EXPERT_BLOCK)
    return x, (conf, shift.reshape(bsz, RWKV_COLS)[:, _RWKV_INV_PERM], wkv, sc, pool)


def _decode_layer(x, states, wl, start_pos):
    conf, shift, wkv, sc, pool = states
    n = x.shape[0]
    za, zb, zc, zd = _norm_proj(x, wl["norm1_g"], wl["w_in"], IN_SPLITS, 128)
    tr = lambda a: jnp.transpose(a, (1, 0, 2))
    (ya, yc, yd, conf_n, sc_n, pool_n, r, w, k2, v, kk, bvec, g, bonus) = _decode_acd_prep(
        za, zb, zc, zd, tr(conf), shift[:, _RWKV_PERM], tr(sc), tr(pool), wl, start_pos)
    bh = n * RWKV_H
    lane = lambda a: a.reshape(bh, 1, RWKV_HEAD)
    s_new, y = _decode_wkv(wkv.reshape(bh, RWKV_HEAD, RWKV_HEAD), lane(w), lane(kk), lane(bvec), lane(k2), lane(r),
                           v.reshape(bh, RWKV_HEAD, 1), 64)
    yb = _decode_post(y.reshape(n, GROUP_W), bonus, g, wl)
    hres = _out_proj(x, (ya, yb, yc, yd), wl["w_out"], 128)
    x = _peer_block(hres, wl, 128, 128, EXPERT_BLOCK)
    return x, (tr(conf_n), zb[:, _RWKV_INV_PERM], s_new.reshape(n, RWKV_H, RWKV_HEAD, RWKV_HEAD), tr(sc_n), tr(pool_n))


def kernel(x_prompt, x_sample, state_conformer, state_rwkv_shift, state_rwkv_wkv, state_shortconv, state_pool, norm1_g, norm2_g, final_norm_g, w_in, conf_dw_w, conf_dw_b, conf_ln_g, conf_ln_b, rwkv_mu, rwkv_w0, rwkv_w2, rwkv_a0, rwkv_a2, rwkv_g2, rwkv_k_k, rwkv_k_a, rwkv_r_k, rwkv_ln_g, rwkv_ln_b, sc_conv_w, pool_w, pool_scale, w_out, peer_wq, peer_k1, peer_k2, peer_u, peer_v):
    w = dict(norm1_g=norm1_g, norm2_g=norm2_g, w_in=w_in, conf_dw_w=conf_dw_w, conf_dw_b=conf_dw_b,
             conf_ln_g=conf_ln_g, conf_ln_b=conf_ln_b, rwkv_mu=rwkv_mu, rwkv_w0=rwkv_w0, rwkv_w2=rwkv_w2,
             rwkv_a0=rwkv_a0, rwkv_a2=rwkv_a2, rwkv_g2=rwkv_g2, rwkv_k_k=rwkv_k_k, rwkv_k_a=rwkv_k_a,
             rwkv_r_k=rwkv_r_k, rwkv_ln_g=rwkv_ln_g, rwkv_ln_b=rwkv_ln_b, sc_conv_w=sc_conv_w, pool_w=pool_w,
             pool_scale=pool_scale, w_out=w_out, peer_wq=peer_wq, peer_k1=peer_k1, peer_k2=peer_k2,
             peer_u=peer_u, peer_v=peer_v)
    depth = w_in.shape[0]
    bsz, t, d = x_prompt.shape
    nb, dt, _ = x_sample.shape
    past_len = 16384

    xp = x_prompt.reshape(bsz * t, d)
    xs = x_sample.reshape(nb * dt, d)
    p_states, s_states = [], []
    for l in range(depth):
        wl = _layer_weights(l, w)
        xp, ps = _prompt_layer(xp, wl, bsz, t)
        xs, ss = _decode_layer(xs, (state_conformer[l], state_rwkv_shift[l], state_rwkv_wkv[l],
                                    state_shortconv[l], state_pool[l]), wl, past_len)
        p_states.append(ps)
        s_states.append(ss)
    y_prompt = _final_norm(xp, final_norm_g, 512).reshape(bsz, t, d)
    y_sample = _final_norm(xs, final_norm_g, 128).reshape(nb, dt, d)
    stack = lambda lst, i: jnp.stack([s[i] for s in lst], axis=0)
    conf_p, shift_p, wkv_p, sc_p, pool_p = (stack(p_states, i) for i in range(5))
    conf_s, shift_s, wkv_s, sc_s, pool_s = (stack(s_states, i) for i in range(5))
    return (y_prompt, y_sample, conf_p, conf_s, shift_p, shift_s, wkv_p, wkv_s, sc_p, sc_s, pool_p, pool_s)
```

```python
import functools

import jax
import jax.numpy as jnp
import numpy as np
from jax import lax
from jax.experimental import pallas as pl
from jax.experimental.pallas import tpu as pltpu

F32 = jnp.float32
BF16 = jnp.bfloat16

D_MODEL = 1024
GROUP_W = 256
CONF_K = 31
RWKV_HEAD = 64
RWKV_H = 4
RWKV_COLS = 896
RWKV_GN_EPS = 64e-5
POOL_WINDOWS = (2, 4, 8, 16)
POOL_BUF = 15
N_KEYS = 128
PEER_HEADS = 8
PEER_TOPK = 16
RMS_EPS = 1e-6
LN_EPS = 1e-5
IN_SPLITS = (512, 896, 768, 256)

VMEM_LIMIT_BYTES = 56 * 1024 * 1024
WKV_CHUNK = 64
EXPERT_BLOCK = 2048
WKV_SEQS_PER_STEP = 8
WKV_PASSES = (1, 1, 1, 1)

_RWKV_PERM = np.concatenate([np.arange(0, 256), np.arange(288, 544), np.arange(544, 800),
                             np.arange(256, 288), np.arange(800, 832), np.arange(832, 896)])
_RWKV_INV_PERM = np.argsort(_RWKV_PERM)


def _cparams(*sem):
    return pltpu.CompilerParams(dimension_semantics=tuple(sem) if sem else None,
                                vmem_limit_bytes=VMEM_LIMIT_BYTES)


def _full(shape):
    n = len(shape)
    return pl.BlockSpec(shape, lambda *_: (0,) * n)


def _split2(x):
    hi = x.astype(BF16)
    lo = (x - hi.astype(F32)).astype(BF16)
    return hi, lo


def _split3(x):
    hi = x.astype(BF16)
    r = x - hi.astype(F32)
    mid = r.astype(BF16)
    lo = (r - mid.astype(F32)).astype(BF16)
    return hi, mid, lo


_NN = (((1,), (0,)), ((), ()))
_NT = (((1,), (1,)), ((), ()))


def _dg(a, b, dims):
    return lax.dot_general(a, b, dims, preferred_element_type=F32)


def _mm3(a, b, dims=_NN):
    ah, al = _split2(a)
    bh, bl = _split2(b)
    return _dg(ah, bh, dims) + (_dg(al, bh, dims) + _dg(ah, bl, dims))


def _mm(a, b, dims=_NN, passes=3):
    if passes == 1:
        return _dg(a.astype(BF16), b.astype(BF16), dims)
    return _mm3(a, b, dims)


def _mm_exact_rhs(a, b_bf16):
    h, m, l = _split3(a)
    return _dg(h, b_bf16, _NN) + (_dg(m, b_bf16, _NN) + _dg(l, b_bf16, _NN))


def _mm_exact_lhs(a_bf16, b):
    h, m, l = _split3(b)
    return _dg(a_bf16, h, _NN) + (_dg(a_bf16, m, _NN) + _dg(a_bf16, l, _NN))


def _transpose_mxu(x, eye_bf16):
    h, m, l = _split3(x)
    return _dg(eye_bf16, h, _NT) + (_dg(eye_bf16, m, _NT) + _dg(eye_bf16, l, _NT))


def _eye(n, dtype):
    return (lax.broadcasted_iota(jnp.int32, (n, n), 0) == lax.broadcasted_iota(jnp.int32, (n, n), 1)).astype(dtype)


def _rmsnorm(x, g):
    ms = jnp.mean(x * x, axis=-1, keepdims=True)
    return x * lax.rsqrt(ms + RMS_EPS) * g


def _sigmoid(x):
    return 1.0 / (1.0 + jnp.exp(-x))


def _softplus(x):
    return jnp.maximum(x, 0.0) + jnp.log(1.0 + jnp.exp(-jnp.abs(x)))


def _norm_proj_kernel(x_ref, g_ref, w_ref, *o_refs, splits):
    xb = _rmsnorm(x_ref[...], g_ref[...]).astype(BF16)
    off = 0
    for o_ref, wd in zip(o_refs, splits):
        o_ref[...] = jnp.dot(xb, w_ref[:, off:off + wd], preferred_element_type=F32)
        off += wd


def _norm_proj(x, g, w_bf16, splits, tm):
    n, d = x.shape
    tm = min(tm, n)
    cols = w_bf16.shape[1]
    return pl.pallas_call(
        functools.partial(_norm_proj_kernel, splits=splits),
        grid=(n // tm,),
        in_specs=[pl.BlockSpec((tm, d), lambda i: (i, 0)), _full((1, d)), _full((d, cols))],
        out_specs=[pl.BlockSpec((tm, wd), lambda i: (i, 0)) for wd in splits],
        out_shape=[jax.ShapeDtypeStruct((n, wd), F32) for wd in splits],
        compiler_params=_cparams("parallel"),
        name="norm_proj",
    )(x, g.reshape(1, d), w_bf16)


def _out_proj_kernel(x_ref, ya_ref, yb_ref, yc_ref, yd_ref, w_ref, o_ref):
    acc = x_ref[...]
    for i, y_ref in enumerate((ya_ref, yb_ref, yc_ref, yd_ref)):
        acc = acc + jnp.dot(y_ref[...].astype(BF16), w_ref[i * GROUP_W:(i + 1) * GROUP_W, :],
                            preferred_element_type=F32)
    o_ref[...] = acc


def _out_proj(x, ys, w_bf16, tm):
    n, d = x.shape
    tm = min(tm, n)
    row = lambda i: (i, 0)
    return pl.pallas_call(
        _out_proj_kernel,
        grid=(n // tm,),
        in_specs=[pl.BlockSpec((tm, d), row)] + [pl.BlockSpec((tm, GROUP_W), row)] * 4 + [_full((d, d))],
        out_specs=pl.BlockSpec((tm, d), row),
        out_shape=jax.ShapeDtypeStruct((n, d), F32),
        compiler_params=_cparams("parallel"),
        name="out_proj",
    )(x, *ys, w_bf16)


def _final_norm_kernel(x_ref, g_ref, o_ref):
    o_ref[...] = _rmsnorm(x_ref[...], g_ref[...])


def _final_norm(x, g, tm):
    n, d = x.shape
    tm = min(tm, n)
    return pl.pallas_call(
        _final_norm_kernel,
        grid=(n // tm,),
        in_specs=[pl.BlockSpec((tm, d), lambda i: (i, 0)), _full((1, d))],
        out_specs=pl.BlockSpec((tm, d), lambda i: (i, 0)),
        out_shape=jax.ShapeDtypeStruct((n, d), F32),
        compiler_params=_cparams("parallel"),
        name="final_norm",
    )(x, g.reshape(1, d))


def _layernorm_lanes(x, g, b, eps):
    mu = jnp.mean(x, axis=-1, keepdims=True)
    xc = x - mu
    var = jnp.mean(xc * xc, axis=-1, keepdims=True)
    return xc * lax.rsqrt(var + eps) * g + b


def _conformer_tail(ca, lng, lnb):
    y = _layernorm_lanes(ca, lng, lnb, LN_EPS)
    return y * _sigmoid(y)


def _pool_select(w2, w4, w8, w16, shape):
    lane = lax.broadcasted_iota(jnp.int32, shape, len(shape) - 1)
    wsum = jnp.where(lane < 64, w2, jnp.where(lane < 128, w4, jnp.where(lane < 192, w8, w16)))
    win = jnp.where(lane < 64, 2.0, jnp.where(lane < 128, 4.0, jnp.where(lane < 192, 8.0, 16.0)))
    return wsum, win


def _head_sum(x, ones_bd):
    return _mm_exact_rhs(x, ones_bd)


def _rwkv_prep(p, prev, wr):
    (mu, w0, w2, a0, a2, g2, k_k, k_a, r_k, ones_bd) = wr
    xs = p + (prev - p) * mu
    r = xs[:, 0:256]
    k = xs[:, 256:512]
    v = xs[:, 512:768]
    w_lo = xs[:, 768:800]
    a_lo = xs[:, 800:832]
    g_lo = xs[:, 832:896]
    wexp = -_softplus(-(w0 + jnp.dot(jnp.tanh(w_lo).astype(BF16), w2, preferred_element_type=F32))) - 0.5
    logdecay = -jnp.exp(wexp)
    a = _sigmoid(a0 + jnp.dot(a_lo.astype(BF16), a2, preferred_element_type=F32))
    g = jnp.dot(_sigmoid(g_lo).astype(BF16), g2, preferred_element_type=F32)
    kk = k * k_k
    kk = kk * lax.rsqrt(jnp.maximum(_head_sum(kk * kk, ones_bd), 1e-24))
    k2 = k * (1.0 + (a - 1.0) * k_a)
    bonus = _head_sum(r * k2 * r_k, ones_bd) * v
    return r, logdecay, k2, v, kk, a, g, bonus


def _rwkv_post(y, bonus, g, lng, lnb, ones_bd):
    mu = _head_sum(y, ones_bd) * (1.0 / RWKV_HEAD)
    yc = y - mu
    var = _head_sum(yc * yc, ones_bd) * (1.0 / RWKV_HEAD)
    yn = yc * lax.rsqrt(var + RWKV_GN_EPS) * lng + lnb
    return (yn + bonus) * g


_CONV_ROWS = 64


def _prompt_acd_kernel(za_ref, zc_ref, zd_ref, conf0_ref, sc0_ref, pool0_ref,
                       cw_ref, cb_ref, clg_ref, clb_ref, scw_ref, pw_ref, ps_ref,
                       ya_ref, yc_ref, yd_ref, conf_o_ref, sc_o_ref, pool_o_ref,
                       ext_a, ext_c, ext_d, *, tt, start_pos):
    t = pl.program_id(1)

    @pl.when(t == 0)
    def _():
        ext_a[0:2, :] = jnp.zeros((2, GROUP_W), F32)
        ext_a[2:32, :] = conf0_ref[0]
        ext_c[0:6, :] = jnp.zeros((6, GROUP_W), F32)
        ext_c[6:8, :] = sc0_ref[0]
        ext_d[0:1, :] = jnp.zeros((1, GROUP_W), F32)
        ext_d[1:16, :] = pool0_ref[0]

    za = za_ref[0]
    ext_a[32:32 + tt, :] = za[:, 0:GROUP_W] * _sigmoid(za[:, GROUP_W:2 * GROUP_W])
    for c in range(tt // _CONV_ROWS):
        base = c * _CONV_ROWS
        acc = jnp.zeros((_CONV_ROWS, GROUP_W), F32) + cb_ref[...]
        for k in range(CONF_K):
            acc = acc + cw_ref[k:k + 1, :] * ext_a[base + k + 2:base + k + 2 + _CONV_ROWS, :]
        ya_ref[0, base:base + _CONV_ROWS, :] = _conformer_tail(acc, clg_ref[...], clb_ref[...])
    conf_o_ref[0] = ext_a[tt + 2:tt + 32, :]
    ext_a[0:32, :] = ext_a[tt:tt + 32, :]

    zc = zc_ref[0]
    ext_c[8:8 + tt, :] = zc[:, GROUP_W:2 * GROUP_W] * zc[:, 2 * GROUP_W:3 * GROUP_W]
    cc = (scw_ref[0:1, :] * ext_c[6:6 + tt, :] + scw_ref[1:2, :] * ext_c[7:7 + tt, :]
          + scw_ref[2:3, :] * ext_c[8:8 + tt, :])
    yc_ref[0] = zc[:, 0:GROUP_W] * cc
    sc_o_ref[0] = ext_c[tt + 6:tt + 8, :]
    ext_c[0:8, :] = ext_c[tt:tt + 8, :]

    u = zd_ref[0]
    ext_d[16:16 + tt, :] = u
    w2 = u + ext_d[15:15 + tt, :]
    w4 = w2 + ext_d[14:14 + tt, :] + ext_d[13:13 + tt, :]
    w8 = w4
    for j in range(4, 8):
        w8 = w8 + ext_d[16 - j:16 - j + tt, :]
    w16 = w8
    for j in range(8, 16):
        w16 = w16 + ext_d[16 - j:16 - j + tt, :]
    wsum, win = _pool_select(w2, w4, w8, w16, (tt, GROUP_W))
    pos = (lax.broadcasted_iota(jnp.int32, (tt, GROUP_W), 0) + (t * tt + start_pos + 1)).astype(F32)
    pooled = wsum / jnp.minimum(win, pos) - u
    yd_ref[0] = jnp.dot(pooled.astype(BF16), pw_ref[...], preferred_element_type=F32) * ps_ref[...]
    pool_o_ref[0] = ext_d[tt + 1:tt + 16, :]
    ext_d[0:16, :] = ext_d[tt:tt + 16, :]


def _prompt_acd(za, zc, zd, conf0, sc0, pool0, wl, tt, start_pos):
    b, t, _ = za.shape
    tile = lambda w: pl.BlockSpec((1, tt, w), lambda i, j: (i, j, 0))
    st = lambda r: pl.BlockSpec((1, r, GROUP_W), lambda i, j: (i, 0, 0))
    row = _full((1, GROUP_W))
    return pl.pallas_call(
        functools.partial(_prompt_acd_kernel, tt=tt, start_pos=start_pos),
        grid=(b, t // tt),
        in_specs=[tile(512), tile(768), tile(256), st(30), st(2), st(15),
                  _full((CONF_K, GROUP_W)), row, row, row, _full((3, GROUP_W)), _full((GROUP_W, GROUP_W)), row],
        out_specs=[tile(256), tile(256), tile(256), st(30), st(2), st(15)],
        out_shape=[jax.ShapeDtypeStruct((b, t, GROUP_W), F32)] * 3
        + [jax.ShapeDtypeStruct((b, r, GROUP_W), F32) for r in (30, 2, 15)],
        scratch_shapes=[pltpu.VMEM((32 + tt, GROUP_W), F32), pltpu.VMEM((8 + tt, GROUP_W), F32),
                        pltpu.VMEM((16 + tt, GROUP_W), F32)],
        compiler_params=_cparams("parallel", "arbitrary"),
        name="prompt_acd",
    )(za, zc, zd, conf0, sc0, pool0, wl["conf_dw_w"], wl["conf_dw_b"], wl["conf_ln_g"], wl["conf_ln_b"],
      wl["sc_conv_w"], wl["pool_wbd"], wl["pool_scale"])


def _prompt_rwkv_kernel(zb_ref, shift0_ref, wkv0_ref, mu_ref, w0_ref, w2_ref, a0_ref, a2_ref, g2_ref,
                        kk_ref, ka_ref, rk_ref, lng_ref, lnb_ref, ones_ref, tri_ref,
                        yb_ref, shift_o_ref, wkv_o_ref, prev_s, st_s, *, c, nb):
    t = pl.program_id(1)
    nt = pl.num_programs(1)

    @pl.when(t == 0)
    def _():
        for bb in range(nb):
            prev_s[bb] = jnp.broadcast_to(shift0_ref[bb], prev_s.shape[1:])
            for h in range(RWKV_H):
                st_s[bb, h] = _transpose_mxu(wkv0_ref[bb, h], _eye(RWKV_HEAD, BF16))

    new_states = _rwkv_chunks(zb_ref, mu_ref, w0_ref, w2_ref, a0_ref, a2_ref, g2_ref, kk_ref, ka_ref, rk_ref,
                              lng_ref, lnb_ref, ones_ref, tri_ref, yb_ref, shift_o_ref, prev_s, st_s, c, nb)

    @pl.when(t == nt - 1)
    def _():
        eye_h = _eye(RWKV_HEAD, BF16)
        for bb in range(nb):
            for h in range(RWKV_H):
                wkv_o_ref[bb, h] = _transpose_mxu(new_states[(bb, h)], eye_h)


def _rwkv_chunks(zb_ref, mu_ref, w0_ref, w2_ref, a0_ref, a2_ref, g2_ref, kk_ref, ka_ref, rk_ref,
                 lng_ref, lnb_ref, ones_ref, tri_ref, yb_ref, shift_o_ref, prev_s, st_s, c, nb):
    p = zb_ref[...].reshape(nb * c, RWKV_COLS)
    row = lax.broadcasted_iota(jnp.int32, p.shape, 0)
    prev = pltpu.roll(p, 1, axis=0)
    for bb in range(nb):
        prev = jnp.where(row == bb * c, prev_s[bb, 0:1, :], prev)
    ones_bd = ones_ref[...]
    wr = (mu_ref[...], w0_ref[...], w2_ref[...], a0_ref[...], a2_ref[...], g2_ref[...],
          kk_ref[...], ka_ref[...], rk_ref[...], ones_bd)
    r, logdecay, k2, v, kk, a, g, bonus = _rwkv_prep(p, prev, wr)

    cum = _mm_exact_lhs(tri_ref[...], logdecay)
    g_end = [cum[(bb + 1) * c - 1:(bb + 1) * c, :] for bb in range(nb)]
    cum_end = jnp.concatenate([jnp.broadcast_to(ge, (c, GROUP_W)) for ge in g_end], axis=0)
    g_end = [jnp.exp(ge) for ge in g_end]
    e_neg = jnp.exp(-cum)
    e_end = jnp.exp(cum_end - cum)
    bvec = kk * a
    a_t = -kk * jnp.exp(cum - logdecay)
    r_t = r * jnp.exp(cum)
    b_h = bvec * e_neg
    k_h = k2 * e_neg
    b_e = bvec * e_end
    k_e = k2 * e_end

    ri = lax.broadcasted_iota(jnp.int32, (2 * c, 2 * c), 0)
    ci = lax.broadcasted_iota(jnp.int32, (2 * c, 2 * c), 1)
    rt, cs = ri & (c - 1), ci & (c - 1)
    keep = (rt > cs) | ((ri >= c) & (rt == cs))
    eye_2h = _eye(2 * RWKV_HEAD, BF16)
    eye_f = _eye(RWKV_HEAD, F32)

    p_nt, p_rhs, p_neu, p_out = WKV_PASSES
    chains = [(bb, h) for bb in range(nb) for h in range(RWKV_H)]

    def part(z, bb, h):
        return z[bb * c:(bb + 1) * c, h * RWKV_HEAD:(h + 1) * RWKV_HEAD]

    big = {ch: jnp.where(keep, _mm(jnp.concatenate([part(a_t, *ch), part(r_t, *ch)], axis=0),
                                   jnp.concatenate([part(b_h, *ch), part(k_h, *ch)], axis=0), _NT, p_nt), 0.0)
           for ch in chains}
    s0 = {ch: st_s[ch[0], ch[1]] for ch in chains}
    vh = {ch: part(v, *ch) for ch in chains}
    sa = {ch: _mm(jnp.concatenate([part(a_t, *ch), big[ch][0:c, c:2 * c]], axis=1),
                  jnp.concatenate([s0[ch], vh[ch]], axis=0), _NN, p_rhs) for ch in chains}
    x = {ch: big[ch][0:c, 0:c] for ch in chains}
    n_sq = int(np.log2(c))
    for step in range(n_sq):
        sa = {ch: sa[ch] + _mm(x[ch], sa[ch], _NN, p_neu) for ch in chains}
        if step + 1 < n_sq:
            x = {ch: _mm(x[ch], x[ch], _NN, p_neu) for ch in chains}
    bk = {ch: jnp.concatenate([part(b_e, *ch), part(k_e, *ch)], axis=1) for ch in chains}
    bk_t = {ch: _dg(eye_2h, bk[ch].astype(BF16), _NT) if p_out == 1 else _transpose_mxu(bk[ch], eye_2h)
            for ch in chains}
    y_h = {ch: _mm(jnp.concatenate([part(r_t, *ch), big[ch][c:2 * c, 0:c], big[ch][c:2 * c, c:2 * c]], axis=1),
                   jnp.concatenate([s0[ch], sa[ch], vh[ch]], axis=0), _NN, p_out) for ch in chains}
    s_new = {}
    for ch in chains:
        bb, h = ch
        g_h = g_end[bb][:, h * RWKV_HEAD:(h + 1) * RWKV_HEAD]
        s_new[ch] = _mm3(eye_f * g_h, s0[ch]) + _mm(
            jnp.concatenate([bk_t[ch][0:RWKV_HEAD], bk_t[ch][RWKV_HEAD:]], axis=1),
            jnp.concatenate([sa[ch], vh[ch]], axis=0), _NN, p_out)
        st_s[bb, h] = s_new[ch]

    y = jnp.concatenate([jnp.concatenate([y_h[(bb, h)] for h in range(RWKV_H)], axis=-1) for bb in range(nb)], axis=0)
    yb = _rwkv_post(y, bonus, g, lng_ref[...], lnb_ref[...], ones_bd)
    for bb in range(nb):
        yb_ref[bb] = yb[bb * c:(bb + 1) * c]
        last = p[(bb + 1) * c - 1:(bb + 1) * c, :]
        prev_s[bb] = jnp.broadcast_to(last, prev_s.shape[1:])
        shift_o_ref[bb] = last
    return s_new


def _rwkv_weight_args(wl):
    return (wl["rwkv_mu"], wl["rwkv_w0"], wl["rwkv_w2"], wl["rwkv_a0"], wl["rwkv_a2"], wl["rwkv_g2"],
            wl["rwkv_k_k"], wl["rwkv_k_a"], wl["rwkv_r_k"], wl["rwkv_ln_g"], wl["rwkv_ln_b"], wl["ones_bd"])


_RWKV_WEIGHT_SPECS = [(1, RWKV_COLS), (1, GROUP_W), (32, GROUP_W), (1, GROUP_W), (32, GROUP_W), (64, GROUP_W),
                      (1, GROUP_W), (1, GROUP_W), (1, GROUP_W), (1, GROUP_W), (1, GROUP_W), (GROUP_W, GROUP_W)]


def _prompt_rwkv(zb, shift0, wkv0, wl, nb):
    b, t, _ = zb.shape
    c = WKV_CHUNK
    nb = min(nb, b)
    tri = jnp.kron(jnp.eye(nb, dtype=F32), jnp.tril(jnp.ones((c, c), F32))).astype(BF16)
    return pl.pallas_call(
        functools.partial(_prompt_rwkv_kernel, c=c, nb=nb),
        grid=(b // nb, t // c),
        in_specs=[pl.BlockSpec((nb, c, RWKV_COLS), lambda i, j: (i, j, 0)),
                  pl.BlockSpec((nb, 1, RWKV_COLS), lambda i, j: (i, 0, 0)),
                  pl.BlockSpec((nb, RWKV_H, RWKV_HEAD, RWKV_HEAD), lambda i, j: (i, 0, 0, 0))]
        + [_full(s) for s in _RWKV_WEIGHT_SPECS] + [_full((nb * c, nb * c))],
        out_specs=[pl.BlockSpec((nb, c, GROUP_W), lambda i, j: (i, j, 0)),
                   pl.BlockSpec((nb, 1, RWKV_COLS), lambda i, j: (i, 0, 0)),
                   pl.BlockSpec((nb, RWKV_H, RWKV_HEAD, RWKV_HEAD), lambda i, j: (i, 0, 0, 0))],
        out_shape=[jax.ShapeDtypeStruct((b, t, GROUP_W), F32), jax.ShapeDtypeStruct((b, 1, RWKV_COLS), F32),
                   jax.ShapeDtypeStruct((b, RWKV_H, RWKV_HEAD, RWKV_HEAD), F32)],
        scratch_shapes=[pltpu.VMEM((nb, 8, RWKV_COLS), F32), pltpu.VMEM((nb, RWKV_H, RWKV_HEAD, RWKV_HEAD), F32)],
        compiler_params=_cparams("parallel", "arbitrary"),
        name="prompt_rwkv",
    )(zb, shift0, wkv0, *_rwkv_weight_args(wl), tri)


def _decode_acd_prep_kernel(za_ref, zb_ref, zc_ref, zd_ref, conf_ref, shift_ref, sc_ref, pool_ref,
                            cw_ref, cb_ref, clg_ref, clb_ref, scw_ref, pw_ref, ps_ref,
                            mu_ref, w0_ref, w2_ref, a0_ref, a2_ref, g2_ref, kk_ref, ka_ref, rk_ref, ones_ref,
                            ya_ref, yc_ref, yd_ref, conf_o_ref, sc_o_ref, pool_o_ref,
                            r_o, w_o, k_o, v_o, kkn_o, b_o, g_o, bonus_o, *, start_pos):
    za = za_ref[...]
    glu = za[:, 0:GROUP_W] * _sigmoid(za[:, GROUP_W:2 * GROUP_W])
    acc = cb_ref[...] + cw_ref[CONF_K - 1:CONF_K, :] * glu
    for k in range(CONF_K - 1):
        acc = acc + cw_ref[k:k + 1, :] * conf_ref[k]
    ya_ref[...] = _conformer_tail(acc, clg_ref[...], clb_ref[...])
    for k in range(CONF_K - 2):
        conf_o_ref[k] = conf_ref[k + 1]
    conf_o_ref[CONF_K - 2] = glu

    zc = zc_ref[...]
    u = zc[:, GROUP_W:2 * GROUP_W] * zc[:, 2 * GROUP_W:3 * GROUP_W]
    cc = scw_ref[0:1, :] * sc_ref[0] + scw_ref[1:2, :] * sc_ref[1] + scw_ref[2:3, :] * u
    yc_ref[...] = zc[:, 0:GROUP_W] * cc
    sc_o_ref[0] = sc_ref[1]
    sc_o_ref[1] = u

    d = zd_ref[...]
    w2 = d + pool_ref[POOL_BUF - 1]
    w4 = w2 + pool_ref[POOL_BUF - 2] + pool_ref[POOL_BUF - 3]
    w8 = w4
    for j in range(4, 8):
        w8 = w8 + pool_ref[POOL_BUF - j]
    w16 = w8
    for j in range(8, 16):
        w16 = w16 + pool_ref[POOL_BUF - j]
    wsum, win = _pool_select(w2, w4, w8, w16, d.shape)
    pooled = wsum / jnp.minimum(win, float(start_pos + 1)) - d
    yd_ref[...] = jnp.dot(pooled.astype(BF16), pw_ref[...], preferred_element_type=F32) * ps_ref[...]
    for k in range(POOL_BUF - 1):
        pool_o_ref[k] = pool_ref[k + 1]
    pool_o_ref[POOL_BUF - 1] = d

    wr = (mu_ref[...], w0_ref[...], w2_ref[...], a0_ref[...], a2_ref[...], g2_ref[...],
          kk_ref[...], ka_ref[...], rk_ref[...], ones_ref[...])
    r, logdecay, k2, v, kk, a, g, bonus = _rwkv_prep(zb_ref[...], shift_ref[...], wr)
    r_o[...] = r
    w_o[...] = jnp.exp(logdecay)
    k_o[...] = k2
    v_o[...] = v
    kkn_o[...] = kk
    b_o[...] = kk * a
    g_o[...] = g
    bonus_o[...] = bonus


def _decode_acd_prep(za, zb, zc, zd, conf_t, shift, sc_t, pool_t, wl, start_pos):
    n = za.shape[0]
    ins = (za, zb, zc, zd, conf_t, shift, sc_t, pool_t, wl["conf_dw_w"], wl["conf_dw_b"], wl["conf_ln_g"],
           wl["conf_ln_b"], wl["sc_conv_w"], wl["pool_wbd"], wl["pool_scale"]) + _rwkv_weight_args(wl)[:9] + (wl["ones_bd"],)
    vec = jax.ShapeDtypeStruct((n, GROUP_W), F32)
    outs = [vec, vec, vec, jax.ShapeDtypeStruct(conf_t.shape, F32), jax.ShapeDtypeStruct(sc_t.shape, F32),
            jax.ShapeDtypeStruct(pool_t.shape, F32)] + [vec] * 8
    return pl.pallas_call(
        functools.partial(_decode_acd_prep_kernel, start_pos=start_pos),
        in_specs=[_full(x.shape) for x in ins],
        out_specs=[_full(o.shape) for o in outs],
        out_shape=outs,
        compiler_params=_cparams(),
        name="decode_acd_prep",
    )(*ins)


def _decode_wkv_kernel(s_ref, w_ref, kk_ref, b_ref, k_ref, r_ref, v_ref, s_o_ref, y_o_ref):
    s = s_ref[...]
    sa = -jnp.sum(s * kk_ref[...], axis=-1, keepdims=True)
    s_new = s * w_ref[...] + sa * b_ref[...] + v_ref[...] * k_ref[...]
    s_o_ref[...] = s_new
    y_o_ref[...] = jnp.sum(s_new * r_ref[...], axis=-1, keepdims=True)


def _decode_wkv(s, w, kk, bvec, k, r, v, blk):
    bh = s.shape[0]
    lane = pl.BlockSpec((blk, 1, RWKV_HEAD), lambda i: (i, 0, 0))
    col = pl.BlockSpec((blk, RWKV_HEAD, 1), lambda i: (i, 0, 0))
    mat = pl.BlockSpec((blk, RWKV_HEAD, RWKV_HEAD), lambda i: (i, 0, 0))
    return pl.pallas_call(
        _decode_wkv_kernel,
        grid=(bh // blk,),
        in_specs=[mat, lane, lane, lane, lane, lane, col],
        out_specs=[mat, col],
        out_shape=[jax.ShapeDtypeStruct(s.shape, F32), jax.ShapeDtypeStruct((bh, RWKV_HEAD, 1), F32)],
        compiler_params=_cparams("parallel"),
        name="decode_wkv",
    )(s, w, kk, bvec, k, r, v)


def _decode_post_kernel(y_ref, bonus_ref, g_ref, lng_ref, lnb_ref, ones_ref, o_ref):
    o_ref[...] = _rwkv_post(y_ref[...], bonus_ref[...], g_ref[...], lng_ref[...], lnb_ref[...], ones_ref[...])


def _decode_post(y, bonus, g, wl):
    ins = (y, bonus, g, wl["rwkv_ln_g"], wl["rwkv_ln_b"], wl["ones_bd"])
    return pl.pallas_call(
        _decode_post_kernel,
        in_specs=[_full(x.shape) for x in ins],
        out_specs=_full(y.shape),
        out_shape=jax.ShapeDtypeStruct(y.shape, F32),
        compiler_params=_cparams(),
        name="decode_post",
    )(*ins)


_CELLS = [(a, b) for a in range(PEER_TOPK) for b in range(PEER_TOPK) if (a + 1) * (b + 1) <= PEER_TOPK]
_CELL_PAIRS = [(c, d) for c in _CELLS for d in _CELLS if d[0] < c[0] and d[1] > c[1]]
_PAIRS_AS_D = {x: sum(1 for _, d in _CELL_PAIRS if d == x) for x in _CELLS}


def _top16_rows(s, tb, vals_ref, h, tie_safe, want_rank):
    lanes = 128
    iota = lax.broadcasted_iota(jnp.int32, (N_KEYS, lanes), 0).astype(F32)
    ranks, counts = [], []
    for c0 in range(0, tb, lanes):
        sc = s[:, c0:c0 + lanes]
        rank = jnp.full((N_KEYS, lanes), float(PEER_TOPK), F32)
        for r in range(PEER_TOPK):
            m = jnp.max(sc, axis=0, keepdims=True)
            sel = sc == m
            if tie_safe:
                sel = iota == jnp.min(jnp.where(sel, iota, float(N_KEYS)), axis=0, keepdims=True)
            if want_rank:
                rank = jnp.where(sel, float(r), rank)
            sc = jnp.where(sel, -jnp.inf, sc)
            vals_ref[r, h:h + 1, c0:c0 + lanes] = m
        ranks.append(rank)
        counts.append(jnp.sum(jnp.where(sc == -jnp.inf, 1.0, 0.0), axis=0, keepdims=True))
    cat = lambda xs: jnp.concatenate(xs, axis=1) if len(xs) > 1 else xs[0]
    return (cat(ranks) if want_rank else None), cat(counts)


def _peer_route_kernel(h_ref, g_ref, wq_ref, k1_ref, k2_ref,
                       xn_ref, r2_ref, p2_ref, c1_ref, p1_ref,
                       v1_s, v2_s, r1_s, cnt_s, *, tb):
    refs = (h_ref, g_ref, wq_ref, k1_ref, k2_ref, xn_ref, r2_ref, p2_ref, c1_ref, p1_ref, v1_s, v2_s, r1_s, cnt_s)
    n_bad = _peer_route_body(*refs, tb=tb, tie_safe=False)

    @pl.when(jnp.max(n_bad) > 0.0)
    def _():
        _peer_route_body(*refs, tb=tb, tie_safe=True)


def _peer_route_body(h_ref, g_ref, wq_ref, k1_ref, k2_ref,
                     xn_ref, r2_ref, p2_ref, c1_ref, p1_ref,
                     v1_s, v2_s, r1_s, cnt_s, *, tb, tie_safe):
    xn = _rmsnorm(h_ref[...], g_ref[...])
    xb = xn.astype(BF16)
    xn_ref[...] = xn.T.astype(BF16)
    n_bad = jnp.zeros((1, tb), F32)
    for h in range(PEER_HEADS):
        q = jnp.dot(xb, wq_ref[:, h * 256:(h + 1) * 256], preferred_element_type=F32).astype(BF16)
        s1 = _dg(k1_ref[h], q[:, 0:128], _NT)
        s2 = _dg(k2_ref[h], q[:, 128:256], _NT)
        r1, n1 = _top16_rows(s1, tb, v1_s, h, tie_safe, want_rank=tie_safe)
        r2, n2 = _top16_rows(s2, tb, v2_s, h, tie_safe, want_rank=True)
        n_bad = n_bad + jnp.where(n1 != float(PEER_TOPK), 1.0, 0.0) + jnp.where(n2 != float(PEER_TOPK), 1.0, 0.0)
        r1_s[h] = r1 if tie_safe else s1
        r2_ref[h] = r2.astype(BF16)
        p1_ref[h] = jnp.exp(s1 - v1_s[0, h:h + 1, :])
        p2_ref[h] = jnp.exp(s2 - v2_s[0, h:h + 1, :]).astype(BF16)

    sums = {c: v1_s[c[0]] + v2_s[c[1]] for c in _CELLS}
    rank = {c: jnp.full((PEER_HEADS, tb), float((c[0] + 1) * (c[1] + 1) - 1 + _PAIRS_AS_D[c]), F32) for c in _CELLS}
    for c, d in _CELL_PAIRS:
        won = jnp.where(sums[d] >= sums[c], 1.0, 0.0)
        rank[c] = rank[c] + won
        rank[d] = rank[d] - won
    e1 = [jnp.exp(v1_s[a] - v1_s[0]) for a in range(PEER_TOPK)]
    e2 = [jnp.exp(v2_s[b] - v2_s[0]) for b in range(PEER_TOPK)]
    z = jnp.zeros((PEER_HEADS, tb), F32)
    cnt = [jnp.zeros((PEER_HEADS, tb), F32) for _ in range(PEER_TOPK)]
    for c in _CELLS:
        sel = rank[c] < float(PEER_TOPK)
        cnt[c[0]] = cnt[c[0]] + jnp.where(sel, 1.0, 0.0)
        z = z + jnp.where(sel, e1[c[0]] * e2[c[1]], 0.0)
    for a in range(PEER_TOPK):
        cnt_s[a] = cnt[a]
    cnt_s[PEER_TOPK] = 0.5 / z

    for h in range(PEER_HEADS):
        r1 = r1_s[h]
        c1 = jnp.zeros((N_KEYS, tb), F32)
        for a in range(PEER_TOPK):
            hit = r1 == (float(a) if tie_safe else v1_s[a, h:h + 1, :])
            c1 = jnp.where(hit, cnt_s[a, h:h + 1, :], c1)
        c1_ref[h] = c1
        p1_ref[h] = p1_ref[h] * cnt_s[PEER_TOPK, h:h + 1, :]
    return n_bad


def _peer_route(hres, g, wq_bf16, k1_bf16, k2_bf16, tb):
    n, d = hres.shape
    tb = min(tb, n)
    gate = pl.BlockSpec((PEER_HEADS, N_KEYS, tb), lambda i: (0, 0, i))
    gshape = lambda dt: jax.ShapeDtypeStruct((PEER_HEADS, N_KEYS, n), dt)
    return pl.pallas_call(
        functools.partial(_peer_route_kernel, tb=tb),
        grid=(n // tb,),
        in_specs=[pl.BlockSpec((tb, d), lambda i: (i, 0)), _full((1, d)), _full(wq_bf16.shape),
                  _full(k1_bf16.shape), _full(k2_bf16.shape)],
        out_specs=[pl.BlockSpec((d, tb), lambda i: (0, i)), gate, gate, gate, gate],
        out_shape=[jax.ShapeDtypeStruct((d, n), BF16), gshape(BF16), gshape(BF16), gshape(F32), gshape(F32)],
        scratch_shapes=[pltpu.VMEM((PEER_TOPK, PEER_HEADS, tb), F32), pltpu.VMEM((PEER_TOPK, PEER_HEADS, tb), F32),
                        pltpu.VMEM((PEER_HEADS, N_KEYS, tb), F32), pltpu.VMEM((PEER_TOPK + 1, PEER_HEADS, tb), F32)],
        compiler_params=_cparams("parallel"),
        name="peer_route",
    )(hres, g.reshape(1, d), wq_bf16, k1_bf16, k2_bf16)


_SQRT_HALF = float(np.sqrt(0.5))


def _peer_expert_kernel(xn_ref, hres_ref, u_ref, vt_ref, r2_ref, p2_ref, c1_ref, p1_ref, o_ref,
                        acc_s, ht_s, at_s, *, eb):
    j = pl.program_id(1)
    per = eb // N_KEYS

    @pl.when(j == 0)
    def _():
        acc_s[...] = jnp.zeros_like(acc_s)

    tb = ht_s.shape[1]
    ht_s[...] = jnp.dot(u_ref[...], xn_ref[...], preferred_element_type=F32)
    for i in range(per):
        i1 = j * per + i

        def row_tile(ref, h):
            row = jnp.broadcast_to(ref[h, pl.ds(i1, 1), :], (16, tb)).astype(BF16)
            return jnp.concatenate([row] * (N_KEYS // 16), axis=0)

        gt = None
        for h in range(PEER_HEADS):
            c1 = row_tile(c1_ref, h)
            p1 = row_tile(p1_ref, h)
            term = jnp.where(r2_ref[h] < c1, p2_ref[h], jnp.zeros((), BF16)) * p1
            gt = term if gt is None else gt + term
        ht = ht_s[i * N_KEYS:(i + 1) * N_KEYS, :]
        act = ht * (1.0 + lax.erf(ht * _SQRT_HALF))
        at_s[i * N_KEYS:(i + 1) * N_KEYS, :] = act.astype(BF16) * gt
    acc_s[...] += jnp.dot(vt_ref[0], at_s[...], preferred_element_type=F32)

    @pl.when(j == pl.num_programs(1) - 1)
    def _():
        o_ref[...] = hres_ref[...] + acc_s[...].T


def _peer_experts(xn, hres, u_bf16, vt_bf16, gates, tb, eb):
    n, d = hres.shape
    tb = min(tb, n)
    nblk = u_bf16.shape[0] // eb
    gate = pl.BlockSpec((PEER_HEADS, N_KEYS, tb), lambda i, j: (0, 0, i))
    return pl.pallas_call(
        functools.partial(_peer_expert_kernel, eb=eb),
        grid=(n // tb, nblk),
        in_specs=[pl.BlockSpec((d, tb), lambda i, j: (0, i)), pl.BlockSpec((tb, d), lambda i, j: (i, 0)),
                  pl.BlockSpec((eb, d), lambda i, j: (j, 0)),
                  pl.BlockSpec((1, d, eb), lambda i, j: (j, 0, 0)),
                  gate, gate, gate, gate],
        out_specs=pl.BlockSpec((tb, d), lambda i, j: (i, 0)),
        out_shape=jax.ShapeDtypeStruct((n, d), F32),
        scratch_shapes=[pltpu.VMEM((d, tb), F32), pltpu.VMEM((eb, tb), F32), pltpu.VMEM((eb, tb), BF16)],
        compiler_params=_cparams("parallel", "arbitrary"),
        name="peer_experts",
    )(xn, hres, u_bf16, vt_bf16, *gates)


def _layer_weights(l, w):
    row = lambda a: a[l].reshape(1, -1)
    perm = _RWKV_PERM
    w_in = w["w_in"][l]
    w_in = jnp.concatenate([w_in[:, 0:512], w_in[:, 512:1408][:, perm], w_in[:, 1408:]], axis=1).astype(BF16)
    eye4 = jnp.eye(RWKV_H, dtype=F32)
    ones_bd = jnp.kron(eye4, jnp.ones((RWKV_HEAD, RWKV_HEAD), F32)).astype(BF16)
    pool_wbd = jax.scipy.linalg.block_diag(*[w["pool_w"][l, gi] for gi in range(4)]).astype(BF16)
    return dict(
        norm1_g=w["norm1_g"][l], norm2_g=w["norm2_g"][l], w_in=w_in,
        conf_dw_w=w["conf_dw_w"][l], conf_dw_b=row(w["conf_dw_b"]), conf_ln_g=row(w["conf_ln_g"]),
        conf_ln_b=row(w["conf_ln_b"]), sc_conv_w=w["sc_conv_w"][l], pool_wbd=pool_wbd, pool_scale=row(w["pool_scale"]),
        rwkv_mu=w["rwkv_mu"][l][perm].reshape(1, -1), rwkv_w0=row(w["rwkv_w0"]), rwkv_w2=w["rwkv_w2"][l].astype(BF16),
        rwkv_a0=row(w["rwkv_a0"]), rwkv_a2=w["rwkv_a2"][l].astype(BF16), rwkv_g2=w["rwkv_g2"][l].astype(BF16),
        rwkv_k_k=row(w["rwkv_k_k"]), rwkv_k_a=row(w["rwkv_k_a"]), rwkv_r_k=row(w["rwkv_r_k"]),
        rwkv_ln_g=row(w["rwkv_ln_g"]), rwkv_ln_b=row(w["rwkv_ln_b"]), ones_bd=ones_bd,
        w_out=w["w_out"][l].astype(BF16), peer_wq=w["peer_wq"][l].astype(BF16),
        peer_k1=w["peer_k1"][l].astype(BF16), peer_k2=w["peer_k2"][l].astype(BF16),
        peer_u=w["peer_u"][l].astype(BF16),
        peer_vt=jnp.transpose(w["peer_v"][l].astype(BF16).reshape(-1, EXPERT_BLOCK, D_MODEL), (0, 2, 1)),
    )


def _peer_block(hres, wl, tb_route, tb_exp, eb):
    xn, r2, p2, c1, p1 = _peer_route(hres, wl["norm2_g"], wl["peer_wq"], wl["peer_k1"], wl["peer_k2"], tb_route)
    return _peer_experts(xn, hres, wl["peer_u"], wl["peer_vt"], (r2, p2, c1, p1), tb_exp, eb)


def _prompt_layer(x, wl, bsz, t):
    n = bsz * t
    za, zb, zc, zd = _norm_proj(x, wl["norm1_g"], wl["w_in"], IN_SPLITS, 512)
    z3 = lambda a: a.reshape(bsz, t, -1)
    zeros = lambda *s: jnp.zeros(s, F32)
    ya, yc, yd, conf, sc, pool = _prompt_acd(z3(za), z3(zc), z3(zd), zeros(bsz, 30, GROUP_W), zeros(bsz, 2, GROUP_W),
                                             zeros(bsz, 15, GROUP_W), wl, 256, 0)
    yb, shift, wkv = _prompt_rwkv(z3(zb), zeros(bsz, 1, RWKV_COLS), zeros(bsz, RWKV_H, RWKV_HEAD, RWKV_HEAD), wl,
                                   WKV_SEQS_PER_STEP)
    flat = lambda a: a.reshape(n, GROUP_W)
    hres = _out_proj(x, (flat(ya), flat(yb), flat(yc), flat(yd)), wl["w_out"], 512)
    x = _peer_block(hres, wl, 256, 512, EXPERT_BLOCK)
    return x, (conf, shift.reshape(bsz, RWKV_COLS)[:, _RWKV_INV_PERM], wkv, sc, pool)


def _decode_layer(x, states, wl, start_pos):
    conf, shift, wkv, sc, pool = states
    n = x.shape[0]
    za, zb, zc, zd = _norm_proj(x, wl["norm1_g"], wl["w_in"], IN_SPLITS, 128)
    tr = lambda a: jnp.transpose(a, (1, 0, 2))
    (ya, yc, yd, conf_n, sc_n, pool_n, r, w, k2, v, kk, bvec, g, bonus) = _decode_acd_prep(
        za, zb, zc, zd, tr(conf), shift[:, _RWKV_PERM], tr(sc), tr(pool), wl, start_pos)
    bh = n * RWKV_H
    lane = lambda a: a.reshape(bh, 1, RWKV_HEAD)
    s_new, y = _decode_wkv(wkv.reshape(bh, RWKV_HEAD, RWKV_HEAD), lane(w), lane(kk), lane(bvec), lane(k2), lane(r),
                           v.reshape(bh, RWKV_HEAD, 1), 64)
    yb = _decode_post(y.reshape(n, GROUP_W), bonus, g, wl)
    hres = _out_proj(x, (ya, yb, yc, yd), wl["w_out"], 128)
    x = _peer_block(hres, wl, 128, 128, EXPERT_BLOCK)
    return x, (tr(conf_n), zb[:, _RWKV_INV_PERM], s_new.reshape(n, RWKV_H, RWKV_HEAD, RWKV_HEAD), tr(sc_n), tr(pool_n))


def kernel(x_prompt, x_sample, state_conformer, state_rwkv_shift, state_rwkv_wkv, state_shortconv, state_pool, norm1_g, norm2_g, final_norm_g, w_in, conf_dw_w, conf_dw_b, conf_ln_g, conf_ln_b, rwkv_mu, rwkv_w0, rwkv_w2, rwkv_a0, rwkv_a2, rwkv_g2, rwkv_k_k, rwkv_k_a, rwkv_r_k, rwkv_ln_g, rwkv_ln_b, sc_conv_w, pool_w, pool_scale, w_out, peer_wq, peer_k1, peer_k2, peer_u, peer_v):
    w = dict(norm1_g=norm1_g, norm2_g=norm2_g, w_in=w_in, conf_dw_w=conf_dw_w, conf_dw_b=conf_dw_b,
             conf_ln_g=conf_ln_g, conf_ln_b=conf_ln_b, rwkv_mu=rwkv_mu, rwkv_w0=rwkv_w0, rwkv_w2=rwkv_w2,
             rwkv_a0=rwkv_a0, rwkv_a2=rwkv_a2, rwkv_g2=rwkv_g2, rwkv_k_k=rwkv_k_k, rwkv_k_a=rwkv_k_a,
             rwkv_r_k=rwkv_r_k, rwkv_ln_g=rwkv_ln_g, rwkv_ln_b=rwkv_ln_b, sc_conv_w=sc_conv_w, pool_w=pool_w,
             pool_scale=pool_scale, w_out=w_out, peer_wq=peer_wq, peer_k1=peer_k1, peer_k2=peer_k2,
             peer_u=peer_u, peer_v=peer_v)
    depth = w_in.shape[0]
    bsz, t, d = x_prompt.shape
    nb, dt, _ = x_sample.shape
    past_len = 16384

    xp = x_prompt.reshape(bsz * t, d)
    xs = x_sample.reshape(nb * dt, d)
    p_states, s_states = [], []
    for l in range(depth):
        wl = _layer_weights(l, w)
        xp, ps = _prompt_layer(xp, wl, bsz, t)
        xs, ss = _decode_layer(xs, (state_conformer[l], state_rwkv_shift[l], state_rwkv_wkv[l],
                                    state_shortconv[l], state_pool[l]), wl, past_len)
        p_states.append(ps)
        s_states.append(ss)
    y_prompt = _final_norm(xp, final_norm_g, 512).reshape(bsz, t, d)
    y_sample = _final_norm(xs, final_norm_g, 128).reshape(nb, dt, d)
    stack = lambda lst, i: jnp.stack([s[i] for s in lst], axis=0)
    conf_p, shift_p, wkv_p, sc_p, pool_p = (stack(p_states, i) for i in range(5))
    conf_s, shift_s, wkv_s, sc_s, pool_s = (stack(s_states, i) for i in range(5))
    return (y_prompt, y_sample, conf_p, conf_s, shift_p, shift_s, wkv_p, wkv_s, sc_p, sc_s, pool_p, pool_s)
```

```python
import functools

import jax
import jax.numpy as jnp
import numpy as np
from jax import lax
from jax.experimental import pallas as pl
from jax.experimental.pallas import tpu as pltpu

F32 = jnp.float32
BF16 = jnp.bfloat16

D_MODEL = 1024
GROUP_W = 256
CONF_K = 31
RWKV_HEAD = 64
RWKV_H = 4
RWKV_COLS = 896
RWKV_GN_EPS = 64e-5
POOL_WINDOWS = (2, 4, 8, 16)
POOL_BUF = 15
N_KEYS = 128
PEER_HEADS = 8
PEER_TOPK = 16
RMS_EPS = 1e-6
LN_EPS = 1e-5
IN_SPLITS = (512, 896, 768, 256)

VMEM_LIMIT_BYTES = 56 * 1024 * 1024
WKV_CHUNK = 64
EXPERT_BLOCK = 2048
TABLE_CAST_ROWS = 512
WKV_SEQS_PER_STEP = 8
WKV_PASSES = (1, 1, 1, 1)

_RWKV_PERM = np.concatenate([np.arange(0, 256), np.arange(288, 544), np.arange(544, 800),
                             np.arange(256, 288), np.arange(800, 832), np.arange(832, 896)])
_RWKV_INV_PERM = np.argsort(_RWKV_PERM)


def _cparams(*sem):
    return pltpu.CompilerParams(dimension_semantics=tuple(sem) if sem else None,
                                vmem_limit_bytes=VMEM_LIMIT_BYTES)


def _full(shape):
    n = len(shape)
    return pl.BlockSpec(shape, lambda *_: (0,) * n)


def _split2(x):
    hi = x.astype(BF16)
    lo = (x - hi.astype(F32)).astype(BF16)
    return hi, lo


def _split3(x):
    hi = x.astype(BF16)
    r = x - hi.astype(F32)
    mid = r.astype(BF16)
    lo = (r - mid.astype(F32)).astype(BF16)
    return hi, mid, lo


_NN = (((1,), (0,)), ((), ()))
_NT = (((1,), (1,)), ((), ()))


def _dg(a, b, dims):
    return lax.dot_general(a, b, dims, preferred_element_type=F32)


def _mm3(a, b, dims=_NN):
    ah, al = _split2(a)
    bh, bl = _split2(b)
    return _dg(ah, bh, dims) + (_dg(al, bh, dims) + _dg(ah, bl, dims))


def _mm(a, b, dims=_NN, passes=3):
    if passes == 1:
        return _dg(a.astype(BF16), b.astype(BF16), dims)
    return _mm3(a, b, dims)


def _mm_exact_rhs(a, b_bf16):
    h, m, l = _split3(a)
    return _dg(h, b_bf16, _NN) + (_dg(m, b_bf16, _NN) + _dg(l, b_bf16, _NN))


def _mm_exact_lhs(a_bf16, b):
    h, m, l = _split3(b)
    return _dg(a_bf16, h, _NN) + (_dg(a_bf16, m, _NN) + _dg(a_bf16, l, _NN))


def _transpose_mxu(x, eye_bf16):
    h, m, l = _split3(x)
    return _dg(eye_bf16, h, _NT) + (_dg(eye_bf16, m, _NT) + _dg(eye_bf16, l, _NT))


def _eye(n, dtype):
    return (lax.broadcasted_iota(jnp.int32, (n, n), 0) == lax.broadcasted_iota(jnp.int32, (n, n), 1)).astype(dtype)


def _rmsnorm(x, g):
    ms = jnp.mean(x * x, axis=-1, keepdims=True)
    return x * lax.rsqrt(ms + RMS_EPS) * g


def _sigmoid(x):
    return 1.0 / (1.0 + jnp.exp(-x))


def _softplus(x):
    return jnp.maximum(x, 0.0) + jnp.log(1.0 + jnp.exp(-jnp.abs(x)))


def _norm_proj_kernel(x_ref, g_ref, w_ref, *o_refs, splits):
    xb = _rmsnorm(x_ref[...], g_ref[...]).astype(BF16)
    off = 0
    for o_ref, wd in zip(o_refs, splits):
        o_ref[...] = jnp.dot(xb, w_ref[:, off:off + wd], preferred_element_type=F32)
        off += wd


def _norm_proj(x, g, w_bf16, splits, tm):
    n, d = x.shape
    tm = min(tm, n)
    cols = w_bf16.shape[1]
    return pl.pallas_call(
        functools.partial(_norm_proj_kernel, splits=splits),
        grid=(n // tm,),
        in_specs=[pl.BlockSpec((tm, d), lambda i: (i, 0)), _full((1, d)), _full((d, cols))],
        out_specs=[pl.BlockSpec((tm, wd), lambda i: (i, 0)) for wd in splits],
        out_shape=[jax.ShapeDtypeStruct((n, wd), F32) for wd in splits],
        compiler_params=_cparams("parallel"),
        name="norm_proj",
    )(x, g.reshape(1, d), w_bf16)


def _out_proj_kernel(x_ref, ya_ref, yb_ref, yc_ref, yd_ref, w_ref, o_ref):
    acc = x_ref[...]
    for i, y_ref in enumerate((ya_ref, yb_ref, yc_ref, yd_ref)):
        acc = acc + jnp.dot(y_ref[...].astype(BF16), w_ref[i * GROUP_W:(i + 1) * GROUP_W, :],
                            preferred_element_type=F32)
    o_ref[...] = acc


def _out_proj(x, ys, w_bf16, tm):
    n, d = x.shape
    tm = min(tm, n)
    row = lambda i: (i, 0)
    return pl.pallas_call(
        _out_proj_kernel,
        grid=(n // tm,),
        in_specs=[pl.BlockSpec((tm, d), row)] + [pl.BlockSpec((tm, GROUP_W), row)] * 4 + [_full((d, d))],
        out_specs=pl.BlockSpec((tm, d), row),
        out_shape=jax.ShapeDtypeStruct((n, d), F32),
        compiler_params=_cparams("parallel"),
        name="out_proj",
    )(x, *ys, w_bf16)


def _table_cast_kernel(u_ref, v_ref, ub_ref, vt_ref):
    ub_ref[...] = u_ref[...].astype(BF16)
    vt_ref[0] = v_ref[...].T.astype(BF16)


def _table_cast(u, v, l, eb):
    _, e, d = u.shape
    rows = TABLE_CAST_ROWS
    per = eb // rows
    return pl.pallas_call(
        _table_cast_kernel,
        grid=(e // rows,),
        in_specs=[pl.BlockSpec((None, rows, d), lambda i: (l, i, 0)), pl.BlockSpec((None, rows, d), lambda i: (l, i, 0))],
        out_specs=[pl.BlockSpec((rows, d), lambda i: (i, 0)), pl.BlockSpec((1, d, rows), lambda i: (i // per, 0, i % per))],
        out_shape=[jax.ShapeDtypeStruct((e, d), BF16), jax.ShapeDtypeStruct((e // eb, d, eb), BF16)],
        compiler_params=_cparams("parallel"),
        name="table_cast",
    )(u, v)


def _layernorm_lanes(x, g, b, eps):
    mu = jnp.mean(x, axis=-1, keepdims=True)
    xc = x - mu
    var = jnp.mean(xc * xc, axis=-1, keepdims=True)
    return xc * lax.rsqrt(var + eps) * g + b


def _conformer_tail(ca, lng, lnb):
    y = _layernorm_lanes(ca, lng, lnb, LN_EPS)
    return y * _sigmoid(y)


def _pool_select(w2, w4, w8, w16, shape):
    lane = lax.broadcasted_iota(jnp.int32, shape, len(shape) - 1)
    wsum = jnp.where(lane < 64, w2, jnp.where(lane < 128, w4, jnp.where(lane < 192, w8, w16)))
    win = jnp.where(lane < 64, 2.0, jnp.where(lane < 128, 4.0, jnp.where(lane < 192, 8.0, 16.0)))
    return wsum, win


def _head_sum(x, ones_bd):
    return _mm_exact_rhs(x, ones_bd)


def _rwkv_prep(p, prev, wr):
    (mu, w0, w2, a0, a2, g2, k_k, k_a, r_k, ones_bd) = wr
    xs = p + (prev - p) * mu
    r = xs[:, 0:256]
    k = xs[:, 256:512]
    v = xs[:, 512:768]
    w_lo = xs[:, 768:800]
    a_lo = xs[:, 800:832]
    g_lo = xs[:, 832:896]
    wexp = -_softplus(-(w0 + jnp.dot(jnp.tanh(w_lo).astype(BF16), w2, preferred_element_type=F32))) - 0.5
    logdecay = -jnp.exp(wexp)
    a = _sigmoid(a0 + jnp.dot(a_lo.astype(BF16), a2, preferred_element_type=F32))
    g = jnp.dot(_sigmoid(g_lo).astype(BF16), g2, preferred_element_type=F32)
    kk = k * k_k
    kk = kk * lax.rsqrt(jnp.maximum(_head_sum(kk * kk, ones_bd), 1e-24))
    k2 = k * (1.0 + (a - 1.0) * k_a)
    bonus = _head_sum(r * k2 * r_k, ones_bd) * v
    return r, logdecay, k2, v, kk, a, g, bonus


def _rwkv_post(y, bonus, g, lng, lnb, ones_bd):
    mu = _head_sum(y, ones_bd) * (1.0 / RWKV_HEAD)
    yc = y - mu
    var = _head_sum(yc * yc, ones_bd) * (1.0 / RWKV_HEAD)
    yn = yc * lax.rsqrt(var + RWKV_GN_EPS) * lng + lnb
    return (yn + bonus) * g


_CONV_ROWS = 64


def _prompt_acd_kernel(za_ref, zc_ref, zd_ref, conf0_ref, sc0_ref, pool0_ref,
                       cw_ref, cb_ref, clg_ref, clb_ref, scw_ref, pw_ref, ps_ref,
                       ya_ref, yc_ref, yd_ref, conf_o_ref, sc_o_ref, pool_o_ref,
                       ext_a, ext_c, ext_d, sh_a, *, tt, start_pos):
    t = pl.program_id(1)

    @pl.when(t == 0)
    def _():
        ext_a[0:2, :] = jnp.zeros((2, GROUP_W), F32)
        ext_a[2:32, :] = conf0_ref[0]
        ext_a[tt + 32:tt + 40, :] = jnp.zeros((8, GROUP_W), F32)
        ext_c[0:6, :] = jnp.zeros((6, GROUP_W), F32)
        ext_c[6:8, :] = sc0_ref[0]
        ext_d[0:1, :] = jnp.zeros((1, GROUP_W), F32)
        ext_d[1:16, :] = pool0_ref[0]

    za = za_ref[0]
    ext_a[32:32 + tt, :] = za[:, 0:GROUP_W] * _sigmoid(za[:, GROUP_W:2 * GROUP_W])
    for r in range(1, 8):
        sh_a[r - 1] = ext_a[r:r + tt + 32, :]
    for c in range(tt // _CONV_ROWS):
        base = c * _CONV_ROWS
        acc = jnp.zeros((_CONV_ROWS, GROUP_W), F32) + cb_ref[...]
        for k in range(CONF_K):
            off = base + k + 2
            r = off % 8
            win = (ext_a[off:off + _CONV_ROWS, :] if r == 0
                   else sh_a[r - 1, off - r:off - r + _CONV_ROWS, :])
            acc = acc + cw_ref[k:k + 1, :] * win
        ya_ref[0, base:base + _CONV_ROWS, :] = _conformer_tail(acc, clg_ref[...], clb_ref[...])
    conf_o_ref[0] = ext_a[tt + 2:tt + 32, :]
    ext_a[0:32, :] = ext_a[tt:tt + 32, :]

    zc = zc_ref[0]
    ext_c[8:8 + tt, :] = zc[:, GROUP_W:2 * GROUP_W] * zc[:, 2 * GROUP_W:3 * GROUP_W]
    cc = (scw_ref[0:1, :] * ext_c[6:6 + tt, :] + scw_ref[1:2, :] * ext_c[7:7 + tt, :]
          + scw_ref[2:3, :] * ext_c[8:8 + tt, :])
    yc_ref[0] = zc[:, 0:GROUP_W] * cc
    sc_o_ref[0] = ext_c[tt + 6:tt + 8, :]
    ext_c[0:8, :] = ext_c[tt:tt + 8, :]

    u = zd_ref[0]
    ext_d[16:16 + tt, :] = u
    w2 = u + ext_d[15:15 + tt, :]
    w4 = w2 + ext_d[14:14 + tt, :] + ext_d[13:13 + tt, :]
    w8 = w4
    for j in range(4, 8):
        w8 = w8 + ext_d[16 - j:16 - j + tt, :]
    w16 = w8
    for j in range(8, 16):
        w16 = w16 + ext_d[16 - j:16 - j + tt, :]
    wsum, win = _pool_select(w2, w4, w8, w16, (tt, GROUP_W))
    pos = (lax.broadcasted_iota(jnp.int32, (tt, GROUP_W), 0) + (t * tt + start_pos + 1)).astype(F32)
    pooled = wsum / jnp.minimum(win, pos) - u
    yd_ref[0] = jnp.dot(pooled.astype(BF16), pw_ref[...], preferred_element_type=F32) * ps_ref[...]
    pool_o_ref[0] = ext_d[tt + 1:tt + 16, :]
    ext_d[0:16, :] = ext_d[tt:tt + 16, :]


def _prompt_acd(za, zc, zd, conf0, sc0, pool0, wl, tt, start_pos):
    b, t, _ = za.shape
    tile = lambda w: pl.BlockSpec((1, tt, w), lambda i, j: (i, j, 0))
    st = lambda r: pl.BlockSpec((1, r, GROUP_W), lambda i, j: (i, 0, 0))
    row = _full((1, GROUP_W))
    return pl.pallas_call(
        functools.partial(_prompt_acd_kernel, tt=tt, start_pos=start_pos),
        grid=(b, t // tt),
        in_specs=[tile(512), tile(768), tile(256), st(30), st(2), st(15),
                  _full((CONF_K, GROUP_W)), row, row, row, _full((3, GROUP_W)), _full((GROUP_W, GROUP_W)), row],
        out_specs=[tile(256), tile(256), tile(256), st(30), st(2), st(15)],
        out_shape=[jax.ShapeDtypeStruct((b, t, GROUP_W), F32)] * 3
        + [jax.ShapeDtypeStruct((b, r, GROUP_W), F32) for r in (30, 2, 15)],
        scratch_shapes=[pltpu.VMEM((40 + tt, GROUP_W), F32), pltpu.VMEM((8 + tt, GROUP_W), F32),
                        pltpu.VMEM((16 + tt, GROUP_W), F32), pltpu.VMEM((7, 32 + tt, GROUP_W), F32)],
        compiler_params=_cparams("parallel", "arbitrary"),
        name="prompt_acd",
    )(za, zc, zd, conf0, sc0, pool0, wl["conf_dw_w"], wl["conf_dw_b"], wl["conf_ln_g"], wl["conf_ln_b"],
      wl["sc_conv_w"], wl["pool_wbd"], wl["pool_scale"])


def _prompt_rwkv_kernel(zb_ref, shift0_ref, wkv0_ref, mu_ref, w0_ref, w2_ref, a0_ref, a2_ref, g2_ref,
                        kk_ref, ka_ref, rk_ref, lng_ref, lnb_ref, ones_ref, tri_ref,
                        yb_ref, shift_o_ref, wkv_o_ref, prev_s, st_s, *, c, nb):
    t = pl.program_id(1)
    nt = pl.num_programs(1)

    @pl.when(t == 0)
    def _():
        for bb in range(nb):
            prev_s[bb] = jnp.broadcast_to(shift0_ref[bb], prev_s.shape[1:])
            for h in range(RWKV_H):
                st_s[bb, h] = _transpose_mxu(wkv0_ref[bb, h], _eye(RWKV_HEAD, BF16))

    new_states = _rwkv_chunks(zb_ref, mu_ref, w0_ref, w2_ref, a0_ref, a2_ref, g2_ref, kk_ref, ka_ref, rk_ref,
                              lng_ref, lnb_ref, ones_ref, tri_ref, yb_ref, shift_o_ref, prev_s, st_s, c, nb)

    @pl.when(t == nt - 1)
    def _():
        eye_h = _eye(RWKV_HEAD, BF16)
        for bb in range(nb):
            for h in range(RWKV_H):
                wkv_o_ref[bb, h] = _transpose_mxu(new_states[(bb, h)], eye_h)


def _rwkv_chunks(zb_ref, mu_ref, w0_ref, w2_ref, a0_ref, a2_ref, g2_ref, kk_ref, ka_ref, rk_ref,
                 lng_ref, lnb_ref, ones_ref, tri_ref, yb_ref, shift_o_ref, prev_s, st_s, c, nb):
    p = zb_ref[...].reshape(nb * c, RWKV_COLS)
    row = lax.broadcasted_iota(jnp.int32, p.shape, 0)
    prev = pltpu.roll(p, 1, axis=0)
    for bb in range(nb):
        prev = jnp.where(row == bb * c, prev_s[bb, 0:1, :], prev)
    ones_bd = ones_ref[...]
    wr = (mu_ref[...], w0_ref[...], w2_ref[...], a0_ref[...], a2_ref[...], g2_ref[...],
          kk_ref[...], ka_ref[...], rk_ref[...], ones_bd)
    r, logdecay, k2, v, kk, a, g, bonus = _rwkv_prep(p, prev, wr)

    cum = _mm_exact_lhs(tri_ref[...], logdecay)
    g_end = [cum[(bb + 1) * c - 1:(bb + 1) * c, :] for bb in range(nb)]
    cum_end = jnp.concatenate([jnp.broadcast_to(ge, (c, GROUP_W)) for ge in g_end], axis=0)
    g_end = [jnp.exp(ge) for ge in g_end]
    e_neg = jnp.exp(-cum)
    e_end = jnp.exp(cum_end - cum)
    bvec = kk * a
    a_t = -kk * jnp.exp(cum - logdecay)
    r_t = r * jnp.exp(cum)
    b_h = bvec * e_neg
    k_h = k2 * e_neg
    b_e = bvec * e_end
    k_e = k2 * e_end

    ri = lax.broadcasted_iota(jnp.int32, (2 * c, 2 * c), 0)
    ci = lax.broadcasted_iota(jnp.int32, (2 * c, 2 * c), 1)
    rt, cs = ri & (c - 1), ci & (c - 1)
    keep = (rt > cs) | ((ri >= c) & (rt == cs))
    eye_2h = _eye(2 * RWKV_HEAD, BF16)
    eye_f = _eye(RWKV_HEAD, F32)

    p_nt, p_rhs, p_neu, p_out = WKV_PASSES
    chains = [(bb, h) for bb in range(nb) for h in range(RWKV_H)]

    def part(z, bb, h):
        return z[bb * c:(bb + 1) * c, h * RWKV_HEAD:(h + 1) * RWKV_HEAD]

    big = {ch: jnp.where(keep, _mm(jnp.concatenate([part(a_t, *ch), part(r_t, *ch)], axis=0),
                                   jnp.concatenate([part(b_h, *ch), part(k_h, *ch)], axis=0), _NT, p_nt), 0.0)
           for ch in chains}
    s0 = {ch: st_s[ch[0], ch[1]] for ch in chains}
    vh = {ch: part(v, *ch) for ch in chains}
    sa = {ch: _mm(jnp.concatenate([part(a_t, *ch), big[ch][0:c, c:2 * c]], axis=1),
                  jnp.concatenate([s0[ch], vh[ch]], axis=0), _NN, p_rhs) for ch in chains}
    x = {ch: big[ch][0:c, 0:c] for ch in chains}
    n_sq = int(np.log2(c))
    for step in range(n_sq):
        sa = {ch: sa[ch] + _mm(x[ch], sa[ch], _NN, p_neu) for ch in chains}
        if step + 1 < n_sq:
            x = {ch: _mm(x[ch], x[ch], _NN, p_neu) for ch in chains}
    bk = {ch: jnp.concatenate([part(b_e, *ch), part(k_e, *ch)], axis=1) for ch in chains}
    bk_t = {ch: _dg(eye_2h, bk[ch].astype(BF16), _NT) if p_out == 1 else _transpose_mxu(bk[ch], eye_2h)
            for ch in chains}
    y_h = {ch: _mm(jnp.concatenate([part(r_t, *ch), big[ch][c:2 * c, 0:c], big[ch][c:2 * c, c:2 * c]], axis=1),
                   jnp.concatenate([s0[ch], sa[ch], vh[ch]], axis=0), _NN, p_out) for ch in chains}
    s_new = {}
    for ch in chains:
        bb, h = ch
        g_h = g_end[bb][:, h * RWKV_HEAD:(h + 1) * RWKV_HEAD]
        s_new[ch] = _mm3(eye_f * g_h, s0[ch]) + _mm(
            jnp.concatenate([bk_t[ch][0:RWKV_HEAD], bk_t[ch][RWKV_HEAD:]], axis=1),
            jnp.concatenate([sa[ch], vh[ch]], axis=0), _NN, p_out)
        st_s[bb, h] = s_new[ch]

    y = jnp.concatenate([jnp.concatenate([y_h[(bb, h)] for h in range(RWKV_H)], axis=-1) for bb in range(nb)], axis=0)
    yb = _rwkv_post(y, bonus, g, lng_ref[...], lnb_ref[...], ones_bd)
    for bb in range(nb):
        yb_ref[bb] = yb[bb * c:(bb + 1) * c]
        last = p[(bb + 1) * c - 1:(bb + 1) * c, :]
        prev_s[bb] = jnp.broadcast_to(last, prev_s.shape[1:])
        shift_o_ref[bb] = last
    return s_new


def _rwkv_weight_args(wl):
    return (wl["rwkv_mu"], wl["rwkv_w0"], wl["rwkv_w2"], wl["rwkv_a0"], wl["rwkv_a2"], wl["rwkv_g2"],
            wl["rwkv_k_k"], wl["rwkv_k_a"], wl["rwkv_r_k"], wl["rwkv_ln_g"], wl["rwkv_ln_b"], wl["ones_bd"])


_RWKV_WEIGHT_SPECS = [(1, RWKV_COLS), (1, GROUP_W), (32, GROUP_W), (1, GROUP_W), (32, GROUP_W), (64, GROUP_W),
                      (1, GROUP_W), (1, GROUP_W), (1, GROUP_W), (1, GROUP_W), (1, GROUP_W), (GROUP_W, GROUP_W)]


def _prompt_rwkv(zb, shift0, wkv0, wl, nb):
    b, t, _ = zb.shape
    c = WKV_CHUNK
    nb = min(nb, b)
    tri = jnp.kron(jnp.eye(nb, dtype=F32), jnp.tril(jnp.ones((c, c), F32))).astype(BF16)
    return pl.pallas_call(
        functools.partial(_prompt_rwkv_kernel, c=c, nb=nb),
        grid=(b // nb, t // c),
        in_specs=[pl.BlockSpec((nb, c, RWKV_COLS), lambda i, j: (i, j, 0)),
                  pl.BlockSpec((nb, 1, RWKV_COLS), lambda i, j: (i, 0, 0)),
                  pl.BlockSpec((nb, RWKV_H, RWKV_HEAD, RWKV_HEAD), lambda i, j: (i, 0, 0, 0))]
        + [_full(s) for s in _RWKV_WEIGHT_SPECS] + [_full((nb * c, nb * c))],
        out_specs=[pl.BlockSpec((nb, c, GROUP_W), lambda i, j: (i, j, 0)),
                   pl.BlockSpec((nb, 1, RWKV_COLS), lambda i, j: (i, 0, 0)),
                   pl.BlockSpec((nb, RWKV_H, RWKV_HEAD, RWKV_HEAD), lambda i, j: (i, 0, 0, 0))],
        out_shape=[jax.ShapeDtypeStruct((b, t, GROUP_W), F32), jax.ShapeDtypeStruct((b, 1, RWKV_COLS), F32),
                   jax.ShapeDtypeStruct((b, RWKV_H, RWKV_HEAD, RWKV_HEAD), F32)],
        scratch_shapes=[pltpu.VMEM((nb, 8, RWKV_COLS), F32), pltpu.VMEM((nb, RWKV_H, RWKV_HEAD, RWKV_HEAD), F32)],
        compiler_params=_cparams("parallel", "arbitrary"),
        name="prompt_rwkv",
    )(zb, shift0, wkv0, *_rwkv_weight_args(wl), tri)


def _decode_acd_prep_kernel(za_ref, zb_ref, zc_ref, zd_ref, conf_ref, shift_ref, sc_ref, pool_ref,
                            cw_ref, cb_ref, clg_ref, clb_ref, scw_ref, pw_ref, ps_ref,
                            mu_ref, w0_ref, w2_ref, a0_ref, a2_ref, g2_ref, kk_ref, ka_ref, rk_ref, ones_ref,
                            ya_ref, yc_ref, yd_ref, conf_o_ref, sc_o_ref, pool_o_ref,
                            r_o, w_o, k_o, v_o, kkn_o, b_o, g_o, bonus_o, *, start_pos):
    za = za_ref[...]
    glu = za[:, 0:GROUP_W] * _sigmoid(za[:, GROUP_W:2 * GROUP_W])
    acc = cb_ref[...] + cw_ref[CONF_K - 1:CONF_K, :] * glu
    for k in range(CONF_K - 1):
        acc = acc + cw_ref[k:k + 1, :] * conf_ref[k]
    ya_ref[...] = _conformer_tail(acc, clg_ref[...], clb_ref[...])
    for k in range(CONF_K - 2):
        conf_o_ref[k] = conf_ref[k + 1]
    conf_o_ref[CONF_K - 2] = glu

    zc = zc_ref[...]
    u = zc[:, GROUP_W:2 * GROUP_W] * zc[:, 2 * GROUP_W:3 * GROUP_W]
    cc = scw_ref[0:1, :] * sc_ref[0] + scw_ref[1:2, :] * sc_ref[1] + scw_ref[2:3, :] * u
    yc_ref[...] = zc[:, 0:GROUP_W] * cc
    sc_o_ref[0] = sc_ref[1]
    sc_o_ref[1] = u

    d = zd_ref[...]
    w2 = d + pool_ref[POOL_BUF - 1]
    w4 = w2 + pool_ref[POOL_BUF - 2] + pool_ref[POOL_BUF - 3]
    w8 = w4
    for j in range(4, 8):
        w8 = w8 + pool_ref[POOL_BUF - j]
    w16 = w8
    for j in range(8, 16):
        w16 = w16 + pool_ref[POOL_BUF - j]
    wsum, win = _pool_select(w2, w4, w8, w16, d.shape)
    pooled = wsum / jnp.minimum(win, float(start_pos + 1)) - d
    yd_ref[...] = jnp.dot(pooled.astype(BF16), pw_ref[...], preferred_element_type=F32) * ps_ref[...]
    for k in range(POOL_BUF - 1):
        pool_o_ref[k] = pool_ref[k + 1]
    pool_o_ref[POOL_BUF - 1] = d

    wr = (mu_ref[...], w0_ref[...], w2_ref[...], a0_ref[...], a2_ref[...], g2_ref[...],
          kk_ref[...], ka_ref[...], rk_ref[...], ones_ref[...])
    r, logdecay, k2, v, kk, a, g, bonus = _rwkv_prep(zb_ref[...], shift_ref[...], wr)
    r_o[...] = r
    w_o[...] = jnp.exp(logdecay)
    k_o[...] = k2
    v_o[...] = v
    kkn_o[...] = kk
    b_o[...] = kk * a
    g_o[...] = g
    bonus_o[...] = bonus


def _decode_acd_prep(za, zb, zc, zd, conf_t, shift, sc_t, pool_t, wl, start_pos):
    n = za.shape[0]
    ins = (za, zb, zc, zd, conf_t, shift, sc_t, pool_t, wl["conf_dw_w"], wl["conf_dw_b"], wl["conf_ln_g"],
           wl["conf_ln_b"], wl["sc_conv_w"], wl["pool_wbd"], wl["pool_scale"]) + _rwkv_weight_args(wl)[:9] + (wl["ones_bd"],)
    vec = jax.ShapeDtypeStruct((n, GROUP_W), F32)
    outs = [vec, vec, vec, jax.ShapeDtypeStruct(conf_t.shape, F32), jax.ShapeDtypeStruct(sc_t.shape, F32),
            jax.ShapeDtypeStruct(pool_t.shape, F32)] + [vec] * 8
    return pl.pallas_call(
        functools.partial(_decode_acd_prep_kernel, start_pos=start_pos),
        in_specs=[_full(x.shape) for x in ins],
        out_specs=[_full(o.shape) for o in outs],
        out_shape=outs,
        compiler_params=_cparams(),
        name="decode_acd_prep",
    )(*ins)


def _decode_wkv_kernel(s_ref, w_ref, kk_ref, b_ref, k_ref, r_ref, v_ref, s_o_ref, y_o_ref):
    s = s_ref[...]
    sa = -jnp.sum(s * kk_ref[...], axis=-1, keepdims=True)
    s_new = s * w_ref[...] + sa * b_ref[...] + v_ref[...] * k_ref[...]
    s_o_ref[...] = s_new
    y_o_ref[...] = jnp.sum(s_new * r_ref[...], axis=-1, keepdims=True)


def _decode_wkv(s, w, kk, bvec, k, r, v, blk):
    bh = s.shape[0]
    lane = pl.BlockSpec((blk, 1, RWKV_HEAD), lambda i: (i, 0, 0))
    col = pl.BlockSpec((blk, RWKV_HEAD, 1), lambda i: (i, 0, 0))
    mat = pl.BlockSpec((blk, RWKV_HEAD, RWKV_HEAD), lambda i: (i, 0, 0))
    return pl.pallas_call(
        _decode_wkv_kernel,
        grid=(bh // blk,),
        in_specs=[mat, lane, lane, lane, lane, lane, col],
        out_specs=[mat, col],
        out_shape=[jax.ShapeDtypeStruct(s.shape, F32), jax.ShapeDtypeStruct((bh, RWKV_HEAD, 1), F32)],
        compiler_params=_cparams("parallel"),
        name="decode_wkv",
    )(s, w, kk, bvec, k, r, v)


def _decode_post_kernel(y_ref, bonus_ref, g_ref, lng_ref, lnb_ref, ones_ref, o_ref):
    o_ref[...] = _rwkv_post(y_ref[...], bonus_ref[...], g_ref[...], lng_ref[...], lnb_ref[...], ones_ref[...])


def _decode_post(y, bonus, g, wl):
    ins = (y, bonus, g, wl["rwkv_ln_g"], wl["rwkv_ln_b"], wl["ones_bd"])
    return pl.pallas_call(
        _decode_post_kernel,
        in_specs=[_full(x.shape) for x in ins],
        out_specs=_full(y.shape),
        out_shape=jax.ShapeDtypeStruct(y.shape, F32),
        compiler_params=_cparams(),
        name="decode_post",
    )(*ins)


_CELLS = [(a, b) for a in range(PEER_TOPK) for b in range(PEER_TOPK) if (a + 1) * (b + 1) <= PEER_TOPK]
_CELL_PAIRS = [(c, d) for c in _CELLS for d in _CELLS if d[0] < c[0] and d[1] > c[1]]
_PAIRS_AS_D = {x: sum(1 for _, d in _CELL_PAIRS if d == x) for x in _CELLS}


def _top16_rows(s, tb, vals_ref, h, tie_safe, want_rank):
    lanes = 128
    iota = lax.broadcasted_iota(jnp.int32, (N_KEYS, lanes), 0).astype(F32)
    ranks, counts = [], []
    for c0 in range(0, tb, lanes):
        sc = s[:, c0:c0 + lanes]
        rank = jnp.full((N_KEYS, lanes), float(PEER_TOPK), F32)
        for r in range(PEER_TOPK):
            m = jnp.max(sc, axis=0, keepdims=True)
            sel = sc == m
            if tie_safe:
                sel = iota == jnp.min(jnp.where(sel, iota, float(N_KEYS)), axis=0, keepdims=True)
            if want_rank:
                rank = jnp.where(sel, float(r), rank)
            sc = jnp.where(sel, -jnp.inf, sc)
            vals_ref[r, h:h + 1, c0:c0 + lanes] = m
        ranks.append(rank)
        counts.append(jnp.sum(jnp.where(sc == -jnp.inf, 1.0, 0.0), axis=0, keepdims=True))
    cat = lambda xs: jnp.concatenate(xs, axis=1) if len(xs) > 1 else xs[0]
    return (cat(ranks) if want_rank else None), cat(counts)


def _peer_route_kernel(h_ref, g_ref, wq_ref, k1_ref, k2_ref,
                       xn_ref, r2_ref, p2_ref, c1_ref, p1_ref,
                       v1_s, v2_s, r1_s, cnt_s, *, tb):
    refs = (h_ref, g_ref, wq_ref, k1_ref, k2_ref, xn_ref, r2_ref, p2_ref, c1_ref, p1_ref, v1_s, v2_s, r1_s, cnt_s)
    n_bad = _peer_route_body(*refs, tb=tb, tie_safe=False)

    @pl.when(jnp.max(n_bad) > 0.0)
    def _():
        _peer_route_body(*refs, tb=tb, tie_safe=True)


def _peer_route_body(h_ref, g_ref, wq_ref, k1_ref, k2_ref,
                     xn_ref, r2_ref, p2_ref, c1_ref, p1_ref,
                     v1_s, v2_s, r1_s, cnt_s, *, tb, tie_safe):
    xn = _rmsnorm(h_ref[...], g_ref[...])
    xb = xn.astype(BF16)
    xn_ref[...] = xn.T.astype(BF16)
    n_bad = jnp.zeros((1, tb), F32)
    for h in range(PEER_HEADS):
        q = jnp.dot(xb, wq_ref[:, h * 256:(h + 1) * 256], preferred_element_type=F32).astype(BF16)
        s1 = _dg(k1_ref[h], q[:, 0:128], _NT)
        s2 = _dg(k2_ref[h], q[:, 128:256], _NT)
        r1, n1 = _top16_rows(s1, tb, v1_s, h, tie_safe, want_rank=tie_safe)
        r2, n2 = _top16_rows(s2, tb, v2_s, h, tie_safe, want_rank=True)
        n_bad = n_bad + jnp.where(n1 != float(PEER_TOPK), 1.0, 0.0) + jnp.where(n2 != float(PEER_TOPK), 1.0, 0.0)
        r1_s[h] = r1 if tie_safe else s1
        r2_ref[h] = r2.astype(BF16)
        p1_ref[h] = jnp.exp(s1 - v1_s[0, h:h + 1, :])
        p2_ref[h] = jnp.exp(s2 - v2_s[0, h:h + 1, :]).astype(BF16)

    sums = {c: v1_s[c[0]] + v2_s[c[1]] for c in _CELLS}
    rank = {c: jnp.full((PEER_HEADS, tb), float((c[0] + 1) * (c[1] + 1) - 1 + _PAIRS_AS_D[c]), F32) for c in _CELLS}
    for c, d in _CELL_PAIRS:
        won = jnp.where(sums[d] >= sums[c], 1.0, 0.0)
        rank[c] = rank[c] + won
        rank[d] = rank[d] - won
    e1 = [jnp.exp(v1_s[a] - v1_s[0]) for a in range(PEER_TOPK)]
    e2 = [jnp.exp(v2_s[b] - v2_s[0]) for b in range(PEER_TOPK)]
    z = jnp.zeros((PEER_HEADS, tb), F32)
    cnt = [jnp.zeros((PEER_HEADS, tb), F32) for _ in range(PEER_TOPK)]
    for c in _CELLS:
        sel = rank[c] < float(PEER_TOPK)
        cnt[c[0]] = cnt[c[0]] + jnp.where(sel, 1.0, 0.0)
        z = z + jnp.where(sel, e1[c[0]] * e2[c[1]], 0.0)
    for a in range(PEER_TOPK):
        cnt_s[a] = cnt[a]
    cnt_s[PEER_TOPK] = 0.5 / z

    for h in range(PEER_HEADS):
        r1 = r1_s[h]
        c1 = jnp.zeros((N_KEYS, tb), F32)
        for a in range(PEER_TOPK):
            hit = r1 == (float(a) if tie_safe else v1_s[a, h:h + 1, :])
            c1 = jnp.where(hit, cnt_s[a, h:h + 1, :], c1)
        c1_ref[h] = c1
        p1_ref[h] = p1_ref[h] * cnt_s[PEER_TOPK, h:h + 1, :]
    return n_bad


def _peer_route(hres, g, wq_bf16, k1_bf16, k2_bf16, tb):
    n, d = hres.shape
    tb = min(tb, n)
    gate = pl.BlockSpec((PEER_HEADS, N_KEYS, tb), lambda i: (0, 0, i))
    gshape = lambda dt: jax.ShapeDtypeStruct((PEER_HEADS, N_KEYS, n), dt)
    return pl.pallas_call(
        functools.partial(_peer_route_kernel, tb=tb),
        grid=(n // tb,),
        in_specs=[pl.BlockSpec((tb, d), lambda i: (i, 0)), _full((1, d)), _full(wq_bf16.shape),
                  _full(k1_bf16.shape), _full(k2_bf16.shape)],
        out_specs=[pl.BlockSpec((d, tb), lambda i: (0, i)), gate, gate, gate, gate],
        out_shape=[jax.ShapeDtypeStruct((d, n), BF16), gshape(BF16), gshape(BF16), gshape(F32), gshape(F32)],
        scratch_shapes=[pltpu.VMEM((PEER_TOPK, PEER_HEADS, tb), F32), pltpu.VMEM((PEER_TOPK, PEER_HEADS, tb), F32),
                        pltpu.VMEM((PEER_HEADS, N_KEYS, tb), F32), pltpu.VMEM((PEER_TOPK + 1, PEER_HEADS, tb), F32)],
        compiler_params=_cparams("parallel"),
        name="peer_route",
    )(hres, g.reshape(1, d), wq_bf16, k1_bf16, k2_bf16)


_SQRT_HALF = float(np.sqrt(0.5))


def _peer_expert_kernel(xn_ref, hres_ref, u_ref, vt_ref, r2_ref, p2_ref, c1_ref, p1_ref, fg_ref, o_ref,
                        acc_s, ht_s, at_s, *, eb, final_norm):
    j = pl.program_id(1)
    per = eb // N_KEYS

    @pl.when(j == 0)
    def _():
        acc_s[...] = jnp.zeros_like(acc_s)

    tb = ht_s.shape[1]
    ht_s[...] = jnp.dot(u_ref[...], xn_ref[...], preferred_element_type=F32)
    for i in range(per):
        i1 = j * per + i

        def row_tile(ref, h):
            row = jnp.broadcast_to(ref[h, pl.ds(i1, 1), :], (16, tb)).astype(BF16)
            return jnp.concatenate([row] * (N_KEYS // 16), axis=0)

        gt = None
        for h in range(PEER_HEADS):
            c1 = row_tile(c1_ref, h)
            p1 = row_tile(p1_ref, h)
            term = jnp.where(r2_ref[h] < c1, p2_ref[h], jnp.zeros((), BF16)) * p1
            gt = term if gt is None else gt + term
        ht = ht_s[i * N_KEYS:(i + 1) * N_KEYS, :]
        act = ht * (1.0 + lax.erf(ht * _SQRT_HALF))
        at_s[i * N_KEYS:(i + 1) * N_KEYS, :] = act.astype(BF16) * gt
    acc_s[...] += jnp.dot(vt_ref[0], at_s[...], preferred_element_type=F32)

    @pl.when(j == pl.num_programs(1) - 1)
    def _():
        res = hres_ref[...] + acc_s[...].T
        o_ref[...] = _rmsnorm(res, fg_ref[...]) if final_norm else res


def _peer_experts(xn, hres, u_bf16, vt_bf16, gates, final_g, final_norm, tb, eb):
    n, d = hres.shape
    tb = min(tb, n)
    nblk = u_bf16.shape[0] // eb
    gate = pl.BlockSpec((PEER_HEADS, N_KEYS, tb), lambda i, j: (0, 0, i))
    return pl.pallas_call(
        functools.partial(_peer_expert_kernel, eb=eb, final_norm=final_norm),
        grid=(n // tb, nblk),
        in_specs=[pl.BlockSpec((d, tb), lambda i, j: (0, i)), pl.BlockSpec((tb, d), lambda i, j: (i, 0)),
                  pl.BlockSpec((eb, d), lambda i, j: (j, 0)),
                  pl.BlockSpec((1, d, eb), lambda i, j: (j, 0, 0)),
                  gate, gate, gate, gate, _full((1, d))],
        out_specs=pl.BlockSpec((tb, d), lambda i, j: (i, 0)),
        out_shape=jax.ShapeDtypeStruct((n, d), F32),
        scratch_shapes=[pltpu.VMEM((d, tb), F32), pltpu.VMEM((eb, tb), F32), pltpu.VMEM((eb, tb), BF16)],
        compiler_params=_cparams("parallel", "arbitrary"),
        name="peer_experts",
    )(xn, hres, u_bf16, vt_bf16, *gates, final_g.reshape(1, d))


def _layer_weights(l, w):
    row = lambda a: a[l].reshape(1, -1)
    perm = _RWKV_PERM
    w_in = w["w_in"][l]
    w_in = jnp.concatenate([w_in[:, 0:512], w_in[:, 512:1408][:, perm], w_in[:, 1408:]], axis=1).astype(BF16)
    eye4 = jnp.eye(RWKV_H, dtype=F32)
    ones_bd = jnp.kron(eye4, jnp.ones((RWKV_HEAD, RWKV_HEAD), F32)).astype(BF16)
    pool_wbd = jax.scipy.linalg.block_diag(*[w["pool_w"][l, gi] for gi in range(4)]).astype(BF16)
    return dict(
        norm1_g=w["norm1_g"][l], norm2_g=w["norm2_g"][l], w_in=w_in,
        conf_dw_w=w["conf_dw_w"][l], conf_dw_b=row(w["conf_dw_b"]), conf_ln_g=row(w["conf_ln_g"]),
        conf_ln_b=row(w["conf_ln_b"]), sc_conv_w=w["sc_conv_w"][l], pool_wbd=pool_wbd, pool_scale=row(w["pool_scale"]),
        rwkv_mu=w["rwkv_mu"][l][perm].reshape(1, -1), rwkv_w0=row(w["rwkv_w0"]), rwkv_w2=w["rwkv_w2"][l].astype(BF16),
        rwkv_a0=row(w["rwkv_a0"]), rwkv_a2=w["rwkv_a2"][l].astype(BF16), rwkv_g2=w["rwkv_g2"][l].astype(BF16),
        rwkv_k_k=row(w["rwkv_k_k"]), rwkv_k_a=row(w["rwkv_k_a"]), rwkv_r_k=row(w["rwkv_r_k"]),
        rwkv_ln_g=row(w["rwkv_ln_g"]), rwkv_ln_b=row(w["rwkv_ln_b"]), ones_bd=ones_bd,
        w_out=w["w_out"][l].astype(BF16), peer_wq=w["peer_wq"][l].astype(BF16),
        peer_k1=w["peer_k1"][l].astype(BF16), peer_k2=w["peer_k2"][l].astype(BF16),
        **dict(zip(("peer_u", "peer_vt"), _table_cast(w["peer_u"], w["peer_v"], l, EXPERT_BLOCK))),
    )


def _peer_block(hres, wl, final_g, final_norm, tb_route, tb_exp, eb):
    xn, r2, p2, c1, p1 = _peer_route(hres, wl["norm2_g"], wl["peer_wq"], wl["peer_k1"], wl["peer_k2"], tb_route)
    return _peer_experts(xn, hres, wl["peer_u"], wl["peer_vt"], (r2, p2, c1, p1), final_g, final_norm, tb_exp, eb)


def _prompt_layer(x, wl, bsz, t, final_g, is_last):
    n = bsz * t
    za, zb, zc, zd = _norm_proj(x, wl["norm1_g"], wl["w_in"], IN_SPLITS, 512)
    z3 = lambda a: a.reshape(bsz, t, -1)
    zeros = lambda *s: jnp.zeros(s, F32)
    ya, yc, yd, conf, sc, pool = _prompt_acd(z3(za), z3(zc), z3(zd), zeros(bsz, 30, GROUP_W), zeros(bsz, 2, GROUP_W),
                                             zeros(bsz, 15, GROUP_W), wl, 256, 0)
    yb, shift, wkv = _prompt_rwkv(z3(zb), zeros(bsz, 1, RWKV_COLS), zeros(bsz, RWKV_H, RWKV_HEAD, RWKV_HEAD), wl,
                                   WKV_SEQS_PER_STEP)
    flat = lambda a: a.reshape(n, GROUP_W)
    hres = _out_proj(x, (flat(ya), flat(yb), flat(yc), flat(yd)), wl["w_out"], 512)
    x = _peer_block(hres, wl, final_g, is_last, 256, 512, EXPERT_BLOCK)
    return x, (conf, shift.reshape(bsz, RWKV_COLS)[:, _RWKV_INV_PERM], wkv, sc, pool)


def _decode_layer(x, states, wl, start_pos, final_g, is_last):
    conf, shift, wkv, sc, pool = states
    n = x.shape[0]
    za, zb, zc, zd = _norm_proj(x, wl["norm1_g"], wl["w_in"], IN_SPLITS, 128)
    tr = lambda a: jnp.transpose(a, (1, 0, 2))
    (ya, yc, yd, conf_n, sc_n, pool_n, r, w, k2, v, kk, bvec, g, bonus) = _decode_acd_prep(
        za, zb, zc, zd, tr(conf), shift[:, _RWKV_PERM], tr(sc), tr(pool), wl, start_pos)
    bh = n * RWKV_H
    lane = lambda a: a.reshape(bh, 1, RWKV_HEAD)
    s_new, y = _decode_wkv(wkv.reshape(bh, RWKV_HEAD, RWKV_HEAD), lane(w), lane(kk), lane(bvec), lane(k2), lane(r),
                           v.reshape(bh, RWKV_HEAD, 1), 64)
    yb = _decode_post(y.reshape(n, GROUP_W), bonus, g, wl)
    hres = _out_proj(x, (ya, yb, yc, yd), wl["w_out"], 128)
    x = _peer_block(hres, wl, final_g, is_last, 128, 128, EXPERT_BLOCK)
    return x, (tr(conf_n), zb[:, _RWKV_INV_PERM], s_new.reshape(n, RWKV_H, RWKV_HEAD, RWKV_HEAD), tr(sc_n), tr(pool_n))


def kernel(x_prompt, x_sample, state_conformer, state_rwkv_shift, state_rwkv_wkv, state_shortconv, state_pool, norm1_g, norm2_g, final_norm_g, w_in, conf_dw_w, conf_dw_b, conf_ln_g, conf_ln_b, rwkv_mu, rwkv_w0, rwkv_w2, rwkv_a0, rwkv_a2, rwkv_g2, rwkv_k_k, rwkv_k_a, rwkv_r_k, rwkv_ln_g, rwkv_ln_b, sc_conv_w, pool_w, pool_scale, w_out, peer_wq, peer_k1, peer_k2, peer_u, peer_v):
    w = dict(norm1_g=norm1_g, norm2_g=norm2_g, w_in=w_in, conf_dw_w=conf_dw_w, conf_dw_b=conf_dw_b,
             conf_ln_g=conf_ln_g, conf_ln_b=conf_ln_b, rwkv_mu=rwkv_mu, rwkv_w0=rwkv_w0, rwkv_w2=rwkv_w2,
             rwkv_a0=rwkv_a0, rwkv_a2=rwkv_a2, rwkv_g2=rwkv_g2, rwkv_k_k=rwkv_k_k, rwkv_k_a=rwkv_k_a,
             rwkv_r_k=rwkv_r_k, rwkv_ln_g=rwkv_ln_g, rwkv_ln_b=rwkv_ln_b, sc_conv_w=sc_conv_w, pool_w=pool_w,
             pool_scale=pool_scale, w_out=w_out, peer_wq=peer_wq, peer_k1=peer_k1, peer_k2=peer_k2,
             peer_u=peer_u, peer_v=peer_v)
    depth = w_in.shape[0]
    bsz, t, d = x_prompt.shape
    nb, dt, _ = x_sample.shape
    past_len = 16384

    xp = x_prompt.reshape(bsz * t, d)
    xs = x_sample.reshape(nb * dt, d)
    p_states, s_states = [], []
    for l in range(depth):
        wl = _layer_weights(l, w)
        last = l == depth - 1
        xp, ps = _prompt_layer(xp, wl, bsz, t, final_norm_g, last)
        xs, ss = _decode_layer(xs, (state_conformer[l], state_rwkv_shift[l], state_rwkv_wkv[l],
                                    state_shortconv[l], state_pool[l]), wl, past_len, final_norm_g, last)
        p_states.append(ps)
        s_states.append(ss)
    y_prompt = xp.reshape(bsz, t, d)
    y_sample = xs.reshape(nb, dt, d)
    stack = lambda lst, i: jnp.stack([s[i] for s in lst], axis=0)
    conf_p, shift_p, wkv_p, sc_p, pool_p = (stack(p_states, i) for i in range(5))
    conf_s, shift_s, wkv_s, sc_s, pool_s = (stack(s_states, i) for i in range(5))
    return (y_prompt, y_sample, conf_p, conf_s, shift_p, shift_s, wkv_p, wkv_s, sc_p, sc_s, pool_p, pool_s)
```

```python
import functools

import jax
import jax.numpy as jnp
import numpy as np
from jax import lax
from jax.experimental import pallas as pl
from jax.experimental.pallas import tpu as pltpu

F32 = jnp.float32
BF16 = jnp.bfloat16

D_MODEL = 1024
GROUP_W = 256
CONF_K = 31
RWKV_HEAD = 64
RWKV_H = 4
RWKV_COLS = 896
RWKV_GN_EPS = 64e-5
POOL_WINDOWS = (2, 4, 8, 16)
POOL_BUF = 15
N_KEYS = 128
PEER_HEADS = 8
PEER_TOPK = 16
RMS_EPS = 1e-6
LN_EPS = 1e-5
IN_SPLITS = (512, 896, 768, 256)

VMEM_LIMIT_BYTES = 56 * 1024 * 1024
WKV_CHUNK = 64
EXPERT_BLOCK = 2048
TABLE_CAST_ROWS = 512
WKV_SEQS_PER_STEP = 8
WKV_PASSES = (1, 1, 1, 1)

_RWKV_PERM = np.concatenate([np.arange(0, 256), np.arange(288, 544), np.arange(544, 800),
                             np.arange(256, 288), np.arange(800, 832), np.arange(832, 896)])
_RWKV_INV_PERM = np.argsort(_RWKV_PERM)


def _cparams(*sem):
    return pltpu.CompilerParams(dimension_semantics=tuple(sem) if sem else None,
                                vmem_limit_bytes=VMEM_LIMIT_BYTES)


def _full(shape):
    n = len(shape)
    return pl.BlockSpec(shape, lambda *_: (0,) * n)


def _split2(x):
    hi = x.astype(BF16)
    lo = (x - hi.astype(F32)).astype(BF16)
    return hi, lo


def _split3(x):
    hi = x.astype(BF16)
    r = x - hi.astype(F32)
    mid = r.astype(BF16)
    lo = (r - mid.astype(F32)).astype(BF16)
    return hi, mid, lo


_NN = (((1,), (0,)), ((), ()))
_NT = (((1,), (1,)), ((), ()))


def _dg(a, b, dims):
    return lax.dot_general(a, b, dims, preferred_element_type=F32)


def _mm3(a, b, dims=_NN):
    ah, al = _split2(a)
    bh, bl = _split2(b)
    return _dg(ah, bh, dims) + (_dg(al, bh, dims) + _dg(ah, bl, dims))


def _mm(a, b, dims=_NN, passes=3):
    if passes == 1:
        return _dg(a.astype(BF16), b.astype(BF16), dims)
    return _mm3(a, b, dims)


def _mm_exact_rhs(a, b_bf16):
    h, m, l = _split3(a)
    return _dg(h, b_bf16, _NN) + (_dg(m, b_bf16, _NN) + _dg(l, b_bf16, _NN))


def _mm_exact_lhs(a_bf16, b):
    h, m, l = _split3(b)
    return _dg(a_bf16, h, _NN) + (_dg(a_bf16, m, _NN) + _dg(a_bf16, l, _NN))


def _transpose_mxu(x, eye_bf16):
    h, m, l = _split3(x)
    return _dg(eye_bf16, h, _NT) + (_dg(eye_bf16, m, _NT) + _dg(eye_bf16, l, _NT))


def _eye(n, dtype):
    return (lax.broadcasted_iota(jnp.int32, (n, n), 0) == lax.broadcasted_iota(jnp.int32, (n, n), 1)).astype(dtype)


def _rmsnorm(x, g):
    ms = jnp.mean(x * x, axis=-1, keepdims=True)
    return x * lax.rsqrt(ms + RMS_EPS) * g


def _sigmoid(x):
    return 1.0 / (1.0 + jnp.exp(-x))


def _softplus(x):
    return jnp.maximum(x, 0.0) + jnp.log(1.0 + jnp.exp(-jnp.abs(x)))


def _norm_proj_kernel(x_ref, g_ref, w_ref, *o_refs, splits):
    xb = _rmsnorm(x_ref[...], g_ref[...]).astype(BF16)
    off = 0
    for o_ref, wd in zip(o_refs, splits):
        o_ref[...] = jnp.dot(xb, w_ref[:, off:off + wd], preferred_element_type=F32)
        off += wd


def _norm_proj(x, g, w_bf16, splits, tm):
    n, d = x.shape
    tm = min(tm, n)
    cols = w_bf16.shape[1]
    return pl.pallas_call(
        functools.partial(_norm_proj_kernel, splits=splits),
        grid=(n // tm,),
        in_specs=[pl.BlockSpec((tm, d), lambda i: (i, 0)), _full((1, d)), _full((d, cols))],
        out_specs=[pl.BlockSpec((tm, wd), lambda i: (i, 0)) for wd in splits],
        out_shape=[jax.ShapeDtypeStruct((n, wd), F32) for wd in splits],
        compiler_params=_cparams("parallel"),
        name="norm_proj",
    )(x, g.reshape(1, d), w_bf16)


def _table_cast_kernel(u_ref, v_ref, ub_ref, vt_ref):
    ub_ref[...] = u_ref[...].astype(BF16)
    vt_ref[0] = v_ref[...].T.astype(BF16)


def _table_cast(u, v, l, eb):
    _, e, d = u.shape
    rows = TABLE_CAST_ROWS
    per = eb // rows
    return pl.pallas_call(
        _table_cast_kernel,
        grid=(e // rows,),
        in_specs=[pl.BlockSpec((None, rows, d), lambda i: (l, i, 0)), pl.BlockSpec((None, rows, d), lambda i: (l, i, 0))],
        out_specs=[pl.BlockSpec((rows, d), lambda i: (i, 0)), pl.BlockSpec((1, d, rows), lambda i: (i // per, 0, i % per))],
        out_shape=[jax.ShapeDtypeStruct((e, d), BF16), jax.ShapeDtypeStruct((e // eb, d, eb), BF16)],
        compiler_params=_cparams("parallel"),
        name="table_cast",
    )(u, v)


def _layernorm_lanes(x, g, b, eps):
    mu = jnp.mean(x, axis=-1, keepdims=True)
    xc = x - mu
    var = jnp.mean(xc * xc, axis=-1, keepdims=True)
    return xc * lax.rsqrt(var + eps) * g + b


def _conformer_tail(ca, lng, lnb):
    y = _layernorm_lanes(ca, lng, lnb, LN_EPS)
    return y * _sigmoid(y)


def _pool_select(w2, w4, w8, w16, shape):
    lane = lax.broadcasted_iota(jnp.int32, shape, len(shape) - 1)
    wsum = jnp.where(lane < 64, w2, jnp.where(lane < 128, w4, jnp.where(lane < 192, w8, w16)))
    win = jnp.where(lane < 64, 2.0, jnp.where(lane < 128, 4.0, jnp.where(lane < 192, 8.0, 16.0)))
    return wsum, win


def _head_sum(x, ones_bd):
    return _mm_exact_rhs(x, ones_bd)


def _rwkv_prep(p, prev, wr):
    (mu, w0, w2, a0, a2, g2, k_k, k_a, r_k, ones_bd) = wr
    xs = p + (prev - p) * mu
    r = xs[:, 0:256]
    k = xs[:, 256:512]
    v = xs[:, 512:768]
    w_lo = xs[:, 768:800]
    a_lo = xs[:, 800:832]
    g_lo = xs[:, 832:896]
    wexp = -_softplus(-(w0 + jnp.dot(jnp.tanh(w_lo).astype(BF16), w2, preferred_element_type=F32))) - 0.5
    logdecay = -jnp.exp(wexp)
    a = _sigmoid(a0 + jnp.dot(a_lo.astype(BF16), a2, preferred_element_type=F32))
    g = jnp.dot(_sigmoid(g_lo).astype(BF16), g2, preferred_element_type=F32)
    kk = k * k_k
    kk = kk * lax.rsqrt(jnp.maximum(_head_sum(kk * kk, ones_bd), 1e-24))
    k2 = k * (1.0 + (a - 1.0) * k_a)
    bonus = _head_sum(r * k2 * r_k, ones_bd) * v
    return r, logdecay, k2, v, kk, a, g, bonus


def _rwkv_post(y, bonus, g, lng, lnb, ones_bd):
    mu = _head_sum(y, ones_bd) * (1.0 / RWKV_HEAD)
    yc = y - mu
    var = _head_sum(yc * yc, ones_bd) * (1.0 / RWKV_HEAD)
    yn = yc * lax.rsqrt(var + RWKV_GN_EPS) * lng + lnb
    return (yn + bonus) * g


_CONV_ROWS = 64


def _prompt_acd_kernel(za_ref, zc_ref, zd_ref, conf0_ref, sc0_ref, pool0_ref,
                       cw_ref, cb_ref, clg_ref, clb_ref, scw_ref, pw_ref, ps_ref,
                       ya_ref, yc_ref, yd_ref, conf_o_ref, sc_o_ref, pool_o_ref,
                       ext_a, ext_c, ext_d, sh_a, *, tt, start_pos):
    t = pl.program_id(1)

    @pl.when(t == 0)
    def _():
        ext_a[0:2, :] = jnp.zeros((2, GROUP_W), F32)
        ext_a[2:32, :] = conf0_ref[0]
        ext_a[tt + 32:tt + 40, :] = jnp.zeros((8, GROUP_W), F32)
        ext_c[0:6, :] = jnp.zeros((6, GROUP_W), F32)
        ext_c[6:8, :] = sc0_ref[0]
        ext_d[0:1, :] = jnp.zeros((1, GROUP_W), F32)
        ext_d[1:16, :] = pool0_ref[0]

    za = za_ref[0]
    ext_a[32:32 + tt, :] = za[:, 0:GROUP_W] * _sigmoid(za[:, GROUP_W:2 * GROUP_W])
    for r in range(1, 8):
        sh_a[r - 1] = ext_a[r:r + tt + 32, :]
    for c in range(tt // _CONV_ROWS):
        base = c * _CONV_ROWS
        acc = jnp.zeros((_CONV_ROWS, GROUP_W), F32) + cb_ref[...]
        for k in range(CONF_K):
            off = base + k + 2
            r = off % 8
            win = (ext_a[off:off + _CONV_ROWS, :] if r == 0
                   else sh_a[r - 1, off - r:off - r + _CONV_ROWS, :])
            acc = acc + cw_ref[k:k + 1, :] * win
        ya_ref[0, base:base + _CONV_ROWS, :] = _conformer_tail(acc, clg_ref[...], clb_ref[...])
    conf_o_ref[0] = ext_a[tt + 2:tt + 32, :]
    ext_a[0:32, :] = ext_a[tt:tt + 32, :]

    zc = zc_ref[0]
    ext_c[8:8 + tt, :] = zc[:, GROUP_W:2 * GROUP_W] * zc[:, 2 * GROUP_W:3 * GROUP_W]
    cc = (scw_ref[0:1, :] * ext_c[6:6 + tt, :] + scw_ref[1:2, :] * ext_c[7:7 + tt, :]
          + scw_ref[2:3, :] * ext_c[8:8 + tt, :])
    yc_ref[0] = zc[:, 0:GROUP_W] * cc
    sc_o_ref[0] = ext_c[tt + 6:tt + 8, :]
    ext_c[0:8, :] = ext_c[tt:tt + 8, :]

    u = zd_ref[0]
    ext_d[16:16 + tt, :] = u
    w2 = u + ext_d[15:15 + tt, :]
    w4 = w2 + ext_d[14:14 + tt, :] + ext_d[13:13 + tt, :]
    w8 = w4
    for j in range(4, 8):
        w8 = w8 + ext_d[16 - j:16 - j + tt, :]
    w16 = w8
    for j in range(8, 16):
        w16 = w16 + ext_d[16 - j:16 - j + tt, :]
    wsum, win = _pool_select(w2, w4, w8, w16, (tt, GROUP_W))
    pos = (lax.broadcasted_iota(jnp.int32, (tt, GROUP_W), 0) + (t * tt + start_pos + 1)).astype(F32)
    pooled = wsum / jnp.minimum(win, pos) - u
    yd_ref[0] = jnp.dot(pooled.astype(BF16), pw_ref[...], preferred_element_type=F32) * ps_ref[...]
    pool_o_ref[0] = ext_d[tt + 1:tt + 16, :]
    ext_d[0:16, :] = ext_d[tt:tt + 16, :]


def _prompt_acd(za, zc, zd, conf0, sc0, pool0, wl, tt, start_pos):
    b, t, _ = za.shape
    tile = lambda w: pl.BlockSpec((1, tt, w), lambda i, j: (i, j, 0))
    st = lambda r: pl.BlockSpec((1, r, GROUP_W), lambda i, j: (i, 0, 0))
    row = _full((1, GROUP_W))
    return pl.pallas_call(
        functools.partial(_prompt_acd_kernel, tt=tt, start_pos=start_pos),
        grid=(b, t // tt),
        in_specs=[tile(512), tile(768), tile(256), st(30), st(2), st(15),
                  _full((CONF_K, GROUP_W)), row, row, row, _full((3, GROUP_W)), _full((GROUP_W, GROUP_W)), row],
        out_specs=[tile(256), tile(256), tile(256), st(30), st(2), st(15)],
        out_shape=[jax.ShapeDtypeStruct((b, t, GROUP_W), F32)] * 3
        + [jax.ShapeDtypeStruct((b, r, GROUP_W), F32) for r in (30, 2, 15)],
        scratch_shapes=[pltpu.VMEM((40 + tt, GROUP_W), F32), pltpu.VMEM((8 + tt, GROUP_W), F32),
                        pltpu.VMEM((16 + tt, GROUP_W), F32), pltpu.VMEM((7, 32 + tt, GROUP_W), F32)],
        compiler_params=_cparams("parallel", "arbitrary"),
        name="prompt_acd",
    )(za, zc, zd, conf0, sc0, pool0, wl["conf_dw_w"], wl["conf_dw_b"], wl["conf_ln_g"], wl["conf_ln_b"],
      wl["sc_conv_w"], wl["pool_wbd"], wl["pool_scale"])


def _prompt_rwkv_kernel(zb_ref, shift0_ref, wkv0_ref, mu_ref, w0_ref, w2_ref, a0_ref, a2_ref, g2_ref,
                        kk_ref, ka_ref, rk_ref, lng_ref, lnb_ref, ones_ref, tri_ref,
                        yb_ref, shift_o_ref, wkv_o_ref, prev_s, st_s, *, c, nb):
    t = pl.program_id(1)
    nt = pl.num_programs(1)

    @pl.when(t == 0)
    def _():
        for bb in range(nb):
            prev_s[bb] = jnp.broadcast_to(shift0_ref[bb], prev_s.shape[1:])
            for h in range(RWKV_H):
                st_s[bb, h] = _transpose_mxu(wkv0_ref[bb, h], _eye(RWKV_HEAD, BF16))

    new_states = _rwkv_chunks(zb_ref, mu_ref, w0_ref, w2_ref, a0_ref, a2_ref, g2_ref, kk_ref, ka_ref, rk_ref,
                              lng_ref, lnb_ref, ones_ref, tri_ref, yb_ref, shift_o_ref, prev_s, st_s, c, nb)

    @pl.when(t == nt - 1)
    def _():
        eye_h = _eye(RWKV_HEAD, BF16)
        for bb in range(nb):
            for h in range(RWKV_H):
                wkv_o_ref[bb, h] = _transpose_mxu(new_states[(bb, h)], eye_h)


def _rwkv_chunks(zb_ref, mu_ref, w0_ref, w2_ref, a0_ref, a2_ref, g2_ref, kk_ref, ka_ref, rk_ref,
                 lng_ref, lnb_ref, ones_ref, tri_ref, yb_ref, shift_o_ref, prev_s, st_s, c, nb):
    p = zb_ref[...].reshape(nb * c, RWKV_COLS)
    row = lax.broadcasted_iota(jnp.int32, p.shape, 0)
    prev = pltpu.roll(p, 1, axis=0)
    for bb in range(nb):
        prev = jnp.where(row == bb * c, prev_s[bb, 0:1, :], prev)
    ones_bd = ones_ref[...]
    wr = (mu_ref[...], w0_ref[...], w2_ref[...], a0_ref[...], a2_ref[...], g2_ref[...],
          kk_ref[...], ka_ref[...], rk_ref[...], ones_bd)
    r, logdecay, k2, v, kk, a, g, bonus = _rwkv_prep(p, prev, wr)

    cum = _mm_exact_lhs(tri_ref[...], logdecay)
    g_end = [cum[(bb + 1) * c - 1:(bb + 1) * c, :] for bb in range(nb)]
    cum_end = jnp.concatenate([jnp.broadcast_to(ge, (c, GROUP_W)) for ge in g_end], axis=0)
    g_end = [jnp.exp(ge) for ge in g_end]
    e_neg = jnp.exp(-cum)
    e_end = jnp.exp(cum_end - cum)
    bvec = kk * a
    a_t = -kk * jnp.exp(cum - logdecay)
    r_t = r * jnp.exp(cum)
    b_h = bvec * e_neg
    k_h = k2 * e_neg
    b_e = bvec * e_end
    k_e = k2 * e_end

    ri = lax.broadcasted_iota(jnp.int32, (2 * c, 2 * c), 0)
    ci = lax.broadcasted_iota(jnp.int32, (2 * c, 2 * c), 1)
    rt, cs = ri & (c - 1), ci & (c - 1)
    keep = (rt > cs) | ((ri >= c) & (rt == cs))
    eye_2h = _eye(2 * RWKV_HEAD, BF16)
    eye_f = _eye(RWKV_HEAD, F32)

    p_nt, p_rhs, p_neu, p_out = WKV_PASSES
    chains = [(bb, h) for bb in range(nb) for h in range(RWKV_H)]

    def part(z, bb, h):
        return z[bb * c:(bb + 1) * c, h * RWKV_HEAD:(h + 1) * RWKV_HEAD]

    big = {ch: jnp.where(keep, _mm(jnp.concatenate([part(a_t, *ch), part(r_t, *ch)], axis=0),
                                   jnp.concatenate([part(b_h, *ch), part(k_h, *ch)], axis=0), _NT, p_nt), 0.0)
           for ch in chains}
    s0 = {ch: st_s[ch[0], ch[1]] for ch in chains}
    vh = {ch: part(v, *ch) for ch in chains}
    sa = {ch: _mm(jnp.concatenate([part(a_t, *ch), big[ch][0:c, c:2 * c]], axis=1),
                  jnp.concatenate([s0[ch], vh[ch]], axis=0), _NN, p_rhs) for ch in chains}
    x = {ch: big[ch][0:c, 0:c] for ch in chains}
    n_sq = int(np.log2(c))
    for step in range(n_sq):
        sa = {ch: sa[ch] + _mm(x[ch], sa[ch], _NN, p_neu) for ch in chains}
        if step + 1 < n_sq:
            x = {ch: _mm(x[ch], x[ch], _NN, p_neu) for ch in chains}
    bk = {ch: jnp.concatenate([part(b_e, *ch), part(k_e, *ch)], axis=1) for ch in chains}
    bk_t = {ch: _dg(eye_2h, bk[ch].astype(BF16), _NT) if p_out == 1 else _transpose_mxu(bk[ch], eye_2h)
            for ch in chains}
    y_h = {ch: _mm(jnp.concatenate([part(r_t, *ch), big[ch][c:2 * c, 0:c], big[ch][c:2 * c, c:2 * c]], axis=1),
                   jnp.concatenate([s0[ch], sa[ch], vh[ch]], axis=0), _NN, p_out) for ch in chains}
    s_new = {}
    for ch in chains:
        bb, h = ch
        g_h = g_end[bb][:, h * RWKV_HEAD:(h + 1) * RWKV_HEAD]
        s_new[ch] = _mm3(eye_f * g_h, s0[ch]) + _mm(
            jnp.concatenate([bk_t[ch][0:RWKV_HEAD], bk_t[ch][RWKV_HEAD:]], axis=1),
            jnp.concatenate([sa[ch], vh[ch]], axis=0), _NN, p_out)
        st_s[bb, h] = s_new[ch]

    y = jnp.concatenate([jnp.concatenate([y_h[(bb, h)] for h in range(RWKV_H)], axis=-1) for bb in range(nb)], axis=0)
    yb = _rwkv_post(y, bonus, g, lng_ref[...], lnb_ref[...], ones_bd)
    for bb in range(nb):
        yb_ref[bb] = yb[bb * c:(bb + 1) * c]
        last = p[(bb + 1) * c - 1:(bb + 1) * c, :]
        prev_s[bb] = jnp.broadcast_to(last, prev_s.shape[1:])
        shift_o_ref[bb] = last
    return s_new


def _rwkv_weight_args(wl):
    return (wl["rwkv_mu"], wl["rwkv_w0"], wl["rwkv_w2"], wl["rwkv_a0"], wl["rwkv_a2"], wl["rwkv_g2"],
            wl["rwkv_k_k"], wl["rwkv_k_a"], wl["rwkv_r_k"], wl["rwkv_ln_g"], wl["rwkv_ln_b"], wl["ones_bd"])


_RWKV_WEIGHT_SPECS = [(1, RWKV_COLS), (1, GROUP_W), (32, GROUP_W), (1, GROUP_W), (32, GROUP_W), (64, GROUP_W),
                      (1, GROUP_W), (1, GROUP_W), (1, GROUP_W), (1, GROUP_W), (1, GROUP_W), (GROUP_W, GROUP_W)]


def _prompt_rwkv(zb, shift0, wkv0, wl, nb):
    b, t, _ = zb.shape
    c = WKV_CHUNK
    nb = min(nb, b)
    tri = jnp.kron(jnp.eye(nb, dtype=F32), jnp.tril(jnp.ones((c, c), F32))).astype(BF16)
    return pl.pallas_call(
        functools.partial(_prompt_rwkv_kernel, c=c, nb=nb),
        grid=(b // nb, t // c),
        in_specs=[pl.BlockSpec((nb, c, RWKV_COLS), lambda i, j: (i, j, 0)),
                  pl.BlockSpec((nb, 1, RWKV_COLS), lambda i, j: (i, 0, 0)),
                  pl.BlockSpec((nb, RWKV_H, RWKV_HEAD, RWKV_HEAD), lambda i, j: (i, 0, 0, 0))]
        + [_full(s) for s in _RWKV_WEIGHT_SPECS] + [_full((nb * c, nb * c))],
        out_specs=[pl.BlockSpec((nb, c, GROUP_W), lambda i, j: (i, j, 0)),
                   pl.BlockSpec((nb, 1, RWKV_COLS), lambda i, j: (i, 0, 0)),
                   pl.BlockSpec((nb, RWKV_H, RWKV_HEAD, RWKV_HEAD), lambda i, j: (i, 0, 0, 0))],
        out_shape=[jax.ShapeDtypeStruct((b, t, GROUP_W), F32), jax.ShapeDtypeStruct((b, 1, RWKV_COLS), F32),
                   jax.ShapeDtypeStruct((b, RWKV_H, RWKV_HEAD, RWKV_HEAD), F32)],
        scratch_shapes=[pltpu.VMEM((nb, 8, RWKV_COLS), F32), pltpu.VMEM((nb, RWKV_H, RWKV_HEAD, RWKV_HEAD), F32)],
        compiler_params=_cparams("parallel", "arbitrary"),
        name="prompt_rwkv",
    )(zb, shift0, wkv0, *_rwkv_weight_args(wl), tri)


def _decode_acd_prep_kernel(za_ref, zb_ref, zc_ref, zd_ref, conf_ref, shift_ref, sc_ref, pool_ref,
                            cw_ref, cb_ref, clg_ref, clb_ref, scw_ref, pw_ref, ps_ref,
                            mu_ref, w0_ref, w2_ref, a0_ref, a2_ref, g2_ref, kk_ref, ka_ref, rk_ref, ones_ref,
                            ya_ref, yc_ref, yd_ref, conf_o_ref, sc_o_ref, pool_o_ref,
                            r_o, w_o, k_o, v_o, kkn_o, b_o, g_o, bonus_o, *, start_pos):
    za = za_ref[...]
    glu = za[:, 0:GROUP_W] * _sigmoid(za[:, GROUP_W:2 * GROUP_W])
    acc = cb_ref[...] + cw_ref[CONF_K - 1:CONF_K, :] * glu
    for k in range(CONF_K - 1):
        acc = acc + cw_ref[k:k + 1, :] * conf_ref[k]
    ya_ref[...] = _conformer_tail(acc, clg_ref[...], clb_ref[...])
    for k in range(CONF_K - 2):
        conf_o_ref[k] = conf_ref[k + 1]
    conf_o_ref[CONF_K - 2] = glu

    zc = zc_ref[...]
    u = zc[:, GROUP_W:2 * GROUP_W] * zc[:, 2 * GROUP_W:3 * GROUP_W]
    cc = scw_ref[0:1, :] * sc_ref[0] + scw_ref[1:2, :] * sc_ref[1] + scw_ref[2:3, :] * u
    yc_ref[...] = zc[:, 0:GROUP_W] * cc
    sc_o_ref[0] = sc_ref[1]
    sc_o_ref[1] = u

    d = zd_ref[...]
    w2 = d + pool_ref[POOL_BUF - 1]
    w4 = w2 + pool_ref[POOL_BUF - 2] + pool_ref[POOL_BUF - 3]
    w8 = w4
    for j in range(4, 8):
        w8 = w8 + pool_ref[POOL_BUF - j]
    w16 = w8
    for j in range(8, 16):
        w16 = w16 + pool_ref[POOL_BUF - j]
    wsum, win = _pool_select(w2, w4, w8, w16, d.shape)
    pooled = wsum / jnp.minimum(win, float(start_pos + 1)) - d
    yd_ref[...] = jnp.dot(pooled.astype(BF16), pw_ref[...], preferred_element_type=F32) * ps_ref[...]
    for k in range(POOL_BUF - 1):
        pool_o_ref[k] = pool_ref[k + 1]
    pool_o_ref[POOL_BUF - 1] = d

    wr = (mu_ref[...], w0_ref[...], w2_ref[...], a0_ref[...], a2_ref[...], g2_ref[...],
          kk_ref[...], ka_ref[...], rk_ref[...], ones_ref[...])
    r, logdecay, k2, v, kk, a, g, bonus = _rwkv_prep(zb_ref[...], shift_ref[...], wr)
    r_o[...] = r
    w_o[...] = jnp.exp(logdecay)
    k_o[...] = k2
    v_o[...] = v
    kkn_o[...] = kk
    b_o[...] = kk * a
    g_o[...] = g
    bonus_o[...] = bonus


def _decode_acd_prep(za, zb, zc, zd, conf_t, shift, sc_t, pool_t, wl, start_pos):
    n = za.shape[0]
    ins = (za, zb, zc, zd, conf_t, shift, sc_t, pool_t, wl["conf_dw_w"], wl["conf_dw_b"], wl["conf_ln_g"],
           wl["conf_ln_b"], wl["sc_conv_w"], wl["pool_wbd"], wl["pool_scale"]) + _rwkv_weight_args(wl)[:9] + (wl["ones_bd"],)
    vec = jax.ShapeDtypeStruct((n, GROUP_W), F32)
    outs = [vec, vec, vec, jax.ShapeDtypeStruct(conf_t.shape, F32), jax.ShapeDtypeStruct(sc_t.shape, F32),
            jax.ShapeDtypeStruct(pool_t.shape, F32)] + [vec] * 8
    return pl.pallas_call(
        functools.partial(_decode_acd_prep_kernel, start_pos=start_pos),
        in_specs=[_full(x.shape) for x in ins],
        out_specs=[_full(o.shape) for o in outs],
        out_shape=outs,
        compiler_params=_cparams(),
        name="decode_acd_prep",
    )(*ins)


def _decode_wkv_kernel(s_ref, w_ref, kk_ref, b_ref, k_ref, r_ref, v_ref, s_o_ref, y_o_ref):
    s = s_ref[...]
    sa = -jnp.sum(s * kk_ref[...], axis=-1, keepdims=True)
    s_new = s * w_ref[...] + sa * b_ref[...] + v_ref[...] * k_ref[...]
    s_o_ref[...] = s_new
    y_o_ref[...] = jnp.sum(s_new * r_ref[...], axis=-1, keepdims=True)


def _decode_wkv(s, w, kk, bvec, k, r, v, blk):
    bh = s.shape[0]
    lane = pl.BlockSpec((blk, 1, RWKV_HEAD), lambda i: (i, 0, 0))
    col = pl.BlockSpec((blk, RWKV_HEAD, 1), lambda i: (i, 0, 0))
    mat = pl.BlockSpec((blk, RWKV_HEAD, RWKV_HEAD), lambda i: (i, 0, 0))
    return pl.pallas_call(
        _decode_wkv_kernel,
        grid=(bh // blk,),
        in_specs=[mat, lane, lane, lane, lane, lane, col],
        out_specs=[mat, col],
        out_shape=[jax.ShapeDtypeStruct(s.shape, F32), jax.ShapeDtypeStruct((bh, RWKV_HEAD, 1), F32)],
        compiler_params=_cparams("parallel"),
        name="decode_wkv",
    )(s, w, kk, bvec, k, r, v)


def _decode_post_kernel(y_ref, bonus_ref, g_ref, lng_ref, lnb_ref, ones_ref, o_ref):
    o_ref[...] = _rwkv_post(y_ref[...], bonus_ref[...], g_ref[...], lng_ref[...], lnb_ref[...], ones_ref[...])


def _decode_post(y, bonus, g, wl):
    ins = (y, bonus, g, wl["rwkv_ln_g"], wl["rwkv_ln_b"], wl["ones_bd"])
    return pl.pallas_call(
        _decode_post_kernel,
        in_specs=[_full(x.shape) for x in ins],
        out_specs=_full(y.shape),
        out_shape=jax.ShapeDtypeStruct(y.shape, F32),
        compiler_params=_cparams(),
        name="decode_post",
    )(*ins)


_CELLS = [(a, b) for a in range(PEER_TOPK) for b in range(PEER_TOPK) if (a + 1) * (b + 1) <= PEER_TOPK]
_CELL_PAIRS = [(c, d) for c in _CELLS for d in _CELLS if d[0] < c[0] and d[1] > c[1]]
_PAIRS_AS_D = {x: sum(1 for _, d in _CELL_PAIRS if d == x) for x in _CELLS}


def _top16_rows(s, tb, vals_ref, h, tie_safe, want_rank):
    lanes = 128
    iota = lax.broadcasted_iota(jnp.int32, (N_KEYS, lanes), 0).astype(F32)
    ranks, counts = [], []
    for c0 in range(0, tb, lanes):
        sc = s[:, c0:c0 + lanes]
        rank = jnp.full((N_KEYS, lanes), float(PEER_TOPK), F32)
        for r in range(PEER_TOPK):
            m = jnp.max(sc, axis=0, keepdims=True)
            sel = sc == m
            if tie_safe:
                sel = iota == jnp.min(jnp.where(sel, iota, float(N_KEYS)), axis=0, keepdims=True)
            if want_rank:
                rank = jnp.where(sel, float(r), rank)
            sc = jnp.where(sel, -jnp.inf, sc)
            vals_ref[r, h:h + 1, c0:c0 + lanes] = m
        ranks.append(rank)
        counts.append(jnp.sum(jnp.where(sc == -jnp.inf, 1.0, 0.0), axis=0, keepdims=True))
    cat = lambda xs: jnp.concatenate(xs, axis=1) if len(xs) > 1 else xs[0]
    return (cat(ranks) if want_rank else None), cat(counts)


def _top16_values(s, tb, vals_ref, h):
    lanes, nv = 128, N_KEYS // 8
    ties = []
    for c0 in range(0, tb, lanes):
        v = [s[8 * k:8 * (k + 1), c0:c0 + lanes] for k in range(nv)]
        k = 2
        while k <= nv:
            j = k // 2
            while j >= 1:
                for i in range(nv):
                    l = i ^ j
                    if l > i:
                        hi, lo = jnp.maximum(v[i], v[l]), jnp.minimum(v[i], v[l])
                        v[i], v[l] = (hi, lo) if (i & k) == 0 else (lo, hi)
                j //= 2
            k *= 2
        tie = jnp.zeros((1, lanes), F32)
        m = None
        for r in range(PEER_TOPK + 1):
            prev = m
            m = jnp.max(v[0], axis=0, keepdims=True)
            if r > 0:
                tie = tie + jnp.where(m == prev, 1.0, 0.0)
            if r == PEER_TOPK:
                break
            vals_ref[r, h:h + 1, c0:c0 + lanes] = m
            hit = v[0] == m
            tie = tie + jnp.where(jnp.sum(jnp.where(hit, 1.0, 0.0), axis=0, keepdims=True) > 1.0, 1.0, 0.0)
            depth = PEER_TOPK - r
            for d in range(depth):
                v[d] = jnp.where(hit, v[d + 1] if d + 1 < nv else -jnp.inf, v[d])
        ties.append(tie)
    return jnp.concatenate(ties, axis=1) if len(ties) > 1 else ties[0]


def _peer_route_kernel(*refs, tb):
    n_bad = _peer_route_body(*refs, tb=tb, tie_safe=False)

    @pl.when(jnp.max(n_bad) > 0.0)
    def _():
        _peer_route_body(*refs, tb=tb, tie_safe=True)


def _peer_route_body(x_ref, ya_ref, yb_ref, yc_ref, yd_ref, wo_ref, g_ref, wq_ref, k1_ref, k2_ref,
                     hres_ref, xn_ref, r2_ref, p2_ref, c1_ref, p1_ref,
                     v1_s, v2_s, r1_s, cnt_s, *, tb, tie_safe):
    hres = x_ref[...]
    for i, y_ref in enumerate((ya_ref, yb_ref, yc_ref, yd_ref)):
        hres = hres + jnp.dot(y_ref[...].astype(BF16), wo_ref[i * GROUP_W:(i + 1) * GROUP_W, :],
                              preferred_element_type=F32)
    hres_ref[...] = hres
    xn = _rmsnorm(hres, g_ref[...])
    xb = xn.astype(BF16)
    xn_ref[...] = xn.T.astype(BF16)
    n_bad = jnp.zeros((1, tb), F32)
    for h in range(PEER_HEADS):
        q = jnp.dot(xb, wq_ref[:, h * 256:(h + 1) * 256], preferred_element_type=F32).astype(BF16)
        s1 = _dg(k1_ref[h], q[:, 0:128], _NT)
        s2 = _dg(k2_ref[h], q[:, 128:256], _NT)
        if tie_safe:
            r1, _ = _top16_rows(s1, tb, v1_s, h, True, want_rank=True)
        else:
            r1 = s1
            n_bad = n_bad + _top16_values(s1, tb, v1_s, h)
        r2, n2 = _top16_rows(s2, tb, v2_s, h, tie_safe, want_rank=True)
        n_bad = n_bad + jnp.where(n2 != float(PEER_TOPK), 1.0, 0.0)
        r1_s[h] = r1
        r2_ref[h] = r2.astype(BF16)
        p1_ref[h] = jnp.exp(s1 - v1_s[0, h:h + 1, :])
        p2_ref[h] = jnp.exp(s2 - v2_s[0, h:h + 1, :]).astype(BF16)

    sums = {c: v1_s[c[0]] + v2_s[c[1]] for c in _CELLS}
    rank = {c: jnp.full((PEER_HEADS, tb), float((c[0] + 1) * (c[1] + 1) - 1 + _PAIRS_AS_D[c]), F32) for c in _CELLS}
    for c, d in _CELL_PAIRS:
        won = jnp.where(sums[d] >= sums[c], 1.0, 0.0)
        rank[c] = rank[c] + won
        rank[d] = rank[d] - won
    e1 = [jnp.exp(v1_s[a] - v1_s[0]) for a in range(PEER_TOPK)]
    e2 = [jnp.exp(v2_s[b] - v2_s[0]) for b in range(PEER_TOPK)]
    z = jnp.zeros((PEER_HEADS, tb), F32)
    cnt = [jnp.zeros((PEER_HEADS, tb), F32) for _ in range(PEER_TOPK)]
    for c in _CELLS:
        sel = rank[c] < float(PEER_TOPK)
        cnt[c[0]] = cnt[c[0]] + jnp.where(sel, 1.0, 0.0)
        z = z + jnp.where(sel, e1[c[0]] * e2[c[1]], 0.0)
    for a in range(PEER_TOPK):
        cnt_s[a] = cnt[a]
    cnt_s[PEER_TOPK] = 0.5 / z

    for h in range(PEER_HEADS):
        r1 = r1_s[h]
        c1 = jnp.zeros((N_KEYS, tb), F32)
        for a in range(PEER_TOPK):
            hit = r1 == (float(a) if tie_safe else v1_s[a, h:h + 1, :])
            c1 = jnp.where(hit, cnt_s[a, h:h + 1, :], c1)
        c1_ref[h] = c1
        p1_ref[h] = p1_ref[h] * cnt_s[PEER_TOPK, h:h + 1, :]
    return n_bad


def _peer_route(x, ys, wo_bf16, g, wq_bf16, k1_bf16, k2_bf16, tb):
    n, d = x.shape
    tb = min(tb, n)
    row = lambda i: (i, 0)
    gate = pl.BlockSpec((PEER_HEADS, N_KEYS, tb), lambda i: (0, 0, i))
    gshape = lambda dt: jax.ShapeDtypeStruct((PEER_HEADS, N_KEYS, n), dt)
    return pl.pallas_call(
        functools.partial(_peer_route_kernel, tb=tb),
        grid=(n // tb,),
        in_specs=[pl.BlockSpec((tb, d), row)] + [pl.BlockSpec((tb, GROUP_W), row)] * 4
        + [_full((d, d)), _full((1, d)), _full(wq_bf16.shape), _full(k1_bf16.shape), _full(k2_bf16.shape)],
        out_specs=[pl.BlockSpec((tb, d), row), pl.BlockSpec((d, tb), lambda i: (0, i)), gate, gate, gate, gate],
        out_shape=[jax.ShapeDtypeStruct((n, d), F32), jax.ShapeDtypeStruct((d, n), BF16),
                   gshape(BF16), gshape(BF16), gshape(F32), gshape(F32)],
        scratch_shapes=[pltpu.VMEM((PEER_TOPK, PEER_HEADS, tb), F32), pltpu.VMEM((PEER_TOPK, PEER_HEADS, tb), F32),
                        pltpu.VMEM((PEER_HEADS, N_KEYS, tb), F32), pltpu.VMEM((PEER_TOPK + 1, PEER_HEADS, tb), F32)],
        compiler_params=_cparams("parallel"),
        name="peer_route",
    )(x, *ys, wo_bf16, g.reshape(1, d), wq_bf16, k1_bf16, k2_bf16)


_SQRT_HALF = float(np.sqrt(0.5))


def _peer_expert_kernel(xn_ref, hres_ref, u_ref, vt_ref, r2_ref, p2_ref, c1_ref, p1_ref, fg_ref, o_ref,
                        acc_s, ht_s, at_s, *, eb, final_norm):
    j = pl.program_id(1)
    per = eb // N_KEYS

    @pl.when(j == 0)
    def _():
        acc_s[...] = jnp.zeros_like(acc_s)

    tb = ht_s.shape[1]
    ht_s[...] = jnp.dot(u_ref[...], xn_ref[...], preferred_element_type=F32)
    for i in range(per):
        i1 = j * per + i

        def row_tile(ref, h):
            row = jnp.broadcast_to(ref[h, pl.ds(i1, 1), :], (16, tb)).astype(BF16)
            return jnp.concatenate([row] * (N_KEYS // 16), axis=0)

        gt = None
        for h in range(PEER_HEADS):
            c1 = row_tile(c1_ref, h)
            p1 = row_tile(p1_ref, h)
            term = jnp.where(r2_ref[h] < c1, p2_ref[h], jnp.zeros((), BF16)) * p1
            gt = term if gt is None else gt + term
        ht = ht_s[i * N_KEYS:(i + 1) * N_KEYS, :]
        act = ht * (1.0 + lax.erf(ht * _SQRT_HALF))
        at_s[i * N_KEYS:(i + 1) * N_KEYS, :] = act.astype(BF16) * gt
    acc_s[...] += jnp.dot(vt_ref[0], at_s[...], preferred_element_type=F32)

    @pl.when(j == pl.num_programs(1) - 1)
    def _():
        res = hres_ref[...] + acc_s[...].T
        o_ref[...] = _rmsnorm(res, fg_ref[...]) if final_norm else res


def _peer_experts(xn, hres, u_bf16, vt_bf16, gates, final_g, final_norm, tb, eb):
    n, d = hres.shape
    tb = min(tb, n)
    nblk = u_bf16.shape[0] // eb
    gate = pl.BlockSpec((PEER_HEADS, N_KEYS, tb), lambda i, j: (0, 0, i))
    return pl.pallas_call(
        functools.partial(_peer_expert_kernel, eb=eb, final_norm=final_norm),
        grid=(n // tb, nblk),
        in_specs=[pl.BlockSpec((d, tb), lambda i, j: (0, i)), pl.BlockSpec((tb, d), lambda i, j: (i, 0)),
                  pl.BlockSpec((eb, d), lambda i, j: (j, 0)),
                  pl.BlockSpec((1, d, eb), lambda i, j: (j, 0, 0)),
                  gate, gate, gate, gate, _full((1, d))],
        out_specs=pl.BlockSpec((tb, d), lambda i, j: (i, 0)),
        out_shape=jax.ShapeDtypeStruct((n, d), F32),
        scratch_shapes=[pltpu.VMEM((d, tb), F32), pltpu.VMEM((eb, tb), F32), pltpu.VMEM((eb, tb), BF16)],
        compiler_params=_cparams("parallel", "arbitrary"),
        name="peer_experts",
    )(xn, hres, u_bf16, vt_bf16, *gates, final_g.reshape(1, d))


def _layer_weights(l, w):
    row = lambda a: a[l].reshape(1, -1)
    perm = _RWKV_PERM
    w_in = w["w_in"][l]
    w_in = jnp.concatenate([w_in[:, 0:512], w_in[:, 512:1408][:, perm], w_in[:, 1408:]], axis=1).astype(BF16)
    eye4 = jnp.eye(RWKV_H, dtype=F32)
    ones_bd = jnp.kron(eye4, jnp.ones((RWKV_HEAD, RWKV_HEAD), F32)).astype(BF16)
    pool_wbd = jax.scipy.linalg.block_diag(*[w["pool_w"][l, gi] for gi in range(4)]).astype(BF16)
    return dict(
        norm1_g=w["norm1_g"][l], norm2_g=w["norm2_g"][l], w_in=w_in,
        conf_dw_w=w["conf_dw_w"][l], conf_dw_b=row(w["conf_dw_b"]), conf_ln_g=row(w["conf_ln_g"]),
        conf_ln_b=row(w["conf_ln_b"]), sc_conv_w=w["sc_conv_w"][l], pool_wbd=pool_wbd, pool_scale=row(w["pool_scale"]),
        rwkv_mu=w["rwkv_mu"][l][perm].reshape(1, -1), rwkv_w0=row(w["rwkv_w0"]), rwkv_w2=w["rwkv_w2"][l].astype(BF16),
        rwkv_a0=row(w["rwkv_a0"]), rwkv_a2=w["rwkv_a2"][l].astype(BF16), rwkv_g2=w["rwkv_g2"][l].astype(BF16),
        rwkv_k_k=row(w["rwkv_k_k"]), rwkv_k_a=row(w["rwkv_k_a"]), rwkv_r_k=row(w["rwkv_r_k"]),
        rwkv_ln_g=row(w["rwkv_ln_g"]), rwkv_ln_b=row(w["rwkv_ln_b"]), ones_bd=ones_bd,
        w_out=w["w_out"][l].astype(BF16), peer_wq=w["peer_wq"][l].astype(BF16),
        peer_k1=w["peer_k1"][l].astype(BF16), peer_k2=w["peer_k2"][l].astype(BF16),
        **dict(zip(("peer_u", "peer_vt"), _table_cast(w["peer_u"], w["peer_v"], l, EXPERT_BLOCK))),
    )


def _peer_block(x, ys, wl, final_g, final_norm, tb_route, tb_exp, eb):
    hres, xn, r2, p2, c1, p1 = _peer_route(x, ys, wl["w_out"], wl["norm2_g"], wl["peer_wq"], wl["peer_k1"],
                                           wl["peer_k2"], tb_route)
    return _peer_experts(xn, hres, wl["peer_u"], wl["peer_vt"], (r2, p2, c1, p1), final_g, final_norm, tb_exp, eb)


def _prompt_layer(x, wl, bsz, t, final_g, is_last):
    n = bsz * t
    za, zb, zc, zd = _norm_proj(x, wl["norm1_g"], wl["w_in"], IN_SPLITS, 512)
    z3 = lambda a: a.reshape(bsz, t, -1)
    zeros = lambda *s: jnp.zeros(s, F32)
    ya, yc, yd, conf, sc, pool = _prompt_acd(z3(za), z3(zc), z3(zd), zeros(bsz, 30, GROUP_W), zeros(bsz, 2, GROUP_W),
                                             zeros(bsz, 15, GROUP_W), wl, 256, 0)
    yb, shift, wkv = _prompt_rwkv(z3(zb), zeros(bsz, 1, RWKV_COLS), zeros(bsz, RWKV_H, RWKV_HEAD, RWKV_HEAD), wl,
                                   WKV_SEQS_PER_STEP)
    flat = lambda a: a.reshape(n, GROUP_W)
    x = _peer_block(x, (flat(ya), flat(yb), flat(yc), flat(yd)), wl, final_g, is_last, 256, 512, EXPERT_BLOCK)
    return x, (conf, shift.reshape(bsz, RWKV_COLS)[:, _RWKV_INV_PERM], wkv, sc, pool)


def _decode_layer(x, states, wl, start_pos, final_g, is_last):
    conf, shift, wkv, sc, pool = states
    n = x.shape[0]
    za, zb, zc, zd = _norm_proj(x, wl["norm1_g"], wl["w_in"], IN_SPLITS, 128)
    tr = lambda a: jnp.transpose(a, (1, 0, 2))
    (ya, yc, yd, conf_n, sc_n, pool_n, r, w, k2, v, kk, bvec, g, bonus) = _decode_acd_prep(
        za, zb, zc, zd, tr(conf), shift[:, _RWKV_PERM], tr(sc), tr(pool), wl, start_pos)
    bh = n * RWKV_H
    lane = lambda a: a.reshape(bh, 1, RWKV_HEAD)
    s_new, y = _decode_wkv(wkv.reshape(bh, RWKV_HEAD, RWKV_HEAD), lane(w), lane(kk), lane(bvec), lane(k2), lane(r),
                           v.reshape(bh, RWKV_HEAD, 1), 64)
    yb = _decode_post(y.reshape(n, GROUP_W), bonus, g, wl)
    x = _peer_block(x, (ya, yb, yc, yd), wl, final_g, is_last, 128, 128, EXPERT_BLOCK)
    return x, (tr(conf_n), zb[:, _RWKV_INV_PERM], s_new.reshape(n, RWKV_H, RWKV_HEAD, RWKV_HEAD), tr(sc_n), tr(pool_n))


def kernel(x_prompt, x_sample, state_conformer, state_rwkv_shift, state_rwkv_wkv, state_shortconv, state_pool, norm1_g, norm2_g, final_norm_g, w_in, conf_dw_w, conf_dw_b, conf_ln_g, conf_ln_b, rwkv_mu, rwkv_w0, rwkv_w2, rwkv_a0, rwkv_a2, rwkv_g2, rwkv_k_k, rwkv_k_a, rwkv_r_k, rwkv_ln_g, rwkv_ln_b, sc_conv_w, pool_w, pool_scale, w_out, peer_wq, peer_k1, peer_k2, peer_u, peer_v):
    w = dict(norm1_g=norm1_g, norm2_g=norm2_g, w_in=w_in, conf_dw_w=conf_dw_w, conf_dw_b=conf_dw_b,
             conf_ln_g=conf_ln_g, conf_ln_b=conf_ln_b, rwkv_mu=rwkv_mu, rwkv_w0=rwkv_w0, rwkv_w2=rwkv_w2,
             rwkv_a0=rwkv_a0, rwkv_a2=rwkv_a2, rwkv_g2=rwkv_g2, rwkv_k_k=rwkv_k_k, rwkv_k_a=rwkv_k_a,
             rwkv_r_k=rwkv_r_k, rwkv_ln_g=rwkv_ln_g, rwkv_ln_b=rwkv_ln_b, sc_conv_w=sc_conv_w, pool_w=pool_w,
             pool_scale=pool_scale, w_out=w_out, peer_wq=peer_wq, peer_k1=peer_k1, peer_k2=peer_k2,
             peer_u=peer_u, peer_v=peer_v)
    depth = w_in.shape[0]
    bsz, t, d = x_prompt.shape
    nb, dt, _ = x_sample.shape
    past_len = 16384

    xp = x_prompt.reshape(bsz * t, d)
    xs = x_sample.reshape(nb * dt, d)
    p_states, s_states = [], []
    for l in range(depth):
        wl = _layer_weights(l, w)
        last = l == depth - 1
        xp, ps = _prompt_layer(xp, wl, bsz, t, final_norm_g, last)
        xs, ss = _decode_layer(xs, (state_conformer[l], state_rwkv_shift[l], state_rwkv_wkv[l],
                                    state_shortconv[l], state_pool[l]), wl, past_len, final_norm_g, last)
        p_states.append(ps)
        s_states.append(ss)
    y_prompt = xp.reshape(bsz, t, d)
    y_sample = xs.reshape(nb, dt, d)
    stack = lambda lst, i: jnp.stack([s[i] for s in lst], axis=0)
    conf_p, shift_p, wkv_p, sc_p, pool_p = (stack(p_states, i) for i in range(5))
    conf_s, shift_s, wkv_s, sc_s, pool_s = (stack(s_states, i) for i in range(5))
    return (y_prompt, y_sample, conf_p, conf_s, shift_p, shift_s, wkv_p, wkv_s, sc_p, sc_s, pool_p, pool_s)
```

```python
import functools

import jax
import jax.numpy as jnp
import numpy as np
from jax import lax
from jax.experimental import pallas as pl
from jax.experimental.pallas import tpu as pltpu

F32 = jnp.float32
BF16 = jnp.bfloat16

D_MODEL = 1024
GROUP_W = 256
CONF_K = 31
RWKV_HEAD = 64
RWKV_H = 4
RWKV_COLS = 896
RWKV_GN_EPS = 64e-5
POOL_WINDOWS = (2, 4, 8, 16)
POOL_BUF = 15
N_KEYS = 128
PEER_HEADS = 8
PEER_TOPK = 16
RMS_EPS = 1e-6
LN_EPS = 1e-5
IN_SPLITS = (512, 896, 768, 256)

VMEM_LIMIT_BYTES = 56 * 1024 * 1024
WKV_CHUNK = 64
EXPERT_BLOCK = 2048
TABLE_CAST_ROWS = 512
WKV_SEQS_PER_STEP = 8
PROJ_ROWS = 512
MIXER_ROWS = 256
ROUTE_TOKENS = 256
EXPERT_TOKENS = 512
DECODE_TOKENS = 128
DECODE_WKV_BLOCK = 64
PAST_LEN = 16384

_RWKV_PERM = np.concatenate([np.arange(0, 256), np.arange(288, 544), np.arange(544, 800),
                             np.arange(256, 288), np.arange(800, 832), np.arange(832, 896)])
_RWKV_INV_PERM = np.argsort(_RWKV_PERM)


def _cparams(*sem):
    return pltpu.CompilerParams(dimension_semantics=tuple(sem) if sem else None,
                                vmem_limit_bytes=VMEM_LIMIT_BYTES)


def _full(shape):
    n = len(shape)
    return pl.BlockSpec(shape, lambda *_: (0,) * n)


def _split2(x):
    hi = x.astype(BF16)
    lo = (x - hi.astype(F32)).astype(BF16)
    return hi, lo


def _split3(x):
    hi = x.astype(BF16)
    r = x - hi.astype(F32)
    mid = r.astype(BF16)
    lo = (r - mid.astype(F32)).astype(BF16)
    return hi, mid, lo


_NN = (((1,), (0,)), ((), ()))
_NT = (((1,), (1,)), ((), ()))


def _dg(a, b, dims):
    return lax.dot_general(a, b, dims, preferred_element_type=F32)


def _mm3(a, b, dims=_NN):
    ah, al = _split2(a)
    bh, bl = _split2(b)
    return _dg(ah, bh, dims) + (_dg(al, bh, dims) + _dg(ah, bl, dims))


def _mm1(a, b, dims=_NN):
    return _dg(a.astype(BF16), b.astype(BF16), dims)


def _mm_exact_rhs(a, b_bf16):
    h, m, l = _split3(a)
    return _dg(h, b_bf16, _NN) + (_dg(m, b_bf16, _NN) + _dg(l, b_bf16, _NN))


def _mm_exact_lhs(a_bf16, b):
    h, m, l = _split3(b)
    return _dg(a_bf16, h, _NN) + (_dg(a_bf16, m, _NN) + _dg(a_bf16, l, _NN))


def _transpose_mxu(x, eye_bf16):
    h, m, l = _split3(x)
    return _dg(eye_bf16, h, _NT) + (_dg(eye_bf16, m, _NT) + _dg(eye_bf16, l, _NT))


def _eye(n, dtype):
    return (lax.broadcasted_iota(jnp.int32, (n, n), 0) == lax.broadcasted_iota(jnp.int32, (n, n), 1)).astype(dtype)


def _rmsnorm(x, g):
    ms = jnp.mean(x * x, axis=-1, keepdims=True)
    return x * lax.rsqrt(ms + RMS_EPS) * g


def _sigmoid(x):
    return 1.0 / (1.0 + jnp.exp(-x))


def _softplus(x):
    return jnp.maximum(x, 0.0) + jnp.log(1.0 + jnp.exp(-jnp.abs(x)))


def _norm_proj_kernel(x_ref, g_ref, w_ref, *o_refs, splits):
    xb = _rmsnorm(x_ref[...], g_ref[...]).astype(BF16)
    off = 0
    for o_ref, wd in zip(o_refs, splits):
        o_ref[...] = jnp.dot(xb, w_ref[:, off:off + wd], preferred_element_type=F32)
        off += wd


def _norm_proj(x, g, w_bf16, splits, tm):
    n, d = x.shape
    tm = min(tm, n)
    cols = w_bf16.shape[1]
    return pl.pallas_call(
        functools.partial(_norm_proj_kernel, splits=splits),
        grid=(n // tm,),
        in_specs=[pl.BlockSpec((tm, d), lambda i: (i, 0)), _full((1, d)), _full((d, cols))],
        out_specs=[pl.BlockSpec((tm, wd), lambda i: (i, 0)) for wd in splits],
        out_shape=[jax.ShapeDtypeStruct((n, wd), F32) for wd in splits],
        compiler_params=_cparams("parallel"),
        name="norm_proj",
    )(x, g.reshape(1, d), w_bf16)


def _table_cast_kernel(u_ref, v_ref, ub_ref, vt_ref):
    ub_ref[...] = u_ref[...].astype(BF16)
    vt_ref[0] = v_ref[...].T.astype(BF16)


def _table_cast(u, v, l, eb):
    _, e, d = u.shape
    rows = TABLE_CAST_ROWS
    per = eb // rows
    return pl.pallas_call(
        _table_cast_kernel,
        grid=(e // rows,),
        in_specs=[pl.BlockSpec((None, rows, d), lambda i: (l, i, 0)), pl.BlockSpec((None, rows, d), lambda i: (l, i, 0))],
        out_specs=[pl.BlockSpec((rows, d), lambda i: (i, 0)), pl.BlockSpec((1, d, rows), lambda i: (i // per, 0, i % per))],
        out_shape=[jax.ShapeDtypeStruct((e, d), BF16), jax.ShapeDtypeStruct((e // eb, d, eb), BF16)],
        compiler_params=_cparams("parallel"),
        name="table_cast",
    )(u, v)


def _layernorm_lanes(x, g, b, eps):
    mu = jnp.mean(x, axis=-1, keepdims=True)
    xc = x - mu
    var = jnp.mean(xc * xc, axis=-1, keepdims=True)
    return xc * lax.rsqrt(var + eps) * g + b


def _conformer_tail(ca, lng, lnb):
    y = _layernorm_lanes(ca, lng, lnb, LN_EPS)
    return y * _sigmoid(y)


def _pool_select(w2, w4, w8, w16, shape):
    lane = lax.broadcasted_iota(jnp.int32, shape, len(shape) - 1)
    wsum = jnp.where(lane < 64, w2, jnp.where(lane < 128, w4, jnp.where(lane < 192, w8, w16)))
    win = jnp.where(lane < 64, 2.0, jnp.where(lane < 128, 4.0, jnp.where(lane < 192, 8.0, 16.0)))
    return wsum, win


def _head_sum(x, ones_bd):
    return _mm_exact_rhs(x, ones_bd)


def _rwkv_prep(p, prev, wr):
    (mu, w0, w2, a0, a2, g2, k_k, k_a, r_k, ones_bd) = wr
    xs = p + (prev - p) * mu
    r = xs[:, 0:256]
    k = xs[:, 256:512]
    v = xs[:, 512:768]
    w_lo = xs[:, 768:800]
    a_lo = xs[:, 800:832]
    g_lo = xs[:, 832:896]
    wexp = -_softplus(-(w0 + jnp.dot(jnp.tanh(w_lo).astype(BF16), w2, preferred_element_type=F32))) - 0.5
    logdecay = -jnp.exp(wexp)
    a = _sigmoid(a0 + jnp.dot(a_lo.astype(BF16), a2, preferred_element_type=F32))
    g = jnp.dot(_sigmoid(g_lo).astype(BF16), g2, preferred_element_type=F32)
    kk = k * k_k
    kk = kk * lax.rsqrt(jnp.maximum(_head_sum(kk * kk, ones_bd), 1e-24))
    k2 = k * (1.0 + (a - 1.0) * k_a)
    bonus = _head_sum(r * k2 * r_k, ones_bd) * v
    return r, logdecay, k2, v, kk, a, g, bonus


def _rwkv_post(y, bonus, g, lng, lnb, ones_bd):
    mu = _head_sum(y, ones_bd) * (1.0 / RWKV_HEAD)
    yc = y - mu
    var = _head_sum(yc * yc, ones_bd) * (1.0 / RWKV_HEAD)
    yn = yc * lax.rsqrt(var + RWKV_GN_EPS) * lng + lnb
    return (yn + bonus) * g


_CONV_ROWS = 64


def _prompt_acd_kernel(za_ref, zc_ref, zd_ref, conf0_ref, sc0_ref, pool0_ref,
                       cw_ref, cb_ref, clg_ref, clb_ref, scw_ref, pw_ref, ps_ref,
                       ya_ref, yc_ref, yd_ref, conf_o_ref, sc_o_ref, pool_o_ref,
                       ext_a, ext_c, ext_d, sh_a, *, tt, start_pos):
    t = pl.program_id(1)

    @pl.when(t == 0)
    def _():
        ext_a[0:2, :] = jnp.zeros((2, GROUP_W), F32)
        ext_a[2:32, :] = conf0_ref[0]
        ext_a[tt + 32:tt + 40, :] = jnp.zeros((8, GROUP_W), F32)
        ext_c[0:6, :] = jnp.zeros((6, GROUP_W), F32)
        ext_c[6:8, :] = sc0_ref[0]
        ext_d[0:1, :] = jnp.zeros((1, GROUP_W), F32)
        ext_d[1:16, :] = pool0_ref[0]

    za = za_ref[0]
    ext_a[32:32 + tt, :] = za[:, 0:GROUP_W] * _sigmoid(za[:, GROUP_W:2 * GROUP_W])
    for r in range(1, 8):
        sh_a[r - 1] = ext_a[r:r + tt + 32, :]
    for c in range(tt // _CONV_ROWS):
        base = c * _CONV_ROWS
        acc = jnp.zeros((_CONV_ROWS, GROUP_W), F32) + cb_ref[...]
        for k in range(CONF_K):
            off = base + k + 2
            r = off % 8
            win = (ext_a[off:off + _CONV_ROWS, :] if r == 0
                   else sh_a[r - 1, off - r:off - r + _CONV_ROWS, :])
            acc = acc + cw_ref[k:k + 1, :] * win
        ya_ref[0, base:base + _CONV_ROWS, :] = _conformer_tail(acc, clg_ref[...], clb_ref[...])
    conf_o_ref[0] = ext_a[tt + 2:tt + 32, :]
    ext_a[0:32, :] = ext_a[tt:tt + 32, :]

    zc = zc_ref[0]
    ext_c[8:8 + tt, :] = zc[:, GROUP_W:2 * GROUP_W] * zc[:, 2 * GROUP_W:3 * GROUP_W]
    cc = (scw_ref[0:1, :] * ext_c[6:6 + tt, :] + scw_ref[1:2, :] * ext_c[7:7 + tt, :]
          + scw_ref[2:3, :] * ext_c[8:8 + tt, :])
    yc_ref[0] = zc[:, 0:GROUP_W] * cc
    sc_o_ref[0] = ext_c[tt + 6:tt + 8, :]
    ext_c[0:8, :] = ext_c[tt:tt + 8, :]

    u = zd_ref[0]
    ext_d[16:16 + tt, :] = u
    w2 = u + ext_d[15:15 + tt, :]
    w4 = w2 + ext_d[14:14 + tt, :] + ext_d[13:13 + tt, :]
    w8 = w4
    for j in range(4, 8):
        w8 = w8 + ext_d[16 - j:16 - j + tt, :]
    w16 = w8
    for j in range(8, 16):
        w16 = w16 + ext_d[16 - j:16 - j + tt, :]
    wsum, win = _pool_select(w2, w4, w8, w16, (tt, GROUP_W))
    pos = (lax.broadcasted_iota(jnp.int32, (tt, GROUP_W), 0) + (t * tt + start_pos + 1)).astype(F32)
    pooled = wsum / jnp.minimum(win, pos) - u
    yd_ref[0] = jnp.dot(pooled.astype(BF16), pw_ref[...], preferred_element_type=F32) * ps_ref[...]
    pool_o_ref[0] = ext_d[tt + 1:tt + 16, :]
    ext_d[0:16, :] = ext_d[tt:tt + 16, :]


def _prompt_acd(za, zc, zd, conf0, sc0, pool0, wl, tt, start_pos):
    b, t, _ = za.shape
    tile = lambda w: pl.BlockSpec((1, tt, w), lambda i, j: (i, j, 0))
    st = lambda r: pl.BlockSpec((1, r, GROUP_W), lambda i, j: (i, 0, 0))
    row = _full((1, GROUP_W))
    return pl.pallas_call(
        functools.partial(_prompt_acd_kernel, tt=tt, start_pos=start_pos),
        grid=(b, t // tt),
        in_specs=[tile(512), tile(768), tile(256), st(30), st(2), st(15),
                  _full((CONF_K, GROUP_W)), row, row, row, _full((3, GROUP_W)), _full((GROUP_W, GROUP_W)), row],
        out_specs=[tile(256), tile(256), tile(256), st(30), st(2), st(15)],
        out_shape=[jax.ShapeDtypeStruct((b, t, GROUP_W), F32)] * 3
        + [jax.ShapeDtypeStruct((b, r, GROUP_W), F32) for r in (30, 2, 15)],
        scratch_shapes=[pltpu.VMEM((40 + tt, GROUP_W), F32), pltpu.VMEM((8 + tt, GROUP_W), F32),
                        pltpu.VMEM((16 + tt, GROUP_W), F32), pltpu.VMEM((7, 32 + tt, GROUP_W), F32)],
        compiler_params=_cparams("parallel", "arbitrary"),
        name="prompt_acd",
    )(za, zc, zd, conf0, sc0, pool0, wl["conf_dw_w"], wl["conf_dw_b"], wl["conf_ln_g"], wl["conf_ln_b"],
      wl["sc_conv_w"], wl["pool_wbd"], wl["pool_scale"])


def _prompt_rwkv_kernel(zb_ref, shift0_ref, wkv0_ref, mu_ref, w0_ref, w2_ref, a0_ref, a2_ref, g2_ref,
                        kk_ref, ka_ref, rk_ref, lng_ref, lnb_ref, ones_ref, tri_ref,
                        yb_ref, shift_o_ref, wkv_o_ref, prev_s, st_s, *, c, nb):
    t = pl.program_id(1)
    nt = pl.num_programs(1)

    @pl.when(t == 0)
    def _():
        for bb in range(nb):
            prev_s[bb] = jnp.broadcast_to(shift0_ref[bb], prev_s.shape[1:])
            for h in range(RWKV_H):
                st_s[bb, h] = _transpose_mxu(wkv0_ref[bb, h], _eye(RWKV_HEAD, BF16))

    new_states = _rwkv_chunks(zb_ref, mu_ref, w0_ref, w2_ref, a0_ref, a2_ref, g2_ref, kk_ref, ka_ref, rk_ref,
                              lng_ref, lnb_ref, ones_ref, tri_ref, yb_ref, shift_o_ref, prev_s, st_s, c, nb)

    @pl.when(t == nt - 1)
    def _():
        eye_h = _eye(RWKV_HEAD, BF16)
        for bb in range(nb):
            for h in range(RWKV_H):
                wkv_o_ref[bb, h] = _transpose_mxu(new_states[(bb, h)], eye_h)


def _rwkv_chunks(zb_ref, mu_ref, w0_ref, w2_ref, a0_ref, a2_ref, g2_ref, kk_ref, ka_ref, rk_ref,
                 lng_ref, lnb_ref, ones_ref, tri_ref, yb_ref, shift_o_ref, prev_s, st_s, c, nb):
    p = zb_ref[...].reshape(nb * c, RWKV_COLS)
    row = lax.broadcasted_iota(jnp.int32, p.shape, 0)
    prev = pltpu.roll(p, 1, axis=0)
    for bb in range(nb):
        prev = jnp.where(row == bb * c, prev_s[bb, 0:1, :], prev)
    ones_bd = ones_ref[...]
    wr = (mu_ref[...], w0_ref[...], w2_ref[...], a0_ref[...], a2_ref[...], g2_ref[...],
          kk_ref[...], ka_ref[...], rk_ref[...], ones_bd)
    r, logdecay, k2, v, kk, a, g, bonus = _rwkv_prep(p, prev, wr)

    cum = _mm_exact_lhs(tri_ref[...], logdecay)
    g_end = [cum[(bb + 1) * c - 1:(bb + 1) * c, :] for bb in range(nb)]
    cum_end = jnp.concatenate([jnp.broadcast_to(ge, (c, GROUP_W)) for ge in g_end], axis=0)
    g_end = [jnp.exp(ge) for ge in g_end]
    e_neg = jnp.exp(-cum)
    e_end = jnp.exp(cum_end - cum)
    bvec = kk * a
    a_t = -kk * jnp.exp(cum - logdecay)
    r_t = r * jnp.exp(cum)
    b_h = bvec * e_neg
    k_h = k2 * e_neg
    b_e = bvec * e_end
    k_e = k2 * e_end

    ri = lax.broadcasted_iota(jnp.int32, (2 * c, 2 * c), 0)
    ci = lax.broadcasted_iota(jnp.int32, (2 * c, 2 * c), 1)
    rt, cs = ri & (c - 1), ci & (c - 1)
    keep = (rt > cs) | ((ri >= c) & (rt == cs))
    eye_2h = _eye(2 * RWKV_HEAD, BF16)
    eye_f = _eye(RWKV_HEAD, F32)

    chains = [(bb, h) for bb in range(nb) for h in range(RWKV_H)]

    def part(z, bb, h):
        return z[bb * c:(bb + 1) * c, h * RWKV_HEAD:(h + 1) * RWKV_HEAD]

    big = {ch: jnp.where(keep, _mm1(jnp.concatenate([part(a_t, *ch), part(r_t, *ch)], axis=0),
                                    jnp.concatenate([part(b_h, *ch), part(k_h, *ch)], axis=0), _NT), 0.0)
           for ch in chains}
    s0 = {ch: st_s[ch[0], ch[1]] for ch in chains}
    vh = {ch: part(v, *ch) for ch in chains}
    sa = {ch: _mm1(jnp.concatenate([part(a_t, *ch), big[ch][0:c, c:2 * c]], axis=1),
                   jnp.concatenate([s0[ch], vh[ch]], axis=0)) for ch in chains}
    x = {ch: big[ch][0:c, 0:c] for ch in chains}
    n_sq = int(np.log2(c))
    for step in range(n_sq):
        sa = {ch: sa[ch] + _mm1(x[ch], sa[ch]) for ch in chains}
        if step + 1 < n_sq:
            x = {ch: _mm1(x[ch], x[ch]) for ch in chains}
    bk = {ch: jnp.concatenate([part(b_e, *ch), part(k_e, *ch)], axis=1) for ch in chains}
    bk_t = {ch: _dg(eye_2h, bk[ch].astype(BF16), _NT) for ch in chains}
    y_h = {ch: _mm1(jnp.concatenate([part(r_t, *ch), big[ch][c:2 * c, 0:c], big[ch][c:2 * c, c:2 * c]], axis=1),
                    jnp.concatenate([s0[ch], sa[ch], vh[ch]], axis=0)) for ch in chains}
    s_new = {}
    for ch in chains:
        bb, h = ch
        g_h = g_end[bb][:, h * RWKV_HEAD:(h + 1) * RWKV_HEAD]
        s_new[ch] = _mm3(eye_f * g_h, s0[ch]) + _mm1(
            jnp.concatenate([bk_t[ch][0:RWKV_HEAD], bk_t[ch][RWKV_HEAD:]], axis=1),
            jnp.concatenate([sa[ch], vh[ch]], axis=0))
        st_s[bb, h] = s_new[ch]

    y = jnp.concatenate([jnp.concatenate([y_h[(bb, h)] for h in range(RWKV_H)], axis=-1) for bb in range(nb)], axis=0)
    yb = _rwkv_post(y, bonus, g, lng_ref[...], lnb_ref[...], ones_bd)
    for bb in range(nb):
        yb_ref[bb] = yb[bb * c:(bb + 1) * c]
        last = p[(bb + 1) * c - 1:(bb + 1) * c, :]
        prev_s[bb] = jnp.broadcast_to(last, prev_s.shape[1:])
        shift_o_ref[bb] = last
    return s_new


def _rwkv_weight_args(wl):
    return (wl["rwkv_mu"], wl["rwkv_w0"], wl["rwkv_w2"], wl["rwkv_a0"], wl["rwkv_a2"], wl["rwkv_g2"],
            wl["rwkv_k_k"], wl["rwkv_k_a"], wl["rwkv_r_k"], wl["rwkv_ln_g"], wl["rwkv_ln_b"], wl["ones_bd"])


_RWKV_WEIGHT_SPECS = [(1, RWKV_COLS), (1, GROUP_W), (32, GROUP_W), (1, GROUP_W), (32, GROUP_W), (64, GROUP_W),
                      (1, GROUP_W), (1, GROUP_W), (1, GROUP_W), (1, GROUP_W), (1, GROUP_W), (GROUP_W, GROUP_W)]


def _prompt_rwkv(zb, shift0, wkv0, wl, nb):
    b, t, _ = zb.shape
    c = WKV_CHUNK
    nb = min(nb, b)
    tri = jnp.kron(jnp.eye(nb, dtype=F32), jnp.tril(jnp.ones((c, c), F32))).astype(BF16)
    return pl.pallas_call(
        functools.partial(_prompt_rwkv_kernel, c=c, nb=nb),
        grid=(b // nb, t // c),
        in_specs=[pl.BlockSpec((nb, c, RWKV_COLS), lambda i, j: (i, j, 0)),
                  pl.BlockSpec((nb, 1, RWKV_COLS), lambda i, j: (i, 0, 0)),
                  pl.BlockSpec((nb, RWKV_H, RWKV_HEAD, RWKV_HEAD), lambda i, j: (i, 0, 0, 0))]
        + [_full(s) for s in _RWKV_WEIGHT_SPECS] + [_full((nb * c, nb * c))],
        out_specs=[pl.BlockSpec((nb, c, GROUP_W), lambda i, j: (i, j, 0)),
                   pl.BlockSpec((nb, 1, RWKV_COLS), lambda i, j: (i, 0, 0)),
                   pl.BlockSpec((nb, RWKV_H, RWKV_HEAD, RWKV_HEAD), lambda i, j: (i, 0, 0, 0))],
        out_shape=[jax.ShapeDtypeStruct((b, t, GROUP_W), F32), jax.ShapeDtypeStruct((b, 1, RWKV_COLS), F32),
                   jax.ShapeDtypeStruct((b, RWKV_H, RWKV_HEAD, RWKV_HEAD), F32)],
        scratch_shapes=[pltpu.VMEM((nb, 8, RWKV_COLS), F32), pltpu.VMEM((nb, RWKV_H, RWKV_HEAD, RWKV_HEAD), F32)],
        compiler_params=_cparams("parallel", "arbitrary"),
        name="prompt_rwkv",
    )(zb, shift0, wkv0, *_rwkv_weight_args(wl), tri)


def _decode_acd_prep_kernel(za_ref, zb_ref, zc_ref, zd_ref, conf_ref, shift_ref, sc_ref, pool_ref,
                            cw_ref, cb_ref, clg_ref, clb_ref, scw_ref, pw_ref, ps_ref,
                            mu_ref, w0_ref, w2_ref, a0_ref, a2_ref, g2_ref, kk_ref, ka_ref, rk_ref, ones_ref,
                            ya_ref, yc_ref, yd_ref, conf_o_ref, sc_o_ref, pool_o_ref,
                            r_o, w_o, k_o, v_o, kkn_o, b_o, g_o, bonus_o, *, start_pos):
    za = za_ref[...]
    glu = za[:, 0:GROUP_W] * _sigmoid(za[:, GROUP_W:2 * GROUP_W])
    acc = cb_ref[...] + cw_ref[CONF_K - 1:CONF_K, :] * glu
    for k in range(CONF_K - 1):
        acc = acc + cw_ref[k:k + 1, :] * conf_ref[k]
    ya_ref[...] = _conformer_tail(acc, clg_ref[...], clb_ref[...])
    for k in range(CONF_K - 2):
        conf_o_ref[k] = conf_ref[k + 1]
    conf_o_ref[CONF_K - 2] = glu

    zc = zc_ref[...]
    u = zc[:, GROUP_W:2 * GROUP_W] * zc[:, 2 * GROUP_W:3 * GROUP_W]
    cc = scw_ref[0:1, :] * sc_ref[0] + scw_ref[1:2, :] * sc_ref[1] + scw_ref[2:3, :] * u
    yc_ref[...] = zc[:, 0:GROUP_W] * cc
    sc_o_ref[0] = sc_ref[1]
    sc_o_ref[1] = u

    d = zd_ref[...]
    w2 = d + pool_ref[POOL_BUF - 1]
    w4 = w2 + pool_ref[POOL_BUF - 2] + pool_ref[POOL_BUF - 3]
    w8 = w4
    for j in range(4, 8):
        w8 = w8 + pool_ref[POOL_BUF - j]
    w16 = w8
    for j in range(8, 16):
        w16 = w16 + pool_ref[POOL_BUF - j]
    wsum, win = _pool_select(w2, w4, w8, w16, d.shape)
    pooled = wsum / jnp.minimum(win, float(start_pos + 1)) - d
    yd_ref[...] = jnp.dot(pooled.astype(BF16), pw_ref[...], preferred_element_type=F32) * ps_ref[...]
    for k in range(POOL_BUF - 1):
        pool_o_ref[k] = pool_ref[k + 1]
    pool_o_ref[POOL_BUF - 1] = d

    wr = (mu_ref[...], w0_ref[...], w2_ref[...], a0_ref[...], a2_ref[...], g2_ref[...],
          kk_ref[...], ka_ref[...], rk_ref[...], ones_ref[...])
    r, logdecay, k2, v, kk, a, g, bonus = _rwkv_prep(zb_ref[...], shift_ref[...], wr)
    r_o[...] = r
    w_o[...] = jnp.exp(logdecay)
    k_o[...] = k2
    v_o[...] = v
    kkn_o[...] = kk
    b_o[...] = kk * a
    g_o[...] = g
    bonus_o[...] = bonus


def _decode_acd_prep(za, zb, zc, zd, conf_t, shift, sc_t, pool_t, wl, start_pos):
    n = za.shape[0]
    ins = (za, zb, zc, zd, conf_t, shift, sc_t, pool_t, wl["conf_dw_w"], wl["conf_dw_b"], wl["conf_ln_g"],
           wl["conf_ln_b"], wl["sc_conv_w"], wl["pool_wbd"], wl["pool_scale"]) + _rwkv_weight_args(wl)[:9] + (wl["ones_bd"],)
    vec = jax.ShapeDtypeStruct((n, GROUP_W), F32)
    outs = [vec, vec, vec, jax.ShapeDtypeStruct(conf_t.shape, F32), jax.ShapeDtypeStruct(sc_t.shape, F32),
            jax.ShapeDtypeStruct(pool_t.shape, F32)] + [vec] * 8
    return pl.pallas_call(
        functools.partial(_decode_acd_prep_kernel, start_pos=start_pos),
        in_specs=[_full(x.shape) for x in ins],
        out_specs=[_full(o.shape) for o in outs],
        out_shape=outs,
        compiler_params=_cparams(),
        name="decode_acd_prep",
    )(*ins)


def _decode_wkv_kernel(s_ref, w_ref, kk_ref, b_ref, k_ref, r_ref, v_ref, s_o_ref, y_o_ref):
    s = s_ref[...]
    sa = -jnp.sum(s * kk_ref[...], axis=-1, keepdims=True)
    s_new = s * w_ref[...] + sa * b_ref[...] + v_ref[...] * k_ref[...]
    s_o_ref[...] = s_new
    y_o_ref[...] = jnp.sum(s_new * r_ref[...], axis=-1, keepdims=True)


def _decode_wkv(s, w, kk, bvec, k, r, v, blk):
    bh = s.shape[0]
    lane = pl.BlockSpec((blk, 1, RWKV_HEAD), lambda i: (i, 0, 0))
    col = pl.BlockSpec((blk, RWKV_HEAD, 1), lambda i: (i, 0, 0))
    mat = pl.BlockSpec((blk, RWKV_HEAD, RWKV_HEAD), lambda i: (i, 0, 0))
    return pl.pallas_call(
        _decode_wkv_kernel,
        grid=(bh // blk,),
        in_specs=[mat, lane, lane, lane, lane, lane, col],
        out_specs=[mat, col],
        out_shape=[jax.ShapeDtypeStruct(s.shape, F32), jax.ShapeDtypeStruct((bh, RWKV_HEAD, 1), F32)],
        compiler_params=_cparams("parallel"),
        name="decode_wkv",
    )(s, w, kk, bvec, k, r, v)


def _decode_post_kernel(y_ref, bonus_ref, g_ref, lng_ref, lnb_ref, ones_ref, o_ref):
    o_ref[...] = _rwkv_post(y_ref[...], bonus_ref[...], g_ref[...], lng_ref[...], lnb_ref[...], ones_ref[...])


def _decode_post(y, bonus, g, wl):
    ins = (y, bonus, g, wl["rwkv_ln_g"], wl["rwkv_ln_b"], wl["ones_bd"])
    return pl.pallas_call(
        _decode_post_kernel,
        in_specs=[_full(x.shape) for x in ins],
        out_specs=_full(y.shape),
        out_shape=jax.ShapeDtypeStruct(y.shape, F32),
        compiler_params=_cparams(),
        name="decode_post",
    )(*ins)


_CELLS = [(a, b) for a in range(PEER_TOPK) for b in range(PEER_TOPK) if (a + 1) * (b + 1) <= PEER_TOPK]
_CELL_PAIRS = [(c, d) for c in _CELLS for d in _CELLS if d[0] < c[0] and d[1] > c[1]]
_PAIRS_AS_D = {x: sum(1 for _, d in _CELL_PAIRS if d == x) for x in _CELLS}


def _top16_rows(s, tb, vals_ref, h, tie_safe, want_rank):
    lanes = 128
    iota = lax.broadcasted_iota(jnp.int32, (N_KEYS, lanes), 0).astype(F32)
    ranks, counts = [], []
    for c0 in range(0, tb, lanes):
        sc = s[:, c0:c0 + lanes]
        rank = jnp.full((N_KEYS, lanes), float(PEER_TOPK), F32)
        for r in range(PEER_TOPK):
            m = jnp.max(sc, axis=0, keepdims=True)
            sel = sc == m
            if tie_safe:
                sel = iota == jnp.min(jnp.where(sel, iota, float(N_KEYS)), axis=0, keepdims=True)
            if want_rank:
                rank = jnp.where(sel, float(r), rank)
            sc = jnp.where(sel, -jnp.inf, sc)
            vals_ref[r, h:h + 1, c0:c0 + lanes] = m
        ranks.append(rank)
        counts.append(jnp.sum(jnp.where(sc == -jnp.inf, 1.0, 0.0), axis=0, keepdims=True))
    cat = lambda xs: jnp.concatenate(xs, axis=1) if len(xs) > 1 else xs[0]
    return (cat(ranks) if want_rank else None), cat(counts)


def _top16_values(s, tb, vals_ref, h):
    lanes, nv = 128, N_KEYS // 8
    ties = []
    for c0 in range(0, tb, lanes):
        v = [s[8 * k:8 * (k + 1), c0:c0 + lanes] for k in range(nv)]
        k = 2
        while k <= nv:
            j = k // 2
            while j >= 1:
                for i in range(nv):
                    l = i ^ j
                    if l > i:
                        hi, lo = jnp.maximum(v[i], v[l]), jnp.minimum(v[i], v[l])
                        v[i], v[l] = (hi, lo) if (i & k) == 0 else (lo, hi)
                j //= 2
            k *= 2
        tie = jnp.zeros((1, lanes), F32)
        m = None
        for r in range(PEER_TOPK + 1):
            prev = m
            m = jnp.max(v[0], axis=0, keepdims=True)
            if r > 0:
                tie = tie + jnp.where(m == prev, 1.0, 0.0)
            if r == PEER_TOPK:
                break
            vals_ref[r, h:h + 1, c0:c0 + lanes] = m
            hit = v[0] == m
            tie = tie + jnp.where(jnp.sum(jnp.where(hit, 1.0, 0.0), axis=0, keepdims=True) > 1.0, 1.0, 0.0)
            depth = PEER_TOPK - r
            for d in range(depth):
                v[d] = jnp.where(hit, v[d + 1] if d + 1 < nv else -jnp.inf, v[d])
        ties.append(tie)
    return jnp.concatenate(ties, axis=1) if len(ties) > 1 else ties[0]


def _peer_route_kernel(*refs, tb):
    n_bad = _peer_route_body(*refs, tb=tb, tie_safe=False)

    @pl.when(jnp.max(n_bad) > 0.0)
    def _():
        _peer_route_body(*refs, tb=tb, tie_safe=True)


def _peer_route_body(x_ref, ya_ref, yb_ref, yc_ref, yd_ref, wo_ref, g_ref, wq_ref, k1_ref, k2_ref,
                     hres_ref, xn_ref, r2_ref, p2_ref, c1_ref, p1_ref,
                     v1_s, v2_s, r1_s, cnt_s, *, tb, tie_safe):
    hres = x_ref[...]
    for i, y_ref in enumerate((ya_ref, yb_ref, yc_ref, yd_ref)):
        hres = hres + jnp.dot(y_ref[...].astype(BF16), wo_ref[i * GROUP_W:(i + 1) * GROUP_W, :],
                              preferred_element_type=F32)
    hres_ref[...] = hres
    xn = _rmsnorm(hres, g_ref[...])
    xb = xn.astype(BF16)
    xn_ref[...] = xn.T.astype(BF16)
    n_bad = jnp.zeros((1, tb), F32)
    for h in range(PEER_HEADS):
        q = jnp.dot(xb, wq_ref[:, h * 256:(h + 1) * 256], preferred_element_type=F32).astype(BF16)
        s1 = _dg(k1_ref[h], q[:, 0:128], _NT)
        s2 = _dg(k2_ref[h], q[:, 128:256], _NT)
        if tie_safe:
            r1, _ = _top16_rows(s1, tb, v1_s, h, True, want_rank=True)
        else:
            r1 = s1
            n_bad = n_bad + _top16_values(s1, tb, v1_s, h)
        r2, n2 = _top16_rows(s2, tb, v2_s, h, tie_safe, want_rank=True)
        n_bad = n_bad + jnp.where(n2 != float(PEER_TOPK), 1.0, 0.0)
        r1_s[h] = r1
        r2_ref[h] = r2.astype(BF16)
        p1_ref[h] = jnp.exp(s1 - v1_s[0, h:h + 1, :])
        p2_ref[h] = jnp.exp(s2 - v2_s[0, h:h + 1, :]).astype(BF16)

    sums = {c: v1_s[c[0]] + v2_s[c[1]] for c in _CELLS}
    rank = {c: jnp.full((PEER_HEADS, tb), float((c[0] + 1) * (c[1] + 1) - 1 + _PAIRS_AS_D[c]), F32) for c in _CELLS}
    for c, d in _CELL_PAIRS:
        won = jnp.where(sums[d] >= sums[c], 1.0, 0.0)
        rank[c] = rank[c] + won
        rank[d] = rank[d] - won
    e1 = [jnp.exp(v1_s[a] - v1_s[0]) for a in range(PEER_TOPK)]
    e2 = [jnp.exp(v2_s[b] - v2_s[0]) for b in range(PEER_TOPK)]
    z = jnp.zeros((PEER_HEADS, tb), F32)
    cnt = [jnp.zeros((PEER_HEADS, tb), F32) for _ in range(PEER_TOPK)]
    for c in _CELLS:
        sel = rank[c] < float(PEER_TOPK)
        cnt[c[0]] = cnt[c[0]] + jnp.where(sel, 1.0, 0.0)
        z = z + jnp.where(sel, e1[c[0]] * e2[c[1]], 0.0)
    for a in range(PEER_TOPK):
        cnt_s[a] = cnt[a]
    cnt_s[PEER_TOPK] = 0.5 / z

    for h in range(PEER_HEADS):
        r1 = r1_s[h]
        c1 = jnp.zeros((N_KEYS, tb), F32)
        for a in range(PEER_TOPK):
            hit = r1 == (float(a) if tie_safe else v1_s[a, h:h + 1, :])
            c1 = jnp.where(hit, cnt_s[a, h:h + 1, :], c1)
        c1_ref[h] = c1
        p1_ref[h] = p1_ref[h] * cnt_s[PEER_TOPK, h:h + 1, :]
    return n_bad


def _peer_route(x, ys, wo_bf16, g, wq_bf16, k1_bf16, k2_bf16, tb):
    n, d = x.shape
    tb = min(tb, n)
    row = lambda i: (i, 0)
    gate = pl.BlockSpec((PEER_HEADS, N_KEYS, tb), lambda i: (0, 0, i))
    gshape = lambda dt: jax.ShapeDtypeStruct((PEER_HEADS, N_KEYS, n), dt)
    return pl.pallas_call(
        functools.partial(_peer_route_kernel, tb=tb),
        grid=(n // tb,),
        in_specs=[pl.BlockSpec((tb, d), row)] + [pl.BlockSpec((tb, GROUP_W), row)] * 4
        + [_full((d, d)), _full((1, d)), _full(wq_bf16.shape), _full(k1_bf16.shape), _full(k2_bf16.shape)],
        out_specs=[pl.BlockSpec((tb, d), row), pl.BlockSpec((d, tb), lambda i: (0, i)), gate, gate, gate, gate],
        out_shape=[jax.ShapeDtypeStruct((n, d), F32), jax.ShapeDtypeStruct((d, n), BF16),
                   gshape(BF16), gshape(BF16), gshape(F32), gshape(F32)],
        scratch_shapes=[pltpu.VMEM((PEER_TOPK, PEER_HEADS, tb), F32), pltpu.VMEM((PEER_TOPK, PEER_HEADS, tb), F32),
                        pltpu.VMEM((PEER_HEADS, N_KEYS, tb), F32), pltpu.VMEM((PEER_TOPK + 1, PEER_HEADS, tb), F32)],
        compiler_params=_cparams("parallel"),
        name="peer_route",
    )(x, *ys, wo_bf16, g.reshape(1, d), wq_bf16, k1_bf16, k2_bf16)


_SQRT_HALF = float(np.sqrt(0.5))


def _peer_expert_kernel(xn_ref, hres_ref, u_ref, vt_ref, r2_ref, p2_ref, c1_ref, p1_ref, fg_ref, o_ref,
                        acc_s, ht_s, at_s, *, eb, final_norm):
    j = pl.program_id(1)
    per = eb // N_KEYS

    @pl.when(j == 0)
    def _():
        acc_s[...] = jnp.zeros_like(acc_s)

    tb = ht_s.shape[1]
    ht_s[...] = jnp.dot(u_ref[...], xn_ref[...], preferred_element_type=F32)
    for i in range(per):
        i1 = j * per + i

        def row_tile(ref, h):
            row = jnp.broadcast_to(ref[h, pl.ds(i1, 1), :], (16, tb)).astype(BF16)
            return jnp.concatenate([row] * (N_KEYS // 16), axis=0)

        gt = None
        for h in range(PEER_HEADS):
            c1 = row_tile(c1_ref, h)
            p1 = row_tile(p1_ref, h)
            term = jnp.where(r2_ref[h] < c1, p2_ref[h], jnp.zeros((), BF16)) * p1
            gt = term if gt is None else gt + term
        ht = ht_s[i * N_KEYS:(i + 1) * N_KEYS, :]
        act = ht * (1.0 + lax.erf(ht * _SQRT_HALF))
        at_s[i * N_KEYS:(i + 1) * N_KEYS, :] = act.astype(BF16) * gt
    acc_s[...] += jnp.dot(vt_ref[0], at_s[...], preferred_element_type=F32)

    @pl.when(j == pl.num_programs(1) - 1)
    def _():
        res = hres_ref[...] + acc_s[...].T
        o_ref[...] = _rmsnorm(res, fg_ref[...]) if final_norm else res


def _peer_experts(xn, hres, u_bf16, vt_bf16, gates, final_g, final_norm, tb, eb):
    n, d = hres.shape
    tb = min(tb, n)
    nblk = u_bf16.shape[0] // eb
    gate = pl.BlockSpec((PEER_HEADS, N_KEYS, tb), lambda i, j: (0, 0, i))
    return pl.pallas_call(
        functools.partial(_peer_expert_kernel, eb=eb, final_norm=final_norm),
        grid=(n // tb, nblk),
        in_specs=[pl.BlockSpec((d, tb), lambda i, j: (0, i)), pl.BlockSpec((tb, d), lambda i, j: (i, 0)),
                  pl.BlockSpec((eb, d), lambda i, j: (j, 0)),
                  pl.BlockSpec((1, d, eb), lambda i, j: (j, 0, 0)),
                  gate, gate, gate, gate, _full((1, d))],
        out_specs=pl.BlockSpec((tb, d), lambda i, j: (i, 0)),
        out_shape=jax.ShapeDtypeStruct((n, d), F32),
        scratch_shapes=[pltpu.VMEM((d, tb), F32), pltpu.VMEM((eb, tb), F32), pltpu.VMEM((eb, tb), BF16)],
        compiler_params=_cparams("parallel", "arbitrary"),
        name="peer_experts",
    )(xn, hres, u_bf16, vt_bf16, *gates, final_g.reshape(1, d))


def _layer_weights(l, w):
    row = lambda a: a[l].reshape(1, -1)
    perm = _RWKV_PERM
    w_in = w["w_in"][l]
    w_in = jnp.concatenate([w_in[:, 0:512], w_in[:, 512:1408][:, perm], w_in[:, 1408:]], axis=1).astype(BF16)
    eye4 = jnp.eye(RWKV_H, dtype=F32)
    ones_bd = jnp.kron(eye4, jnp.ones((RWKV_HEAD, RWKV_HEAD), F32)).astype(BF16)
    pool_wbd = jax.scipy.linalg.block_diag(*[w["pool_w"][l, gi] for gi in range(4)]).astype(BF16)
    return dict(
        norm1_g=w["norm1_g"][l], norm2_g=w["norm2_g"][l], w_in=w_in,
        conf_dw_w=w["conf_dw_w"][l], conf_dw_b=row(w["conf_dw_b"]), conf_ln_g=row(w["conf_ln_g"]),
        conf_ln_b=row(w["conf_ln_b"]), sc_conv_w=w["sc_conv_w"][l], pool_wbd=pool_wbd, pool_scale=row(w["pool_scale"]),
        rwkv_mu=w["rwkv_mu"][l][perm].reshape(1, -1), rwkv_w0=row(w["rwkv_w0"]), rwkv_w2=w["rwkv_w2"][l].astype(BF16),
        rwkv_a0=row(w["rwkv_a0"]), rwkv_a2=w["rwkv_a2"][l].astype(BF16), rwkv_g2=w["rwkv_g2"][l].astype(BF16),
        rwkv_k_k=row(w["rwkv_k_k"]), rwkv_k_a=row(w["rwkv_k_a"]), rwkv_r_k=row(w["rwkv_r_k"]),
        rwkv_ln_g=row(w["rwkv_ln_g"]), rwkv_ln_b=row(w["rwkv_ln_b"]), ones_bd=ones_bd,
        w_out=w["w_out"][l].astype(BF16), peer_wq=w["peer_wq"][l].astype(BF16),
        peer_k1=w["peer_k1"][l].astype(BF16), peer_k2=w["peer_k2"][l].astype(BF16),
        **dict(zip(("peer_u", "peer_vt"), _table_cast(w["peer_u"], w["peer_v"], l, EXPERT_BLOCK))),
    )


def _peer_block(x, ys, wl, final_g, final_norm, tb_route, tb_exp, eb):
    hres, xn, r2, p2, c1, p1 = _peer_route(x, ys, wl["w_out"], wl["norm2_g"], wl["peer_wq"], wl["peer_k1"],
                                           wl["peer_k2"], tb_route)
    return _peer_experts(xn, hres, wl["peer_u"], wl["peer_vt"], (r2, p2, c1, p1), final_g, final_norm, tb_exp, eb)


def _prompt_layer(x, wl, bsz, t, final_g, is_last):
    n = bsz * t
    za, zb, zc, zd = _norm_proj(x, wl["norm1_g"], wl["w_in"], IN_SPLITS, PROJ_ROWS)
    z3 = lambda a: a.reshape(bsz, t, -1)
    zeros = lambda *s: jnp.zeros(s, F32)
    ya, yc, yd, conf, sc, pool = _prompt_acd(z3(za), z3(zc), z3(zd), zeros(bsz, 30, GROUP_W), zeros(bsz, 2, GROUP_W),
                                             zeros(bsz, 15, GROUP_W), wl, MIXER_ROWS, 0)
    yb, shift, wkv = _prompt_rwkv(z3(zb), zeros(bsz, 1, RWKV_COLS), zeros(bsz, RWKV_H, RWKV_HEAD, RWKV_HEAD), wl,
                                   WKV_SEQS_PER_STEP)
    flat = lambda a: a.reshape(n, GROUP_W)
    x = _peer_block(x, (flat(ya), flat(yb), flat(yc), flat(yd)), wl, final_g, is_last, ROUTE_TOKENS, EXPERT_TOKENS,
                    EXPERT_BLOCK)
    return x, (conf, shift.reshape(bsz, RWKV_COLS)[:, _RWKV_INV_PERM], wkv, sc, pool)


def _decode_layer(x, states, wl, start_pos, final_g, is_last):
    conf, shift, wkv, sc, pool = states
    n = x.shape[0]
    za, zb, zc, zd = _norm_proj(x, wl["norm1_g"], wl["w_in"], IN_SPLITS, DECODE_TOKENS)
    tr = lambda a: jnp.transpose(a, (1, 0, 2))
    (ya, yc, yd, conf_n, sc_n, pool_n, r, w, k2, v, kk, bvec, g, bonus) = _decode_acd_prep(
        za, zb, zc, zd, tr(conf), shift[:, _RWKV_PERM], tr(sc), tr(pool), wl, start_pos)
    bh = n * RWKV_H
    lane = lambda a: a.reshape(bh, 1, RWKV_HEAD)
    s_new, y = _decode_wkv(wkv.reshape(bh, RWKV_HEAD, RWKV_HEAD), lane(w), lane(kk), lane(bvec), lane(k2), lane(r),
                           v.reshape(bh, RWKV_HEAD, 1), DECODE_WKV_BLOCK)
    yb = _decode_post(y.reshape(n, GROUP_W), bonus, g, wl)
    x = _peer_block(x, (ya, yb, yc, yd), wl, final_g, is_last, DECODE_TOKENS, DECODE_TOKENS, EXPERT_BLOCK)
    return x, (tr(conf_n), zb[:, _RWKV_INV_PERM], s_new.reshape(n, RWKV_H, RWKV_HEAD, RWKV_HEAD), tr(sc_n), tr(pool_n))


def kernel(x_prompt, x_sample, state_conformer, state_rwkv_shift, state_rwkv_wkv, state_shortconv, state_pool, norm1_g, norm2_g, final_norm_g, w_in, conf_dw_w, conf_dw_b, conf_ln_g, conf_ln_b, rwkv_mu, rwkv_w0, rwkv_w2, rwkv_a0, rwkv_a2, rwkv_g2, rwkv_k_k, rwkv_k_a, rwkv_r_k, rwkv_ln_g, rwkv_ln_b, sc_conv_w, pool_w, pool_scale, w_out, peer_wq, peer_k1, peer_k2, peer_u, peer_v):
    w = dict(norm1_g=norm1_g, norm2_g=norm2_g, w_in=w_in, conf_dw_w=conf_dw_w, conf_dw_b=conf_dw_b,
             conf_ln_g=conf_ln_g, conf_ln_b=conf_ln_b, rwkv_mu=rwkv_mu, rwkv_w0=rwkv_w0, rwkv_w2=rwkv_w2,
             rwkv_a0=rwkv_a0, rwkv_a2=rwkv_a2, rwkv_g2=rwkv_g2, rwkv_k_k=rwkv_k_k, rwkv_k_a=rwkv_k_a,
             rwkv_r_k=rwkv_r_k, rwkv_ln_g=rwkv_ln_g, rwkv_ln_b=rwkv_ln_b, sc_conv_w=sc_conv_w, pool_w=pool_w,
             pool_scale=pool_scale, w_out=w_out, peer_wq=peer_wq, peer_k1=peer_k1, peer_k2=peer_k2,
             peer_u=peer_u, peer_v=peer_v)
    depth = w_in.shape[0]
    bsz, t, d = x_prompt.shape
    nb, dt, _ = x_sample.shape

    xp = x_prompt.reshape(bsz * t, d)
    xs = x_sample.reshape(nb * dt, d)
    p_states, s_states = [], []
    for l in range(depth):
        wl = _layer_weights(l, w)
        last = l == depth - 1
        xp, ps = _prompt_layer(xp, wl, bsz, t, final_norm_g, last)
        xs, ss = _decode_layer(xs, (state_conformer[l], state_rwkv_shift[l], state_rwkv_wkv[l],
                                    state_shortconv[l], state_pool[l]), wl, PAST_LEN, final_norm_g, last)
        p_states.append(ps)
        s_states.append(ss)
    y_prompt = xp.reshape(bsz, t, d)
    y_sample = xs.reshape(nb, dt, d)
    stack = lambda lst, i: jnp.stack([s[i] for s in lst], axis=0)
    conf_p, shift_p, wkv_p, sc_p, pool_p = (stack(p_states, i) for i in range(5))
    conf_s, shift_s, wkv_s, sc_s, pool_s = (stack(s_states, i) for i in range(5))
    return (y_prompt, y_sample, conf_p, conf_s, shift_p, shift_s, wkv_p, wkv_s, sc_p, sc_s, pool_p, pool_s)
```

```python
import functools

import jax
import jax.numpy as jnp
import numpy as np
from jax import lax
from jax.experimental import pallas as pl
from jax.experimental.pallas import tpu as pltpu

F32 = jnp.float32
BF16 = jnp.bfloat16

D_MODEL = 1024
GROUP_W = 256
CONF_K = 31
RWKV_HEAD = 64
RWKV_H = 4
RWKV_COLS = 896
RWKV_GN_EPS = 64e-5
POOL_WINDOWS = (2, 4, 8, 16)
POOL_BUF = 15
N_KEYS = 128
PEER_HEADS = 8
PEER_TOPK = 16
RMS_EPS = 1e-6
LN_EPS = 1e-5
IN_SPLITS = (512, 896, 768, 256)

VMEM_LIMIT_BYTES = 56 * 1024 * 1024
WKV_CHUNK = 64
EXPERT_BLOCK = 2048
TABLE_CAST_ROWS = 512
WKV_SEQS_PER_STEP = 8
PROJ_ROWS = 512
MIXER_ROWS = 256
ROUTE_TOKENS = 256
EXPERT_TOKENS = 512
DECODE_TOKENS = 128
DECODE_WKV_BLOCK = 64
PAST_LEN = 16384

_RWKV_PERM = np.concatenate([np.arange(0, 256), np.arange(288, 544), np.arange(544, 800),
                             np.arange(256, 288), np.arange(800, 832), np.arange(832, 896)])
_RWKV_INV_PERM = np.argsort(_RWKV_PERM)


def _cparams(*sem):
    return pltpu.CompilerParams(dimension_semantics=tuple(sem) if sem else None,
                                vmem_limit_bytes=VMEM_LIMIT_BYTES)


def _full(shape):
    n = len(shape)
    return pl.BlockSpec(shape, lambda *_: (0,) * n)


def _split2(x):
    hi = x.astype(BF16)
    lo = (x - hi.astype(F32)).astype(BF16)
    return hi, lo


def _split3(x):
    hi = x.astype(BF16)
    r = x - hi.astype(F32)
    mid = r.astype(BF16)
    lo = (r - mid.astype(F32)).astype(BF16)
    return hi, mid, lo


_NN = (((1,), (0,)), ((), ()))
_NT = (((1,), (1,)), ((), ()))


def _dg(a, b, dims):
    return lax.dot_general(a, b, dims, preferred_element_type=F32)


def _mm3(a, b, dims=_NN):
    ah, al = _split2(a)
    bh, bl = _split2(b)
    return _dg(ah, bh, dims) + (_dg(al, bh, dims) + _dg(ah, bl, dims))


def _mm1(a, b, dims=_NN):
    return _dg(a.astype(BF16), b.astype(BF16), dims)


def _mm_exact_rhs(a, b_bf16):
    h, m, l = _split3(a)
    return _dg(h, b_bf16, _NN) + (_dg(m, b_bf16, _NN) + _dg(l, b_bf16, _NN))


def _mm_exact_lhs(a_bf16, b):
    h, m, l = _split3(b)
    return _dg(a_bf16, h, _NN) + (_dg(a_bf16, m, _NN) + _dg(a_bf16, l, _NN))


def _transpose_mxu(x, eye_bf16):
    h, m, l = _split3(x)
    return _dg(eye_bf16, h, _NT) + (_dg(eye_bf16, m, _NT) + _dg(eye_bf16, l, _NT))


def _eye(n, dtype):
    return (lax.broadcasted_iota(jnp.int32, (n, n), 0) == lax.broadcasted_iota(jnp.int32, (n, n), 1)).astype(dtype)


def _rmsnorm(x, g):
    ms = jnp.mean(x * x, axis=-1, keepdims=True)
    return x * lax.rsqrt(ms + RMS_EPS) * g


def _sigmoid(x):
    return 1.0 / (1.0 + jnp.exp(-x))


def _softplus(x):
    return jnp.maximum(x, 0.0) + jnp.log(1.0 + jnp.exp(-jnp.abs(x)))


def _norm_proj_kernel(x_ref, g_ref, w_ref, *o_refs, splits):
    xb = _rmsnorm(x_ref[...], g_ref[...]).astype(BF16)
    off = 0
    for o_ref, wd in zip(o_refs, splits):
        o_ref[...] = jnp.dot(xb, w_ref[:, off:off + wd], preferred_element_type=F32)
        off += wd


def _norm_proj(x, g, w_bf16, splits, tm):
    n, d = x.shape
    tm = min(tm, n)
    cols = w_bf16.shape[1]
    return pl.pallas_call(
        functools.partial(_norm_proj_kernel, splits=splits),
        grid=(n // tm,),
        in_specs=[pl.BlockSpec((tm, d), lambda i: (i, 0)), _full((1, d)), _full((d, cols))],
        out_specs=[pl.BlockSpec((tm, wd), lambda i: (i, 0)) for wd in splits],
        out_shape=[jax.ShapeDtypeStruct((n, wd), F32) for wd in splits],
        compiler_params=_cparams("parallel"),
        name="norm_proj",
    )(x, g.reshape(1, d), w_bf16)


def _table_cast_kernel(u_ref, v_ref, ub_ref, vt_ref):
    ub_ref[...] = u_ref[...].astype(BF16)
    vt_ref[0] = v_ref[...].T.astype(BF16)


def _table_cast(u, v, l, eb):
    _, e, d = u.shape
    rows = TABLE_CAST_ROWS
    per = eb // rows
    return pl.pallas_call(
        _table_cast_kernel,
        grid=(e // rows,),
        in_specs=[pl.BlockSpec((None, rows, d), lambda i: (l, i, 0)), pl.BlockSpec((None, rows, d), lambda i: (l, i, 0))],
        out_specs=[pl.BlockSpec((rows, d), lambda i: (i, 0)), pl.BlockSpec((1, d, rows), lambda i: (i // per, 0, i % per))],
        out_shape=[jax.ShapeDtypeStruct((e, d), BF16), jax.ShapeDtypeStruct((e // eb, d, eb), BF16)],
        compiler_params=_cparams("parallel"),
        name="table_cast",
    )(u, v)


def _layernorm_lanes(x, g, b, eps):
    mu = jnp.mean(x, axis=-1, keepdims=True)
    xc = x - mu
    var = jnp.mean(xc * xc, axis=-1, keepdims=True)
    return xc * lax.rsqrt(var + eps) * g + b


def _conformer_tail(ca, lng, lnb):
    y = _layernorm_lanes(ca, lng, lnb, LN_EPS)
    return y * _sigmoid(y)


def _pool_select(w2, w4, w8, w16, shape):
    lane = lax.broadcasted_iota(jnp.int32, shape, len(shape) - 1)
    wsum = jnp.where(lane < 64, w2, jnp.where(lane < 128, w4, jnp.where(lane < 192, w8, w16)))
    win = jnp.where(lane < 64, 2.0, jnp.where(lane < 128, 4.0, jnp.where(lane < 192, 8.0, 16.0)))
    return wsum, win


def _head_sum(x, ones_bd):
    return _mm_exact_rhs(x, ones_bd)


def _rwkv_prep(p, prev, wr):
    (mu, w0, w2, a0, a2, g2, k_k, k_a, r_k, ones_bd) = wr
    xs = p + (prev - p) * mu
    r = xs[:, 0:256]
    k = xs[:, 256:512]
    v = xs[:, 512:768]
    w_lo = xs[:, 768:800]
    a_lo = xs[:, 800:832]
    g_lo = xs[:, 832:896]
    wexp = -_softplus(-(w0 + jnp.dot(jnp.tanh(w_lo).astype(BF16), w2, preferred_element_type=F32))) - 0.5
    logdecay = -jnp.exp(wexp)
    a = _sigmoid(a0 + jnp.dot(a_lo.astype(BF16), a2, preferred_element_type=F32))
    g = jnp.dot(_sigmoid(g_lo).astype(BF16), g2, preferred_element_type=F32)
    kk = k * k_k
    kk = kk * lax.rsqrt(jnp.maximum(_head_sum(kk * kk, ones_bd), 1e-24))
    k2 = k * (1.0 + (a - 1.0) * k_a)
    bonus = _head_sum(r * k2 * r_k, ones_bd) * v
    return r, logdecay, k2, v, kk, a, g, bonus


def _rwkv_post(y, bonus, g, lng, lnb, ones_bd):
    mu = _head_sum(y, ones_bd) * (1.0 / RWKV_HEAD)
    yc = y - mu
    var = _head_sum(yc * yc, ones_bd) * (1.0 / RWKV_HEAD)
    yn = yc * lax.rsqrt(var + RWKV_GN_EPS) * lng + lnb
    return (yn + bonus) * g


_CONV_ROWS = 64


def _prompt_acd_kernel(za_ref, zc_ref, zd_ref, conf0_ref, sc0_ref, pool0_ref,
                       cw_ref, cb_ref, clg_ref, clb_ref, scw_ref, pw_ref, ps_ref,
                       ya_ref, yc_ref, yd_ref, conf_o_ref, sc_o_ref, pool_o_ref,
                       ext_a, ext_c, ext_d, sh_a, *, tt, start_pos):
    t = pl.program_id(1)

    @pl.when(t == 0)
    def _():
        ext_a[0:2, :] = jnp.zeros((2, GROUP_W), F32)
        ext_a[2:32, :] = conf0_ref[0]
        ext_a[tt + 32:tt + 40, :] = jnp.zeros((8, GROUP_W), F32)
        ext_c[0:6, :] = jnp.zeros((6, GROUP_W), F32)
        ext_c[6:8, :] = sc0_ref[0]
        ext_d[0:1, :] = jnp.zeros((1, GROUP_W), F32)
        ext_d[1:16, :] = pool0_ref[0]

    za = za_ref[0]
    ext_a[32:32 + tt, :] = za[:, 0:GROUP_W] * _sigmoid(za[:, GROUP_W:2 * GROUP_W])
    for r in range(1, 8):
        sh_a[r - 1] = ext_a[r:r + tt + 32, :]
    for c in range(tt // _CONV_ROWS):
        base = c * _CONV_ROWS
        acc = jnp.zeros((_CONV_ROWS, GROUP_W), F32) + cb_ref[...]
        for k in range(CONF_K):
            off = base + k + 2
            r = off % 8
            win = (ext_a[off:off + _CONV_ROWS, :] if r == 0
                   else sh_a[r - 1, off - r:off - r + _CONV_ROWS, :])
            acc = acc + cw_ref[k:k + 1, :] * win
        ya_ref[0, base:base + _CONV_ROWS, :] = _conformer_tail(acc, clg_ref[...], clb_ref[...])
    conf_o_ref[0] = ext_a[tt + 2:tt + 32, :]
    ext_a[0:32, :] = ext_a[tt:tt + 32, :]

    zc = zc_ref[0]
    ext_c[8:8 + tt, :] = zc[:, GROUP_W:2 * GROUP_W] * zc[:, 2 * GROUP_W:3 * GROUP_W]
    cc = (scw_ref[0:1, :] * ext_c[6:6 + tt, :] + scw_ref[1:2, :] * ext_c[7:7 + tt, :]
          + scw_ref[2:3, :] * ext_c[8:8 + tt, :])
    yc_ref[0] = zc[:, 0:GROUP_W] * cc
    sc_o_ref[0] = ext_c[tt + 6:tt + 8, :]
    ext_c[0:8, :] = ext_c[tt:tt + 8, :]

    u = zd_ref[0]
    ext_d[16:16 + tt, :] = u
    w2 = u + ext_d[15:15 + tt, :]
    w4 = w2 + ext_d[14:14 + tt, :] + ext_d[13:13 + tt, :]
    w8 = w4
    for j in range(4, 8):
        w8 = w8 + ext_d[16 - j:16 - j + tt, :]
    w16 = w8
    for j in range(8, 16):
        w16 = w16 + ext_d[16 - j:16 - j + tt, :]
    wsum, win = _pool_select(w2, w4, w8, w16, (tt, GROUP_W))
    pos = (lax.broadcasted_iota(jnp.int32, (tt, GROUP_W), 0) + (t * tt + start_pos + 1)).astype(F32)
    pooled = wsum / jnp.minimum(win, pos) - u
    yd_ref[0] = jnp.dot(pooled.astype(BF16), pw_ref[...], preferred_element_type=F32) * ps_ref[...]
    pool_o_ref[0] = ext_d[tt + 1:tt + 16, :]
    ext_d[0:16, :] = ext_d[tt:tt + 16, :]


def _prompt_acd(za, zc, zd, conf0, sc0, pool0, wl, tt, start_pos):
    b, t, _ = za.shape
    tile = lambda w: pl.BlockSpec((1, tt, w), lambda i, j: (i, j, 0))
    st = lambda r: pl.BlockSpec((1, r, GROUP_W), lambda i, j: (i, 0, 0))
    row = _full((1, GROUP_W))
    return pl.pallas_call(
        functools.partial(_prompt_acd_kernel, tt=tt, start_pos=start_pos),
        grid=(b, t // tt),
        in_specs=[tile(512), tile(768), tile(256), st(30), st(2), st(15),
                  _full((CONF_K, GROUP_W)), row, row, row, _full((3, GROUP_W)), _full((GROUP_W, GROUP_W)), row],
        out_specs=[tile(256), tile(256), tile(256), st(30), st(2), st(15)],
        out_shape=[jax.ShapeDtypeStruct((b, t, GROUP_W), F32)] * 3
        + [jax.ShapeDtypeStruct((b, r, GROUP_W), F32) for r in (30, 2, 15)],
        scratch_shapes=[pltpu.VMEM((40 + tt, GROUP_W), F32), pltpu.VMEM((8 + tt, GROUP_W), F32),
                        pltpu.VMEM((16 + tt, GROUP_W), F32), pltpu.VMEM((7, 32 + tt, GROUP_W), F32)],
        compiler_params=_cparams("parallel", "arbitrary"),
        name="prompt_acd",
    )(za, zc, zd, conf0, sc0, pool0, wl["conf_dw_w"], wl["conf_dw_b"], wl["conf_ln_g"], wl["conf_ln_b"],
      wl["sc_conv_w"], wl["pool_wbd"], wl["pool_scale"])


def _prompt_rwkv_kernel(zb_ref, shift0_ref, wkv0_ref, mu_ref, w0_ref, w2_ref, a0_ref, a2_ref, g2_ref,
                        kk_ref, ka_ref, rk_ref, lng_ref, lnb_ref, ones_ref, tri_ref,
                        yb_ref, shift_o_ref, wkv_o_ref, prev_s, st_s, *, c, nb):
    t = pl.program_id(1)
    nt = pl.num_programs(1)

    @pl.when(t == 0)
    def _():
        for bb in range(nb):
            prev_s[bb] = jnp.broadcast_to(shift0_ref[bb], prev_s.shape[1:])
            for h in range(RWKV_H):
                st_s[bb, h] = _transpose_mxu(wkv0_ref[bb, h], _eye(RWKV_HEAD, BF16))

    new_states = _rwkv_chunks(zb_ref, mu_ref, w0_ref, w2_ref, a0_ref, a2_ref, g2_ref, kk_ref, ka_ref, rk_ref,
                              lng_ref, lnb_ref, ones_ref, tri_ref, yb_ref, shift_o_ref, prev_s, st_s, c, nb)

    @pl.when(t == nt - 1)
    def _():
        eye_h = _eye(RWKV_HEAD, BF16)
        for bb in range(nb):
            for h in range(RWKV_H):
                wkv_o_ref[bb, h] = _transpose_mxu(new_states[(bb, h)], eye_h)


def _rwkv_chunks(zb_ref, mu_ref, w0_ref, w2_ref, a0_ref, a2_ref, g2_ref, kk_ref, ka_ref, rk_ref,
                 lng_ref, lnb_ref, ones_ref, tri_ref, yb_ref, shift_o_ref, prev_s, st_s, c, nb):
    p = zb_ref[...].reshape(nb * c, RWKV_COLS)
    row = lax.broadcasted_iota(jnp.int32, p.shape, 0)
    prev = pltpu.roll(p, 1, axis=0)
    for bb in range(nb):
        prev = jnp.where(row == bb * c, prev_s[bb, 0:1, :], prev)
    ones_bd = ones_ref[...]
    wr = (mu_ref[...], w0_ref[...], w2_ref[...], a0_ref[...], a2_ref[...], g2_ref[...],
          kk_ref[...], ka_ref[...], rk_ref[...], ones_bd)
    r, logdecay, k2, v, kk, a, g, bonus = _rwkv_prep(p, prev, wr)

    cum = _mm_exact_lhs(tri_ref[...], logdecay)
    g_end = [cum[(bb + 1) * c - 1:(bb + 1) * c, :] for bb in range(nb)]
    cum_end = jnp.concatenate([jnp.broadcast_to(ge, (c, GROUP_W)) for ge in g_end], axis=0)
    g_end = [jnp.exp(ge) for ge in g_end]
    e_neg = jnp.exp(-cum)
    e_end = jnp.exp(cum_end - cum)
    bvec = kk * a
    a_t = -kk * jnp.exp(cum - logdecay)
    r_t = r * jnp.exp(cum)
    b_h = bvec * e_neg
    k_h = k2 * e_neg
    b_e = bvec * e_end
    k_e = k2 * e_end

    ri = lax.broadcasted_iota(jnp.int32, (2 * c, 2 * c), 0)
    ci = lax.broadcasted_iota(jnp.int32, (2 * c, 2 * c), 1)
    rt, cs = ri & (c - 1), ci & (c - 1)
    keep = (rt > cs) | ((ri >= c) & (rt == cs))
    eye_2h = _eye(2 * RWKV_HEAD, BF16)
    eye_f = _eye(RWKV_HEAD, F32)

    chains = [(bb, h) for bb in range(nb) for h in range(RWKV_H)]

    def part(z, bb, h):
        return z[bb * c:(bb + 1) * c, h * RWKV_HEAD:(h + 1) * RWKV_HEAD]

    big = {ch: jnp.where(keep, _mm1(jnp.concatenate([part(a_t, *ch), part(r_t, *ch)], axis=0),
                                    jnp.concatenate([part(b_h, *ch), part(k_h, *ch)], axis=0), _NT), 0.0)
           for ch in chains}
    s0 = {ch: st_s[ch[0], ch[1]] for ch in chains}
    vh = {ch: part(v, *ch) for ch in chains}
    sa = {ch: _mm1(jnp.concatenate([part(a_t, *ch), big[ch][0:c, c:2 * c]], axis=1),
                   jnp.concatenate([s0[ch], vh[ch]], axis=0)) for ch in chains}
    x = {ch: big[ch][0:c, 0:c] for ch in chains}
    n_sq = int(np.log2(c))
    for step in range(n_sq):
        sa = {ch: sa[ch] + _mm1(x[ch], sa[ch]) for ch in chains}
        if step + 1 < n_sq:
            x = {ch: _mm1(x[ch], x[ch]) for ch in chains}
    bk = {ch: jnp.concatenate([part(b_e, *ch), part(k_e, *ch)], axis=1) for ch in chains}
    bk_t = {ch: _dg(eye_2h, bk[ch].astype(BF16), _NT) for ch in chains}
    y_h = {ch: _mm1(jnp.concatenate([part(r_t, *ch), big[ch][c:2 * c, 0:c], big[ch][c:2 * c, c:2 * c]], axis=1),
                    jnp.concatenate([s0[ch], sa[ch], vh[ch]], axis=0)) for ch in chains}
    s_new = {}
    for ch in chains:
        bb, h = ch
        g_h = g_end[bb][:, h * RWKV_HEAD:(h + 1) * RWKV_HEAD]
        s_new[ch] = _mm3(eye_f * g_h, s0[ch]) + _mm1(
            jnp.concatenate([bk_t[ch][0:RWKV_HEAD], bk_t[ch][RWKV_HEAD:]], axis=1),
            jnp.concatenate([sa[ch], vh[ch]], axis=0))
        st_s[bb, h] = s_new[ch]

    y = jnp.concatenate([jnp.concatenate([y_h[(bb, h)] for h in range(RWKV_H)], axis=-1) for bb in range(nb)], axis=0)
    yb = _rwkv_post(y, bonus, g, lng_ref[...], lnb_ref[...], ones_bd)
    for bb in range(nb):
        yb_ref[bb] = yb[bb * c:(bb + 1) * c]
        last = p[(bb + 1) * c - 1:(bb + 1) * c, :]
        prev_s[bb] = jnp.broadcast_to(last, prev_s.shape[1:])
        shift_o_ref[bb] = last
    return s_new


def _rwkv_weight_args(wl):
    return (wl["rwkv_mu"], wl["rwkv_w0"], wl["rwkv_w2"], wl["rwkv_a0"], wl["rwkv_a2"], wl["rwkv_g2"],
            wl["rwkv_k_k"], wl["rwkv_k_a"], wl["rwkv_r_k"], wl["rwkv_ln_g"], wl["rwkv_ln_b"], wl["ones_bd"])


_RWKV_WEIGHT_SPECS = [(1, RWKV_COLS), (1, GROUP_W), (32, GROUP_W), (1, GROUP_W), (32, GROUP_W), (64, GROUP_W),
                      (1, GROUP_W), (1, GROUP_W), (1, GROUP_W), (1, GROUP_W), (1, GROUP_W), (GROUP_W, GROUP_W)]


def _prompt_rwkv(zb, shift0, wkv0, wl, nb):
    b, t, _ = zb.shape
    c = WKV_CHUNK
    nb = min(nb, b)
    tri = jnp.kron(jnp.eye(nb, dtype=F32), jnp.tril(jnp.ones((c, c), F32))).astype(BF16)
    return pl.pallas_call(
        functools.partial(_prompt_rwkv_kernel, c=c, nb=nb),
        grid=(b // nb, t // c),
        in_specs=[pl.BlockSpec((nb, c, RWKV_COLS), lambda i, j: (i, j, 0)),
                  pl.BlockSpec((nb, 1, RWKV_COLS), lambda i, j: (i, 0, 0)),
                  pl.BlockSpec((nb, RWKV_H, RWKV_HEAD, RWKV_HEAD), lambda i, j: (i, 0, 0, 0))]
        + [_full(s) for s in _RWKV_WEIGHT_SPECS] + [_full((nb * c, nb * c))],
        out_specs=[pl.BlockSpec((nb, c, GROUP_W), lambda i, j: (i, j, 0)),
                   pl.BlockSpec((nb, 1, RWKV_COLS), lambda i, j: (i, 0, 0)),
                   pl.BlockSpec((nb, RWKV_H, RWKV_HEAD, RWKV_HEAD), lambda i, j: (i, 0, 0, 0))],
        out_shape=[jax.ShapeDtypeStruct((b, t, GROUP_W), F32), jax.ShapeDtypeStruct((b, 1, RWKV_COLS), F32),
                   jax.ShapeDtypeStruct((b, RWKV_H, RWKV_HEAD, RWKV_HEAD), F32)],
        scratch_shapes=[pltpu.VMEM((nb, 8, RWKV_COLS), F32), pltpu.VMEM((nb, RWKV_H, RWKV_HEAD, RWKV_HEAD), F32)],
        compiler_params=_cparams("parallel", "arbitrary"),
        name="prompt_rwkv",
    )(zb, shift0, wkv0, *_rwkv_weight_args(wl), tri)


def _decode_acd_prep_kernel(za_ref, zb_ref, zc_ref, zd_ref, conf_ref, shift_ref, sc_ref, pool_ref,
                            cw_ref, cb_ref, clg_ref, clb_ref, scw_ref, pw_ref, ps_ref,
                            mu_ref, w0_ref, w2_ref, a0_ref, a2_ref, g2_ref, kk_ref, ka_ref, rk_ref, ones_ref,
                            ya_ref, yc_ref, yd_ref, conf_o_ref, sc_o_ref, pool_o_ref,
                            r_o, w_o, k_o, v_o, kkn_o, b_o, g_o, bonus_o, *, start_pos):
    za = za_ref[...]
    glu = za[:, 0:GROUP_W] * _sigmoid(za[:, GROUP_W:2 * GROUP_W])
    acc = cb_ref[...] + cw_ref[CONF_K - 1:CONF_K, :] * glu
    for k in range(CONF_K - 1):
        acc = acc + cw_ref[k:k + 1, :] * conf_ref[k]
    ya_ref[...] = _conformer_tail(acc, clg_ref[...], clb_ref[...])
    for k in range(CONF_K - 2):
        conf_o_ref[k] = conf_ref[k + 1]
    conf_o_ref[CONF_K - 2] = glu

    zc = zc_ref[...]
    u = zc[:, GROUP_W:2 * GROUP_W] * zc[:, 2 * GROUP_W:3 * GROUP_W]
    cc = scw_ref[0:1, :] * sc_ref[0] + scw_ref[1:2, :] * sc_ref[1] + scw_ref[2:3, :] * u
    yc_ref[...] = zc[:, 0:GROUP_W] * cc
    sc_o_ref[0] = sc_ref[1]
    sc_o_ref[1] = u

    d = zd_ref[...]
    w2 = d + pool_ref[POOL_BUF - 1]
    w4 = w2 + pool_ref[POOL_BUF - 2] + pool_ref[POOL_BUF - 3]
    w8 = w4
    for j in range(4, 8):
        w8 = w8 + pool_ref[POOL_BUF - j]
    w16 = w8
    for j in range(8, 16):
        w16 = w16 + pool_ref[POOL_BUF - j]
    wsum, win = _pool_select(w2, w4, w8, w16, d.shape)
    pooled = wsum / jnp.minimum(win, float(start_pos + 1)) - d
    yd_ref[...] = jnp.dot(pooled.astype(BF16), pw_ref[...], preferred_element_type=F32) * ps_ref[...]
    for k in range(POOL_BUF - 1):
        pool_o_ref[k] = pool_ref[k + 1]
    pool_o_ref[POOL_BUF - 1] = d

    wr = (mu_ref[...], w0_ref[...], w2_ref[...], a0_ref[...], a2_ref[...], g2_ref[...],
          kk_ref[...], ka_ref[...], rk_ref[...], ones_ref[...])
    r, logdecay, k2, v, kk, a, g, bonus = _rwkv_prep(zb_ref[...], shift_ref[...], wr)
    r_o[...] = r
    w_o[...] = jnp.exp(logdecay)
    k_o[...] = k2
    v_o[...] = v
    kkn_o[...] = kk
    b_o[...] = kk * a
    g_o[...] = g
    bonus_o[...] = bonus


def _decode_acd_prep(za, zb, zc, zd, conf_t, shift, sc_t, pool_t, wl, start_pos):
    n = za.shape[0]
    ins = (za, zb, zc, zd, conf_t, shift, sc_t, pool_t, wl["conf_dw_w"], wl["conf_dw_b"], wl["conf_ln_g"],
           wl["conf_ln_b"], wl["sc_conv_w"], wl["pool_wbd"], wl["pool_scale"]) + _rwkv_weight_args(wl)[:9] + (wl["ones_bd"],)
    vec = jax.ShapeDtypeStruct((n, GROUP_W), F32)
    outs = [vec, vec, vec, jax.ShapeDtypeStruct(conf_t.shape, F32), jax.ShapeDtypeStruct(sc_t.shape, F32),
            jax.ShapeDtypeStruct(pool_t.shape, F32)] + [vec] * 8
    return pl.pallas_call(
        functools.partial(_decode_acd_prep_kernel, start_pos=start_pos),
        in_specs=[_full(x.shape) for x in ins],
        out_specs=[_full(o.shape) for o in outs],
        out_shape=outs,
        compiler_params=_cparams(),
        name="decode_acd_prep",
    )(*ins)


def _decode_wkv_kernel(s_ref, w_ref, kk_ref, b_ref, k_ref, r_ref, v_ref, s_o_ref, y_o_ref):
    s = s_ref[...]
    sa = -jnp.sum(s * kk_ref[...], axis=-1, keepdims=True)
    s_new = s * w_ref[...] + sa * b_ref[...] + v_ref[...] * k_ref[...]
    s_o_ref[...] = s_new
    y_o_ref[...] = jnp.sum(s_new * r_ref[...], axis=-1, keepdims=True)


def _decode_wkv(s, w, kk, bvec, k, r, v, blk):
    bh = s.shape[0]
    lane = pl.BlockSpec((blk, 1, RWKV_HEAD), lambda i: (i, 0, 0))
    col = pl.BlockSpec((blk, RWKV_HEAD, 1), lambda i: (i, 0, 0))
    mat = pl.BlockSpec((blk, RWKV_HEAD, RWKV_HEAD), lambda i: (i, 0, 0))
    return pl.pallas_call(
        _decode_wkv_kernel,
        grid=(bh // blk,),
        in_specs=[mat, lane, lane, lane, lane, lane, col],
        out_specs=[mat, col],
        out_shape=[jax.ShapeDtypeStruct(s.shape, F32), jax.ShapeDtypeStruct((bh, RWKV_HEAD, 1), F32)],
        compiler_params=_cparams("parallel"),
        name="decode_wkv",
    )(s, w, kk, bvec, k, r, v)


def _decode_post_kernel(y_ref, bonus_ref, g_ref, lng_ref, lnb_ref, ones_ref, o_ref):
    o_ref[...] = _rwkv_post(y_ref[...], bonus_ref[...], g_ref[...], lng_ref[...], lnb_ref[...], ones_ref[...])


def _decode_post(y, bonus, g, wl):
    ins = (y, bonus, g, wl["rwkv_ln_g"], wl["rwkv_ln_b"], wl["ones_bd"])
    return pl.pallas_call(
        _decode_post_kernel,
        in_specs=[_full(x.shape) for x in ins],
        out_specs=_full(y.shape),
        out_shape=jax.ShapeDtypeStruct(y.shape, F32),
        compiler_params=_cparams(),
        name="decode_post",
    )(*ins)


_CELLS = [(a, b) for a in range(PEER_TOPK) for b in range(PEER_TOPK) if (a + 1) * (b + 1) <= PEER_TOPK]
_CELL_PAIRS = [(c, d) for c in _CELLS for d in _CELLS if d[0] < c[0] and d[1] > c[1]]
_PAIRS_AS_D = {x: sum(1 for _, d in _CELL_PAIRS if d == x) for x in _CELLS}


def _top16_rows(s, tb, vals_ref, h, tie_safe, want_rank):
    lanes = 128
    iota = lax.broadcasted_iota(jnp.int32, (N_KEYS, lanes), 0).astype(F32)
    ranks, counts = [], []
    for c0 in range(0, tb, lanes):
        sc = s[:, c0:c0 + lanes]
        rank = jnp.full((N_KEYS, lanes), float(PEER_TOPK), F32)
        for r in range(PEER_TOPK):
            m = jnp.max(sc, axis=0, keepdims=True)
            sel = sc == m
            if tie_safe:
                sel = iota == jnp.min(jnp.where(sel, iota, float(N_KEYS)), axis=0, keepdims=True)
            if want_rank:
                rank = jnp.where(sel, float(r), rank)
            sc = jnp.where(sel, -jnp.inf, sc)
            vals_ref[r, h:h + 1, c0:c0 + lanes] = m
        ranks.append(rank)
        counts.append(jnp.sum(jnp.where(sc == -jnp.inf, 1.0, 0.0), axis=0, keepdims=True))
    cat = lambda xs: jnp.concatenate(xs, axis=1) if len(xs) > 1 else xs[0]
    return (cat(ranks) if want_rank else None), cat(counts)


def _top16_values(s, tb, vals_ref, h):
    lanes, nv = 128, N_KEYS // 8
    ties = []
    for c0 in range(0, tb, lanes):
        v = [s[8 * k:8 * (k + 1), c0:c0 + lanes] for k in range(nv)]
        k = 2
        while k <= nv:
            j = k // 2
            while j >= 1:
                for i in range(nv):
                    l = i ^ j
                    if l > i:
                        hi, lo = jnp.maximum(v[i], v[l]), jnp.minimum(v[i], v[l])
                        v[i], v[l] = (hi, lo) if (i & k) == 0 else (lo, hi)
                j //= 2
            k *= 2
        tie = jnp.zeros((1, lanes), F32)
        m = None
        for r in range(PEER_TOPK + 1):
            prev = m
            m = jnp.max(v[0], axis=0, keepdims=True)
            if r > 0:
                tie = tie + jnp.where(m == prev, 1.0, 0.0)
            if r == PEER_TOPK:
                break
            vals_ref[r, h:h + 1, c0:c0 + lanes] = m
            hit = v[0] == m
            tie = tie + jnp.where(jnp.sum(jnp.where(hit, 1.0, 0.0), axis=0, keepdims=True) > 1.0, 1.0, 0.0)
            depth = PEER_TOPK - r
            for d in range(depth):
                v[d] = jnp.where(hit, v[d + 1] if d + 1 < nv else -jnp.inf, v[d])
        ties.append(tie)
    return jnp.concatenate(ties, axis=1) if len(ties) > 1 else ties[0]


def _peer_route_kernel(*refs, tb):
    n_bad = _peer_route_body(*refs, tb=tb, tie_safe=False)

    @pl.when(jnp.max(n_bad) > 0.0)
    def _():
        _peer_route_body(*refs, tb=tb, tie_safe=True)


def _peer_route_body(x_ref, ya_ref, yb_ref, yc_ref, yd_ref, wo_ref, g_ref, wq_ref, k1_ref, k2_ref,
                     hres_ref, xn_ref, r2_ref, p2_ref, c1_ref, p1_ref,
                     v1_s, v2_s, r1_s, cnt_s, *, tb, tie_safe):
    hres = x_ref[...]
    for i, y_ref in enumerate((ya_ref, yb_ref, yc_ref, yd_ref)):
        hres = hres + jnp.dot(y_ref[...].astype(BF16), wo_ref[i * GROUP_W:(i + 1) * GROUP_W, :],
                              preferred_element_type=F32)
    hres_ref[...] = hres
    xn = _rmsnorm(hres, g_ref[...])
    xb = xn.astype(BF16)
    xn_ref[...] = xn.T.astype(BF16)
    n_bad = jnp.zeros((1, tb), F32)
    for h in range(PEER_HEADS):
        q = jnp.dot(xb, wq_ref[:, h * 256:(h + 1) * 256], preferred_element_type=F32).astype(BF16)
        s1 = _dg(k1_ref[h], q[:, 0:128], _NT)
        s2 = _dg(k2_ref[h], q[:, 128:256], _NT)
        if tie_safe:
            r1, _ = _top16_rows(s1, tb, v1_s, h, True, want_rank=True)
        else:
            r1 = s1
            n_bad = n_bad + _top16_values(s1, tb, v1_s, h)
        r2, n2 = _top16_rows(s2, tb, v2_s, h, tie_safe, want_rank=True)
        n_bad = n_bad + jnp.where(n2 != float(PEER_TOPK), 1.0, 0.0)
        r1_s[h] = r1
        r2_ref[h] = r2.astype(BF16)
        p1_ref[h] = jnp.exp(s1 - v1_s[0, h:h + 1, :])
        p2_ref[h] = jnp.exp(s2 - v2_s[0, h:h + 1, :]).astype(BF16)

    sums = {c: v1_s[c[0]] + v2_s[c[1]] for c in _CELLS}
    rank = {c: jnp.full((PEER_HEADS, tb), float((c[0] + 1) * (c[1] + 1) - 1 + _PAIRS_AS_D[c]), F32) for c in _CELLS}
    for c, d in _CELL_PAIRS:
        won = jnp.where(sums[d] >= sums[c], 1.0, 0.0)
        rank[c] = rank[c] + won
        rank[d] = rank[d] - won
    e1 = [jnp.exp(v1_s[a] - v1_s[0]) for a in range(PEER_TOPK)]
    e2 = [jnp.exp(v2_s[b] - v2_s[0]) for b in range(PEER_TOPK)]
    z = jnp.zeros((PEER_HEADS, tb), F32)
    cnt = [jnp.zeros((PEER_HEADS, tb), F32) for _ in range(PEER_TOPK)]
    for c in _CELLS:
        sel = rank[c] < float(PEER_TOPK)
        cnt[c[0]] = cnt[c[0]] + jnp.where(sel, 1.0, 0.0)
        z = z + jnp.where(sel, e1[c[0]] * e2[c[1]], 0.0)
    for a in range(PEER_TOPK):
        cnt_s[a] = cnt[a]
    cnt_s[PEER_TOPK] = 0.5 / z

    half = PEER_TOPK // 2
    if not tie_safe:
        lo = jnp.full((PEER_HEADS, tb), jnp.inf, F32)
        for a in range(half, PEER_TOPK):
            lo = jnp.where(cnt[a] > 0.0, v1_s[a], lo)
        cnt_s[PEER_TOPK + 1] = lo
    for h in range(PEER_HEADS):
        r1 = r1_s[h]
        c1 = jnp.zeros((N_KEYS, tb), F32)
        for a in range(PEER_TOPK if tie_safe else half):
            hit = r1 == (float(a) if tie_safe else v1_s[a, h:h + 1, :])
            c1 = jnp.where(hit, cnt_s[a, h:h + 1, :], c1)
        if not tie_safe:
            in_tail = (r1 <= v1_s[half, h:h + 1, :]) & (r1 >= cnt_s[PEER_TOPK + 1, h:h + 1, :])
            c1 = jnp.where(in_tail, 1.0, c1)
        c1_ref[h] = c1
        p1_ref[h] = p1_ref[h] * cnt_s[PEER_TOPK, h:h + 1, :]
    return n_bad


def _peer_route(x, ys, wo_bf16, g, wq_bf16, k1_bf16, k2_bf16, tb):
    n, d = x.shape
    tb = min(tb, n)
    row = lambda i: (i, 0)
    gate = pl.BlockSpec((PEER_HEADS, N_KEYS, tb), lambda i: (0, 0, i))
    gshape = lambda dt: jax.ShapeDtypeStruct((PEER_HEADS, N_KEYS, n), dt)
    return pl.pallas_call(
        functools.partial(_peer_route_kernel, tb=tb),
        grid=(n // tb,),
        in_specs=[pl.BlockSpec((tb, d), row)] + [pl.BlockSpec((tb, GROUP_W), row)] * 4
        + [_full((d, d)), _full((1, d)), _full(wq_bf16.shape), _full(k1_bf16.shape), _full(k2_bf16.shape)],
        out_specs=[pl.BlockSpec((tb, d), row), pl.BlockSpec((d, tb), lambda i: (0, i)), gate, gate, gate, gate],
        out_shape=[jax.ShapeDtypeStruct((n, d), F32), jax.ShapeDtypeStruct((d, n), BF16),
                   gshape(BF16), gshape(BF16), gshape(F32), gshape(F32)],
        scratch_shapes=[pltpu.VMEM((PEER_TOPK, PEER_HEADS, tb), F32), pltpu.VMEM((PEER_TOPK, PEER_HEADS, tb), F32),
                        pltpu.VMEM((PEER_HEADS, N_KEYS, tb), F32), pltpu.VMEM((PEER_TOPK + 2, PEER_HEADS, tb), F32)],
        compiler_params=_cparams("parallel"),
        name="peer_route",
    )(x, *ys, wo_bf16, g.reshape(1, d), wq_bf16, k1_bf16, k2_bf16)


_SQRT_HALF = float(np.sqrt(0.5))


def _peer_expert_kernel(xn_ref, hres_ref, u_ref, vt_ref, r2_ref, p2_ref, c1_ref, p1_ref, fg_ref, o_ref,
                        acc_s, ht_s, at_s, *, eb, final_norm):
    j = pl.program_id(1)
    per = eb // N_KEYS

    @pl.when(j == 0)
    def _():
        acc_s[...] = jnp.zeros_like(acc_s)

    tb = ht_s.shape[1]
    ht_s[...] = jnp.dot(u_ref[...], xn_ref[...], preferred_element_type=F32)
    for i in range(per):
        i1 = j * per + i

        def row_tile(ref, h):
            row = jnp.broadcast_to(ref[h, pl.ds(i1, 1), :], (16, tb)).astype(BF16)
            return jnp.concatenate([row] * (N_KEYS // 16), axis=0)

        gt = None
        for h in range(PEER_HEADS):
            c1 = row_tile(c1_ref, h)
            p1 = row_tile(p1_ref, h)
            term = jnp.where(r2_ref[h] < c1, p2_ref[h], jnp.zeros((), BF16)) * p1
            gt = term if gt is None else gt + term
        ht = ht_s[i * N_KEYS:(i + 1) * N_KEYS, :]
        act = ht * (1.0 + lax.erf(ht * _SQRT_HALF))
        at_s[i * N_KEYS:(i + 1) * N_KEYS, :] = act.astype(BF16) * gt
    acc_s[...] += jnp.dot(vt_ref[0], at_s[...], preferred_element_type=F32)

    @pl.when(j == pl.num_programs(1) - 1)
    def _():
        res = hres_ref[...] + acc_s[...].T
        o_ref[...] = _rmsnorm(res, fg_ref[...]) if final_norm else res


def _peer_experts(xn, hres, u_bf16, vt_bf16, gates, final_g, final_norm, tb, eb):
    n, d = hres.shape
    tb = min(tb, n)
    nblk = u_bf16.shape[0] // eb
    gate = pl.BlockSpec((PEER_HEADS, N_KEYS, tb), lambda i, j: (0, 0, i))
    return pl.pallas_call(
        functools.partial(_peer_expert_kernel, eb=eb, final_norm=final_norm),
        grid=(n // tb, nblk),
        in_specs=[pl.BlockSpec((d, tb), lambda i, j: (0, i)), pl.BlockSpec((tb, d), lambda i, j: (i, 0)),
                  pl.BlockSpec((eb, d), lambda i, j: (j, 0)),
                  pl.BlockSpec((1, d, eb), lambda i, j: (j, 0, 0)),
                  gate, gate, gate, gate, _full((1, d))],
        out_specs=pl.BlockSpec((tb, d), lambda i, j: (i, 0)),
        out_shape=jax.ShapeDtypeStruct((n, d), F32),
        scratch_shapes=[pltpu.VMEM((d, tb), F32), pltpu.VMEM((eb, tb), F32), pltpu.VMEM((eb, tb), BF16)],
        compiler_params=_cparams("parallel", "arbitrary"),
        name="peer_experts",
    )(xn, hres, u_bf16, vt_bf16, *gates, final_g.reshape(1, d))


def _layer_weights(l, w):
    row = lambda a: a[l].reshape(1, -1)
    perm = _RWKV_PERM
    w_in = w["w_in"][l]
    w_in = jnp.concatenate([w_in[:, 0:512], w_in[:, 512:1408][:, perm], w_in[:, 1408:]], axis=1).astype(BF16)
    eye4 = jnp.eye(RWKV_H, dtype=F32)
    ones_bd = jnp.kron(eye4, jnp.ones((RWKV_HEAD, RWKV_HEAD), F32)).astype(BF16)
    pool_wbd = jax.scipy.linalg.block_diag(*[w["pool_w"][l, gi] for gi in range(4)]).astype(BF16)
    return dict(
        norm1_g=w["norm1_g"][l], norm2_g=w["norm2_g"][l], w_in=w_in,
        conf_dw_w=w["conf_dw_w"][l], conf_dw_b=row(w["conf_dw_b"]), conf_ln_g=row(w["conf_ln_g"]),
        conf_ln_b=row(w["conf_ln_b"]), sc_conv_w=w["sc_conv_w"][l], pool_wbd=pool_wbd, pool_scale=row(w["pool_scale"]),
        rwkv_mu=w["rwkv_mu"][l][perm].reshape(1, -1), rwkv_w0=row(w["rwkv_w0"]), rwkv_w2=w["rwkv_w2"][l].astype(BF16),
        rwkv_a0=row(w["rwkv_a0"]), rwkv_a2=w["rwkv_a2"][l].astype(BF16), rwkv_g2=w["rwkv_g2"][l].astype(BF16),
        rwkv_k_k=row(w["rwkv_k_k"]), rwkv_k_a=row(w["rwkv_k_a"]), rwkv_r_k=row(w["rwkv_r_k"]),
        rwkv_ln_g=row(w["rwkv_ln_g"]), rwkv_ln_b=row(w["rwkv_ln_b"]), ones_bd=ones_bd,
        w_out=w["w_out"][l].astype(BF16), peer_wq=w["peer_wq"][l].astype(BF16),
        peer_k1=w["peer_k1"][l].astype(BF16), peer_k2=w["peer_k2"][l].astype(BF16),
        **dict(zip(("peer_u", "peer_vt"), _table_cast(w["peer_u"], w["peer_v"], l, EXPERT_BLOCK))),
    )


def _peer_block(x, ys, wl, final_g, final_norm, tb_route, tb_exp, eb):
    hres, xn, r2, p2, c1, p1 = _peer_route(x, ys, wl["w_out"], wl["norm2_g"], wl["peer_wq"], wl["peer_k1"],
                                           wl["peer_k2"], tb_route)
    return _peer_experts(xn, hres, wl["peer_u"], wl["peer_vt"], (r2, p2, c1, p1), final_g, final_norm, tb_exp, eb)


def _prompt_layer(x, wl, bsz, t, final_g, is_last):
    n = bsz * t
    za, zb, zc, zd = _norm_proj(x, wl["norm1_g"], wl["w_in"], IN_SPLITS, PROJ_ROWS)
    z3 = lambda a: a.reshape(bsz, t, -1)
    zeros = lambda *s: jnp.zeros(s, F32)
    ya, yc, yd, conf, sc, pool = _prompt_acd(z3(za), z3(zc), z3(zd), zeros(bsz, 30, GROUP_W), zeros(bsz, 2, GROUP_W),
                                             zeros(bsz, 15, GROUP_W), wl, MIXER_ROWS, 0)
    yb, shift, wkv = _prompt_rwkv(z3(zb), zeros(bsz, 1, RWKV_COLS), zeros(bsz, RWKV_H, RWKV_HEAD, RWKV_HEAD), wl,
                                   WKV_SEQS_PER_STEP)
    flat = lambda a: a.reshape(n, GROUP_W)
    x = _peer_block(x, (flat(ya), flat(yb), flat(yc), flat(yd)), wl, final_g, is_last, ROUTE_TOKENS, EXPERT_TOKENS,
                    EXPERT_BLOCK)
    return x, (conf, shift.reshape(bsz, RWKV_COLS)[:, _RWKV_INV_PERM], wkv, sc, pool)


def _decode_layer(x, states, wl, start_pos, final_g, is_last):
    conf, shift, wkv, sc, pool = states
    n = x.shape[0]
    za, zb, zc, zd = _norm_proj(x, wl["norm1_g"], wl["w_in"], IN_SPLITS, DECODE_TOKENS)
    tr = lambda a: jnp.transpose(a, (1, 0, 2))
    (ya, yc, yd, conf_n, sc_n, pool_n, r, w, k2, v, kk, bvec, g, bonus) = _decode_acd_prep(
        za, zb, zc, zd, tr(conf), shift[:, _RWKV_PERM], tr(sc), tr(pool), wl, start_pos)
    bh = n * RWKV_H
    lane = lambda a: a.reshape(bh, 1, RWKV_HEAD)
    s_new, y = _decode_wkv(wkv.reshape(bh, RWKV_HEAD, RWKV_HEAD), lane(w), lane(kk), lane(bvec), lane(k2), lane(r),
                           v.reshape(bh, RWKV_HEAD, 1), DECODE_WKV_BLOCK)
    yb = _decode_post(y.reshape(n, GROUP_W), bonus, g, wl)
    x = _peer_block(x, (ya, yb, yc, yd), wl, final_g, is_last, DECODE_TOKENS, DECODE_TOKENS, EXPERT_BLOCK)
    return x, (tr(conf_n), zb[:, _RWKV_INV_PERM], s_new.reshape(n, RWKV_H, RWKV_HEAD, RWKV_HEAD), tr(sc_n), tr(pool_n))


def kernel(x_prompt, x_sample, state_conformer, state_rwkv_shift, state_rwkv_wkv, state_shortconv, state_pool, norm1_g, norm2_g, final_norm_g, w_in, conf_dw_w, conf_dw_b, conf_ln_g, conf_ln_b, rwkv_mu, rwkv_w0, rwkv_w2, rwkv_a0, rwkv_a2, rwkv_g2, rwkv_k_k, rwkv_k_a, rwkv_r_k, rwkv_ln_g, rwkv_ln_b, sc_conv_w, pool_w, pool_scale, w_out, peer_wq, peer_k1, peer_k2, peer_u, peer_v):
    w = dict(norm1_g=norm1_g, norm2_g=norm2_g, w_in=w_in, conf_dw_w=conf_dw_w, conf_dw_b=conf_dw_b,
             conf_ln_g=conf_ln_g, conf_ln_b=conf_ln_b, rwkv_mu=rwkv_mu, rwkv_w0=rwkv_w0, rwkv_w2=rwkv_w2,
             rwkv_a0=rwkv_a0, rwkv_a2=rwkv_a2, rwkv_g2=rwkv_g2, rwkv_k_k=rwkv_k_k, rwkv_k_a=rwkv_k_a,
             rwkv_r_k=rwkv_r_k, rwkv_ln_g=rwkv_ln_g, rwkv_ln_b=rwkv_ln_b, sc_conv_w=sc_conv_w, pool_w=pool_w,
             pool_scale=pool_scale, w_out=w_out, peer_wq=peer_wq, peer_k1=peer_k1, peer_k2=peer_k2,
             peer_u=peer_u, peer_v=peer_v)
    depth = w_in.shape[0]
    bsz, t, d = x_prompt.shape
    nb, dt, _ = x_sample.shape

    xp = x_prompt.reshape(bsz * t, d)
    xs = x_sample.reshape(nb * dt, d)
    p_states, s_states = [], []
    for l in range(depth):
        wl = _layer_weights(l, w)
        last = l == depth - 1
        xp, ps = _prompt_layer(xp, wl, bsz, t, final_norm_g, last)
        xs, ss = _decode_layer(xs, (state_conformer[l], state_rwkv_shift[l], state_rwkv_wkv[l],
                                    state_shortconv[l], state_pool[l]), wl, PAST_LEN, final_norm_g, last)
        p_states.append(ps)
        s_states.append(ss)
    y_prompt = xp.reshape(bsz, t, d)
    y_sample = xs.reshape(nb, dt, d)
    stack = lambda lst, i: jnp.stack([s[i] for s in lst], axis=0)
    conf_p, shift_p, wkv_p, sc_p, pool_p = (stack(p_states, i) for i in range(5))
    conf_s, shift_s, wkv_s, sc_s, pool_s = (stack(s_states, i) for i in range(5))
    return (y_prompt, y_sample, conf_p, conf_s, shift_p, shift_s, wkv_p, wkv_s, sc_p, sc_s, pool_p, pool_s)
```
